```python
import math
import jax
import jax.numpy as jnp
from jax import lax
import numpy as np

D_MODEL = 1024
BATCH = 4
SEQ = 8192
DEPTH = 4

CTX_LEN = 256
GRID_W = 64
EPS = 1e-6
CONV_W = D_MODEL // 4
CONV_K = 31
MLSTM_DH = 64
MLSTM_W = D_MODEL // 4
MLSTM_HEADS = MLSTM_W // MLSTM_DH
MLSTM_CHUNK = 64
DIFF_DH = 64
DIFF_DV = 2 * DIFF_DH
DIFF_W = D_MODEL // 2
DIFF_HEADS = DIFF_W // DIFF_DV
MIX_W = CONV_W + MLSTM_W + DIFF_W
Q_BLOCK = 128
ROPE_BASE = 10000.0
AXIS_ROT = DIFF_DH // 2
IN_SIZES = (2 * CONV_W, MLSTM_W, MLSTM_W, MLSTM_W, MLSTM_W, 4 * MLSTM_HEADS, 2 * DIFF_HEADS * DIFF_DH, 2 * DIFF_HEADS * DIFF_DH, DIFF_W)
IN_W = sum(IN_SIZES)
PEER_HEADS = 8
PEER_DK = 128
N_KEYS = 128
N_EXPERTS = N_KEYS * N_KEYS
PEER_TOPK = 16
TOKEN_BLOCK = 128

kernel_name = "hybrid_conv_mlstm_diffattn_peer_block"


def rmsnorm(x, g):
    xf = x.astype(jnp.float32)
    y = xf * lax.rsqrt(jnp.mean(xf * xf, axis=-1, keepdims=True) + EPS)
    return (y * g.astype(jnp.float32)).astype(x.dtype)


def layernorm(x, g, b):
    xf = x.astype(jnp.float32)
    mu = jnp.mean(xf, axis=-1, keepdims=True)
    var = jnp.mean(jnp.square(xf - mu), axis=-1, keepdims=True)
    y = (xf - mu) * lax.rsqrt(var + EPS) * g.astype(jnp.float32) + b.astype(jnp.float32)
    return y.astype(x.dtype)


def heads(t, n):
    b, l, w = t.shape
    return t.reshape(b, l, n, w // n).transpose(0, 2, 1, 3)


def merge_heads(t):
    b, n, l, d = t.shape
    return t.transpose(0, 2, 1, 3).reshape(b, l, n * d)


def pair_heads(t):
    b, l, _ = t.shape
    t = t.reshape(b, l, DIFF_HEADS, 2, DIFF_DH)
    return t[:, :, :, 0].transpose(0, 2, 1, 3), t[:, :, :, 1].transpose(0, 2, 1, 3)


def flip_seq(t, rev):
    return jnp.flip(t, axis=2) if rev else t


def axial_rope(n_tokens):
    rows = n_tokens // GRID_W
    row = jnp.repeat(jnp.arange(rows), GRID_W).astype(jnp.float32)
    col = jnp.tile(jnp.arange(GRID_W), rows).astype(jnp.float32)
    inv = ROPE_BASE ** (-jnp.arange(0, AXIS_ROT, 2, dtype=jnp.float32) / AXIS_ROT)
    ang = jnp.concatenate([row[:, None] * inv, col[:, None] * inv], axis=-1)
    return jnp.cos(ang), jnp.sin(ang)


def apply_rope(t, cos, sin):
    tp = t.reshape(t.shape[:-1] + (t.shape[-1] // 2, 2))
    t1, t2 = tp[..., 0], tp[..., 1]
    c = cos.astype(t.dtype)
    s = sin.astype(t.dtype)
    return jnp.stack([t1 * c - t2 * s, t1 * s + t2 * c], axis=-1).reshape(t.shape)


def conv_module(a, conv_w, conv_b, ln_g, ln_b):
    u = a[..., :CONV_W] * jax.nn.sigmoid(a[..., CONV_W:])
    y = lax.conv_general_dilated(u, conv_w.astype(u.dtype), window_strides=(1,),
                                 padding=[(CONV_K // 2, CONV_K // 2)],
                                 dimension_numbers=("NWC", "WIO", "NWC"),
                                 feature_group_count=CONV_W) + conv_b
    return jax.nn.silu(layernorm(y, ln_g, ln_b))


def mlstm_chunk_scan(q, k, v, ig, lf, state):
    bsz, nh, seq, dh = q.shape
    nc = seq // MLSTM_CHUNK

    def chunks(t):
        t = t.reshape(t.shape[:2] + (nc, MLSTM_CHUNK) + t.shape[3:])
        return jnp.moveaxis(t, 2, 0)

    xs = tuple(chunks(t) for t in (q, k, v, ig, lf))
    tril = jnp.tril(jnp.ones((MLSTM_CHUNK, MLSTM_CHUNK), dtype=bool))

    def step(carry, inp):
        C, n, m = carry
        qc, kc, vc, ic, fc = inp
        qf, kf, vf = qc.astype(jnp.float32), kc.astype(jnp.float32), vc.astype(jnp.float32)
        b = jnp.cumsum(fc, axis=-1)
        dlog = jnp.where(tril, b[..., :, None] - b[..., None, :] + ic[..., None, :], -jnp.inf)
        inter = b + m[..., None]
        mt = jnp.maximum(inter, jnp.max(dlog, axis=-1))
        dw = jnp.exp(dlog - mt[..., None])
        iw = jnp.exp(inter - mt)
        s = jnp.einsum("bhtd,bhsd->bhts", qf, kf) * dw
        num = jnp.einsum("bhts,bhse->bhte", s, vf) + iw[..., None] * jnp.einsum("bhed,bhtd->bhte", C, qf)
        den = jnp.sum(s, axis=-1) + iw * jnp.einsum("bhd,bhtd->bht", n, qf)
        h = num / jnp.maximum(jnp.abs(den), jnp.exp(-mt))[..., None]
        bl = b[..., -1]
        wlog = bl[..., None] - b + ic
        mn = jnp.maximum(bl + m, jnp.max(wlog, axis=-1))
        wk = jnp.exp(wlog - mn[..., None])
        decay = jnp.exp(bl + m - mn)
        C = decay[..., None, None] * C + jnp.einsum("bhs,bhse,bhsd->bhed", wk, vf, kf)
        n = decay[..., None] * n + jnp.einsum("bhs,bhsd->bhd", wk, kf)
        return (C, n, mn), h

    state, hs = lax.scan(step, state, xs)
    h = jnp.moveaxis(hs, 0, 2).reshape(bsz, nh, seq, dh)
    return h, state


def mlstm_mixer(lat, ctx, gate_b, norm_g, need_ctx):
    gb = gate_b.reshape(-1).astype(jnp.float32)

    def prep(q, k, v, g):
        b, l, _ = q.shape
        g = (g.astype(jnp.float32) + gb).reshape(b, l, 4, MLSTM_HEADS).transpose(2, 0, 3, 1)
        return heads(q, MLSTM_HEADS), heads(k, MLSTM_HEADS) * (MLSTM_DH ** -0.5), heads(v, MLSTM_HEADS), g

    ql, kl, vl, gl = prep(lat[0], lat[1], lat[2], lat[4])
    qc, kc, vc, gc = prep(ctx[0], ctx[1], ctx[2], ctx[4])
    bsz = ql.shape[0]
    outs_l, outs_c = [], []
    for d, rev in enumerate((False, True)):
        state = (jnp.zeros((bsz, MLSTM_HEADS, MLSTM_DH, MLSTM_DH), jnp.float32),
                 jnp.zeros((bsz, MLSTM_HEADS, MLSTM_DH), jnp.float32),
                 jnp.zeros((bsz, MLSTM_HEADS), jnp.float32))
        hc, state = mlstm_chunk_scan(*[flip_seq(t, rev) for t in (qc, kc, vc, gc[2 * d], jax.nn.log_sigmoid(gc[2 * d + 1]))], state)
        hl, _ = mlstm_chunk_scan(*[flip_seq(t, rev) for t in (ql, kl, vl, gl[2 * d], jax.nn.log_sigmoid(gl[2 * d + 1]))], state)
        outs_c.append(flip_seq(hc, rev))
        outs_l.append(flip_seq(hl, rev))

    def finish(h, o):
        h = merge_heads(rmsnorm(h, norm_g))
        return (h * jax.nn.sigmoid(o.astype(jnp.float32))).astype(o.dtype)

    y_lat = finish(outs_l[0] + outs_l[1], lat[3])
    y_ctx = finish(outs_c[0] + outs_c[1], ctx[3]) if need_ctx else None
    return y_lat, y_ctx


def diff_attention(q1, q2, k1, k2, v, lam):
    bsz, nh, lq, dh = q1.shape
    nb = lq // Q_BLOCK
    scale = dh ** -0.5

    def blocks(t):
        return jnp.moveaxis(t.reshape(bsz, nh, nb, Q_BLOCK, dh), 2, 0)

    def one(args):
        a1, a2 = args
        p1 = jax.nn.softmax(jnp.einsum("bhqd,bhkd->bhqk", a1, k1).astype(jnp.float32) * scale, axis=-1)
        p2 = jax.nn.softmax(jnp.einsum("bhqd,bhkd->bhqk", a2, k2).astype(jnp.float32) * scale, axis=-1)
        p = p1 - lam * p2
        return jnp.einsum("bhqk,bhkd->bhqd", p.astype(v.dtype), v)

    o = lax.map(one, (blocks(q1), blocks(q2)))
    return jnp.moveaxis(o, 0, 2).reshape(bsz, nh, lq, v.shape[-1])


def diff_mixer(lat, ctx, lam_params, lam_init, norm_g, cos, sin, need_ctx):
    dql, dkl, dvl = lat
    dqc, dkc, dvc = ctx
    lp = lam_params.astype(jnp.float32)
    lam = jnp.exp(jnp.sum(lp[0] * lp[1])) - jnp.exp(jnp.sum(lp[2] * lp[3])) + lam_init
    ql1, ql2 = (apply_rope(t, cos, sin) for t in pair_heads(dql))
    kl1, kl2 = (apply_rope(t, cos, sin) for t in pair_heads(dkl))
    qc1, qc2 = pair_heads(dqc)
    kc1, kc2 = pair_heads(dkc)
    vl = heads(dvl, DIFF_HEADS)
    vc = heads(dvc, DIFF_HEADS)
    k1 = jnp.concatenate([kc1, kl1], axis=2)
    k2 = jnp.concatenate([kc2, kl2], axis=2)
    v = jnp.concatenate([vc, vl], axis=2)

    def finish(o):
        return merge_heads(rmsnorm(o, norm_g) * (1.0 - lam_init))

    y_lat = finish(diff_attention(ql1, ql2, k1, k2, v, lam))
    y_ctx = finish(diff_attention(qc1, qc2, kc1, kc2, vc, lam)) if need_ctx else None
    return y_lat, y_ctx


def peer_ffn(h, wq, sub_keys, eu, ev):
    bsz, seq, dm = h.shape
    nb = (bsz * seq) // TOKEN_BLOCK
    hb = h.reshape(nb, TOKEN_BLOCK, dm)

    def one(t):
        q = (t @ wq).reshape(TOKEN_BLOCK, PEER_HEADS, 2, PEER_DK // 2)
        s = jnp.einsum("thpd,hpnd->thpn", q, sub_keys).astype(jnp.float32)
        st, it = lax.top_k(s, PEER_TOPK)
        cand = (st[:, :, 0, :, None] + st[:, :, 1, None, :]).reshape(TOKEN_BLOCK, PEER_HEADS, PEER_TOPK * PEER_TOPK)
        cidx = (it[:, :, 0, :, None] * N_KEYS + it[:, :, 1, None, :]).reshape(TOKEN_BLOCK, PEER_HEADS, PEER_TOPK * PEER_TOPK)
        best, pos = lax.top_k(cand, PEER_TOPK)
        e = jnp.take_along_axis(cidx, pos, axis=-1)
        g = jax.nn.softmax(best, axis=-1)
        u = jnp.take(eu, e, axis=0)
        act = jax.nn.gelu(jnp.einsum("thkd,td->thk", u, t).astype(jnp.float32), approximate=False)
        return jnp.einsum("thk,thkd->td", (g * act).astype(ev.dtype), jnp.take(ev, e, axis=0))

    return lax.map(one, hb).reshape(bsz, seq, dm)


def trunk_layer(x, xc, mod, modc, lam_init, cos, sin, need_ctx,
                norm1_g, norm2_g, w_in, conv_w, conv_b, conv_ln_g, conv_ln_b,
                mlstm_gate_b, mlstm_norm_g, diff_lambda, diff_norm_g, w_out,
                peer_wq, peer_keys, peer_u, peer_v):
    sh1, sc1, g1, sh2, sc2, g2 = [m[:, None, :] for m in jnp.split(mod, 6, axis=-1)]
    sh1c, sc1c, g1c, sh2c, sc2c, g2c = jnp.split(modc, 6, axis=-1)
    splits = np.cumsum(IN_SIZES)[:-1].tolist()
    h = rmsnorm(x, norm1_g) * (1.0 + sc1) + sh1
    hc = rmsnorm(xc, norm1_g) * (1.0 + sc1c) + sh1c
    a, mq, mk, mv, mo, mg, dq, dk, dv = jnp.split(h @ w_in, splits, axis=-1)
    ac, mqc, mkc, mvc, moc, mgc, dqc, dkc, dvc = jnp.split(hc @ w_in, splits, axis=-1)
    conv_l = conv_module(a, conv_w, conv_b, conv_ln_g, conv_ln_b)
    mlstm_l, mlstm_c = mlstm_mixer((mq, mk, mv, mo, mg), (mqc, mkc, mvc, moc, mgc), mlstm_gate_b, mlstm_norm_g, need_ctx)
    diff_l, diff_c = diff_mixer((dq, dk, dv), (dqc, dkc, dvc), diff_lambda, lam_init, diff_norm_g, cos, sin, need_ctx)
    mix = jnp.concatenate([conv_l.astype(x.dtype), mlstm_l.astype(x.dtype), diff_l.astype(x.dtype)], axis=-1)
    x = x + g1 * (mix @ w_out)
    h2 = rmsnorm(x, norm2_g) * (1.0 + sc2) + sh2
    x = x + g2 * peer_ffn(h2, peer_wq, peer_keys, peer_u, peer_v)
    if need_ctx:
        conv_c = conv_module(ac, conv_w, conv_b, conv_ln_g, conv_ln_b)
        mix_c = jnp.concatenate([conv_c.astype(xc.dtype), mlstm_c.astype(xc.dtype), diff_c.astype(xc.dtype)], axis=-1)
        xc = xc + g1c * (mix_c @ w_out)
        h2c = rmsnorm(xc, norm2_g) * (1.0 + sc2c) + sh2c
        xc = xc + g2c * peer_ffn(h2c, peer_wq, peer_keys, peer_u, peer_v)
    return x, xc


def setup_inputs(seed: int = 0) -> dict:
    key = jax.random.key(seed)
    ks = jax.random.split(key, 24)
    f32 = jnp.float32

    def nrm(k, shape, scale):
        return jax.random.normal(k, shape, f32) * scale

    gate_offset = jnp.array([0.0, 3.0, 0.0, 3.0], f32)[None, :, None]
    return {
        "x": nrm(ks[0], (BATCH, SEQ, D_MODEL), 1.0),
        "c": nrm(ks[1], (BATCH, D_MODEL), 1.0),
        "ctx": nrm(ks[2], (BATCH, CTX_LEN, D_MODEL), 1.0),
        "c_ctx": nrm(ks[3], (D_MODEL,), 1.0),
        "ada_w": nrm(ks[4], (DEPTH, D_MODEL, 6 * D_MODEL), 0.5 * D_MODEL ** -0.5),
        "ada_b": nrm(ks[5], (DEPTH, 6 * D_MODEL), 0.02),
        "norm1_g": 1.0 + nrm(ks[6], (DEPTH, D_MODEL), 0.02),
        "norm2_g": 1.0 + nrm(ks[7], (DEPTH, D_MODEL), 0.02),
        "w_in": nrm(ks[8], (DEPTH, D_MODEL, IN_W), D_MODEL ** -0.5),
        "conv_w": nrm(ks[9], (DEPTH, CONV_K, 1, CONV_W), CONV_K ** -0.5),
        "conv_b": nrm(ks[10], (DEPTH, CONV_W), 0.02),
        "conv_ln_g": 1.0 + nrm(ks[11], (DEPTH, CONV_W), 0.02),
        "conv_ln_b": nrm(ks[12], (DEPTH, CONV_W), 0.02),
        "mlstm_gate_b": gate_offset + nrm(ks[13], (DEPTH, 4, MLSTM_HEADS), 0.5),
        "mlstm_norm_g": 1.0 + nrm(ks[14], (DEPTH, MLSTM_DH), 0.02),
        "diff_lambda": nrm(ks[15], (DEPTH, 4, DIFF_DH), 0.1),
        "diff_norm_g": 1.0 + nrm(ks[16], (DEPTH, DIFF_DV), 0.02),
        "w_out": nrm(ks[17], (DEPTH, MIX_W, D_MODEL), MIX_W ** -0.5),
        "peer_wq": nrm(ks[18], (DEPTH, D_MODEL, PEER_HEADS * PEER_DK), D_MODEL ** -0.5),
        "peer_keys": nrm(ks[19], (DEPTH, PEER_HEADS, 2, N_KEYS, PEER_DK // 2), (PEER_DK // 2) ** -0.5),
        "peer_u": nrm(ks[20], (DEPTH, N_EXPERTS, D_MODEL), D_MODEL ** -0.5),
        "peer_v": nrm(ks[21], (DEPTH, N_EXPERTS, D_MODEL), PEER_HEADS ** -0.5),
        "final_g": 1.0 + nrm(ks[22], (D_MODEL,), 0.02),
    }


def reference(x, c, ctx, c_ctx, ada_w, ada_b, norm1_g, norm2_g, w_in, conv_w, conv_b,
              conv_ln_g, conv_ln_b, mlstm_gate_b, mlstm_norm_g, diff_lambda, diff_norm_g,
              w_out, peer_wq, peer_keys, peer_u, peer_v, final_g):
    cos, sin = axial_rope(x.shape[1])
    sc = jax.nn.silu(c)
    scc = jax.nn.silu(c_ctx)
    xc = ctx
    for l in range(DEPTH):
        mod = sc @ ada_w[l] + ada_b[l]
        modc = scc @ ada_w[l] + ada_b[l]
        lam_init = 0.8 - 0.6 * math.exp(-0.3 * l)
        x, xc = trunk_layer(x, xc, mod, modc, lam_init, cos, sin, l < DEPTH - 1,
                            norm1_g[l], norm2_g[l], w_in[l], conv_w[l], conv_b[l],
                            conv_ln_g[l], conv_ln_b[l], mlstm_gate_b[l], mlstm_norm_g[l],
                            diff_lambda[l], diff_norm_g[l], w_out[l], peer_wq[l],
                            peer_keys[l], peer_u[l], peer_v[l])
    return rmsnorm(x, final_g)
```

```python
import functools
import math

import jax
import jax.numpy as jnp
from jax import lax
from jax.experimental import pallas as pl
from jax.experimental.pallas import tpu as pltpu

F32 = jnp.float32
BF16 = jnp.bfloat16
HIGHEST = lax.Precision.HIGHEST

GRID_W = 64
EPS = 1e-6
CONV_K = 31
MLSTM_DH = 64
MLSTM_HEADS = 4
DIFF_DH = 64
DIFF_DV = 128
DIFF_HEADS = 4
ROPE_BASE = 10000.0
PEER_HEADS = 8
N_KEYS = 128
PEER_TOPK = 16

LANES = 128
SUBLANES = 8
TILE = 256
CONV_HALO = 16
ATT_KC = 1024
PEER_TM = 512
PEER_EC = 1024
VMEM_LIMIT = 56 * 1024 * 1024

NT_DIMS = (((1,), (1,)), ((), ()))


def _cparams(sem):
    return pltpu.CompilerParams(dimension_semantics=sem, vmem_limit_bytes=VMEM_LIMIT)


def _rms(x, eps=EPS):
    return x * lax.rsqrt(jnp.mean(x * x, axis=-1, keepdims=True) + eps)


def _log_sigmoid(x):
    return jnp.minimum(x, 0.0) - jnp.log(1.0 + jnp.exp(-jnp.abs(x)))


def _ada_kernel(c_ref, w_ref, b_ref, o_ref):
    c = c_ref[...]
    s = (c * jax.nn.sigmoid(c)).astype(BF16)
    o_ref[...] = jnp.dot(s, w_ref[...].astype(BF16), preferred_element_type=F32) + b_ref[...]


def _ada_call(cvec, ada_w, ada_b):
    depth, d, n = ada_w.shape
    tn = 1536
    rows = cvec.shape[0]
    return pl.pallas_call(
        _ada_kernel,
        grid=(depth, n // tn),
        in_specs=[pl.BlockSpec((rows, d), lambda l, j: (0, 0)),
                  pl.BlockSpec((None, d, tn), lambda l, j: (l, 0, j)),
                  pl.BlockSpec((None, 1, tn), lambda l, j: (l, 0, j))],
        out_specs=pl.BlockSpec((None, rows, tn), lambda l, j: (l, 0, j)),
        out_shape=jax.ShapeDtypeStruct((depth, rows, n), F32),
        compiler_params=_cparams(("parallel", "parallel")),
        name="ada_mod",
    )(cvec, ada_w, ada_b.reshape(depth, 1, n))


def _rope(t, c, sa, sb):
    w = t.shape[1]
    rep = w // LANES
    c, sa, sb = (jnp.concatenate([z] * rep, axis=1) for z in (c, sa, sb))
    return t * c + pltpu.roll(t, w - 1, 1) * sa + pltpu.roll(t, 1, 1) * sb


def _inproj_kernel(*refs, has_peer, cw, mw):
    if has_peer:
        x_ref, peer_ref, modp_ref = refs[:3]
        refs = refs[3:]
    (mod_ref, n1g_ref, wm_ref, wg_ref, wgt_ref, gbc_ref, gbr_ref, cos_ref, sa_ref, sb_ref,
     xo_ref, u_ref, mqkv_ref, mo_ref, gcol_ref, grow_ref, dq_ref, dk_ref, dv_ref) = refs[-19:]
    if not has_peer:
        x_ref = refs[0]
    x = x_ref[...]
    if has_peer:
        x = x + modp_ref[5:6, :] * peer_ref[...]
        xo_ref[...] = x
    else:
        xo_ref[...] = x
    h = _rms(x) * n1g_ref[...] * (1.0 + mod_ref[1:2, :]) + mod_ref[0:1, :]
    hb = h.astype(BF16)

    def proj(lo, hi):
        return jnp.dot(hb, wm_ref[:, lo:hi], preferred_element_type=F32)

    a = proj(0, 2 * cw)
    u_ref[...] = a[:, :cw] * jax.nn.sigmoid(a[:, cw:])
    o = 2 * cw
    mqkv_ref[:, 0:mw] = proj(o, o + mw).astype(BF16)
    mqkv_ref[:, mw:2 * mw] = (proj(o + mw, o + 2 * mw) * (MLSTM_DH ** -0.5)).astype(BF16)
    mqkv_ref[:, 2 * mw:3 * mw] = proj(o + 2 * mw, o + 3 * mw).astype(BF16)
    mo_ref[...] = proj(o + 3 * mw, o + 4 * mw)
    o = o + 4 * mw
    dw = dq_ref.shape[1]
    c, sa, sb = cos_ref[...], sa_ref[...], sb_ref[...]
    dq_ref[...] = (_rope(proj(o, o + dw), c, sa, sb) * (DIFF_DH ** -0.5)).astype(BF16)
    dk_ref[...] = _rope(proj(o + dw, o + 2 * dw), c, sa, sb).astype(BF16)
    dv_ref[...] = proj(o + 2 * dw, o + 3 * dw).astype(BF16)

    g = jnp.dot(h, wg_ref[...], precision=HIGHEST, preferred_element_type=F32) + gbc_ref[...]
    gt = lax.dot_general(wgt_ref[...], h, NT_DIMS, precision=HIGHEST,
                         preferred_element_type=F32) + gbr_ref[...]
    nh = MLSTM_HEADS
    cidx = lax.broadcasted_iota(jnp.int32, g.shape, 1)
    ridx = lax.broadcasted_iota(jnp.int32, gt.shape, 0)
    gcol_ref[...] = jnp.where((cidx // nh) % 2 == 1, _log_sigmoid(g), g)
    grow_ref[...] = jnp.where((ridx // nh) % 2 == 1, _log_sigmoid(gt), gt)


def _inproj_call(x, peer, modp, mod, n1g, wm, wg, wgt, gbc, gbr, cos, sa, sb):
    b, t, d = x.shape
    nt = t // TILE
    cw = d // 4
    mw = d // 4
    dw = d // 2
    has_peer = peer is not None
    tok = lambda bi, i: (bi, i, 0)
    modspec = pl.BlockSpec((None, None, 6, d), lambda bi, i: (bi, jnp.minimum(i, 1), 0, 0))
    full2 = lambda arr: pl.BlockSpec(arr.shape, lambda bi, i: (0, 0))
    in_specs = [pl.BlockSpec((None, TILE, d), tok)]
    args = [x]
    if has_peer:
        in_specs += [pl.BlockSpec((None, TILE, d), tok), modspec]
        args += [peer, modp]
    in_specs += [modspec, full2(n1g), full2(wm), full2(wg), full2(wgt), full2(gbc), full2(gbr)]
    args += [mod, n1g, wm, wg, wgt, gbc, gbr]
    in_specs += [pl.BlockSpec((TILE, LANES), lambda bi, i: (i, 0))] * 3
    args += [cos, sa, sb]
    ng = wg.shape[1]
    out_shape = [jax.ShapeDtypeStruct((b, t, d), F32),
                 jax.ShapeDtypeStruct((b, t, cw), F32),
                 jax.ShapeDtypeStruct((b, t, 3 * mw), BF16),
                 jax.ShapeDtypeStruct((b, t, mw), F32),
                 jax.ShapeDtypeStruct((b, t, ng), F32),
                 jax.ShapeDtypeStruct((b, ng, t), F32),
                 jax.ShapeDtypeStruct((b, t, dw), BF16),
                 jax.ShapeDtypeStruct((b, t, dw), BF16),
                 jax.ShapeDtypeStruct((b, t, dw), BF16)]
    out_specs = [pl.BlockSpec((None, TILE, d), tok),
                 pl.BlockSpec((None, TILE, cw), tok),
                 pl.BlockSpec((None, TILE, 3 * mw), tok),
                 pl.BlockSpec((None, TILE, mw), tok),
                 pl.BlockSpec((None, TILE, ng), tok),
                 pl.BlockSpec((None, ng, TILE), lambda bi, i: (bi, 0, i)),
                 pl.BlockSpec((None, TILE, dw), tok),
                 pl.BlockSpec((None, TILE, dw), tok),
                 pl.BlockSpec((None, TILE, dw), tok)]
    return pl.pallas_call(
        functools.partial(_inproj_kernel, has_peer=has_peer, cw=cw, mw=mw),
        grid=(b, nt), in_specs=in_specs, out_specs=out_specs, out_shape=out_shape,
        compiler_params=_cparams(("parallel", "parallel")),
        name="in_proj",
    )(*args)


def _conv_kernel(up_ref, uc_ref, un_ref, w_ref, b_ref, lg_ref, lb_ref, o_ref, ext_ref):
    i = pl.program_id(1)
    nt = pl.num_programs(1)
    lm = jnp.where(i >= 2, 1.0, 0.0)
    rm = jnp.where(jnp.logical_and(i >= 1, i < nt - 1), 1.0, 0.0)
    hl = CONV_HALO
    ext_ref[0:hl, :] = up_ref[TILE - hl:TILE, :] * lm
    ext_ref[hl:hl + TILE, :] = uc_ref[...]
    ext_ref[hl + TILE:2 * hl + TILE, :] = un_ref[0:hl, :] * rm
    off = hl - CONV_K // 2
    acc = jnp.zeros(uc_ref.shape, F32)
    for k in range(CONV_K):
        acc = acc + w_ref[k:k + 1, :] * ext_ref[off + k:off + k + TILE, :]
    y = acc + b_ref[...]
    mu = jnp.mean(y, axis=-1, keepdims=True)
    yc = y - mu
    var = jnp.mean(yc * yc, axis=-1, keepdims=True)
    z = yc * lax.rsqrt(var + EPS) * lg_ref[...] + lb_ref[...]
    o_ref[...] = (z * jax.nn.sigmoid(z)).astype(o_ref.dtype)


def _conv_call(u, w, bias, lg, lb):
    b, t, cw = u.shape
    nt = t // TILE
    full2 = lambda arr: pl.BlockSpec(arr.shape, lambda bi, i: (0, 0))
    return pl.pallas_call(
        _conv_kernel,
        grid=(b, nt),
        in_specs=[pl.BlockSpec((None, TILE, cw), lambda bi, i: (bi, jnp.maximum(i - 1, 0), 0)),
                  pl.BlockSpec((None, TILE, cw), lambda bi, i: (bi, i, 0)),
                  pl.BlockSpec((None, TILE, cw), lambda bi, i: (bi, jnp.minimum(i + 1, nt - 1), 0)),
                  full2(w), full2(bias), full2(lg), full2(lb)],
        out_specs=pl.BlockSpec((None, TILE, cw), lambda bi, i: (bi, i, 0)),
        out_shape=jax.ShapeDtypeStruct((b, t, cw), BF16),
        scratch_shapes=[pltpu.VMEM((TILE + 2 * CONV_HALO, cw), F32)],
        compiler_params=_cparams(("parallel", "parallel")),
        name="conv_module",
    )(u, u, u, w, bias, lg, lb)


def _mlstm_kernel(qkv_ref, gc_ref, gr_ref, h_ref, c_ref, n_ref, m_ref):
    d = pl.program_id(0)
    j = pl.program_id(2)

    @pl.when(j == 0)
    def _():
        c_ref[...] = jnp.zeros_like(c_ref)
        n_ref[...] = jnp.zeros_like(n_ref)
        m_ref[...] = jnp.zeros_like(m_ref)

    tc = qkv_ref.shape[0]
    mw = qkv_ref.shape[1] // 3
    nh = MLSTM_HEADS
    fwd = d == 0
    row = lax.broadcasted_iota(jnp.int32, (tc, tc), 0)
    col = lax.broadcasted_iota(jnp.int32, (tc, tc), 1)
    tri = (row - col) * jnp.where(fwd, 1, -1) >= 0
    trif = tri.astype(F32)
    gc = gc_ref[...]
    gr = gr_ref[...]
    bcol = jnp.dot(trif, gc, precision=HIGHEST, preferred_element_type=F32)
    brow = lax.dot_general(gr, trif, NT_DIMS, precision=HIGHEST, preferred_element_type=F32)
    bl = jnp.sum(gr, axis=1, keepdims=True)

    q = qkv_ref[:, 0:mw]
    k = qkv_ref[:, mw:2 * mw]
    v = qkv_ref[:, 2 * mw:3 * mw]
    lane_head = lax.broadcasted_iota(jnp.int32, (tc, mw), 1) // MLSTM_DH
    cb = c_ref[...]
    inter_c = lax.dot_general(q, cb.astype(BF16), NT_DIMS, preferred_element_type=F32)
    qf = q.astype(F32)
    qn = qf * n_ref[...]

    out = jnp.zeros((tc, mw), F32)
    wkfull = jnp.zeros((tc, mw), F32)
    rhead = lax.broadcasted_iota(jnp.int32, (mw, 1), 0) // MLSTM_DH
    chead = lax.broadcasted_iota(jnp.int32, (1, mw), 1) // MLSTM_DH
    decay_col = jnp.zeros((mw, 1), F32)
    decay_row = jnp.zeros((1, mw), F32)
    for h in range(nh):
        mh = m_ref[h:h + 1, 0:1]
        bc = bcol[:, nh + h:nh + h + 1]
        br = brow[nh + h:nh + h + 1, :]
        ir = gr[h:h + 1, :]
        dlog = jnp.where(tri, bc - br + ir, -jnp.inf)
        inter = bc + mh
        mt = jnp.maximum(inter, jnp.max(dlog, axis=1, keepdims=True))
        dwt = jnp.exp(dlog - mt)
        iw = jnp.exp(inter - mt)
        hm = lane_head == h
        qh = jnp.where(hm, qf, 0.0).astype(BF16)
        s = lax.dot_general(qh, k, NT_DIMS, preferred_element_type=F32) * dwt
        sv = jnp.dot(s.astype(BF16), v, preferred_element_type=F32)
        qn_h = jnp.sum(jnp.where(hm, qn, 0.0), axis=1, keepdims=True)
        den = jnp.sum(s, axis=1, keepdims=True) + iw * qn_h
        denom = jnp.maximum(jnp.abs(den), jnp.exp(-mt))
        out = jnp.where(hm, (sv + iw * inter_c) / denom, out)
        blh = bl[nh + h:nh + h + 1, :]
        wlog_r = blh - br + ir
        mn = jnp.maximum(blh + mh, jnp.max(wlog_r, axis=1, keepdims=True))
        decay = jnp.exp(blh + mh - mn)
        wk_c = jnp.exp(blh - bc + gc[:, h:h + 1] - mn)
        wkfull = jnp.where(hm, wk_c, wkfull)
        decay_col = jnp.where(rhead == h, decay, decay_col)
        decay_row = jnp.where(chead == h, decay, decay_row)
        m_ref[h:h + 1, :] = jnp.broadcast_to(mn, (1, m_ref.shape[1]))
    h_ref[...] = out

    vw = v.astype(F32) * wkfull
    upd = jnp.dot(vw.T.astype(BF16), k, preferred_element_type=F32)
    c_ref[...] = decay_col * cb + jnp.where(rhead == chead, upd, 0.0)
    n_ref[...] = decay_row * n_ref[...] + jnp.sum(k.astype(F32) * wkfull, axis=0, keepdims=True)


def _mlstm_call(mqkv, gcol, grow):
    b, t, w3 = mqkv.shape
    mw = w3 // 3
    nt = t // TILE
    ng = gcol.shape[-1]

    def tile(d, j):
        return jnp.where(d == 0, j, jnp.where(j == 0, 0, nt - j))

    return pl.pallas_call(
        _mlstm_kernel,
        grid=(2, b, nt),
        in_specs=[pl.BlockSpec((None, TILE, w3), lambda d, bi, j: (bi, tile(d, j), 0)),
                  pl.BlockSpec((None, None, TILE, ng), lambda d, bi, j: (bi, d, tile(d, j), 0)),
                  pl.BlockSpec((None, None, ng, TILE), lambda d, bi, j: (bi, d, 0, tile(d, j)))],
        out_specs=pl.BlockSpec((None, None, TILE, mw), lambda d, bi, j: (d, bi, tile(d, j), 0)),
        out_shape=jax.ShapeDtypeStruct((2, b, t, mw), F32),
        scratch_shapes=[pltpu.VMEM((mw, mw), F32), pltpu.VMEM((1, mw), F32),
                        pltpu.VMEM((SUBLANES, LANES), F32)],
        compiler_params=_cparams(("arbitrary", "arbitrary", "arbitrary")),
        name="mlstm_scan",
    )(mqkv, gcol, grow)


def _attn_kernel(dl_ref, ng_ref, q_ref, k_ref, v_ref, o_ref, m_ref, l_ref, acc_ref, *, lam_init, n_ctx):
    i = pl.program_id(2)
    tq = q_ref.shape[0]
    t_all = k_ref.shape[0]
    q = q_ref[...]
    lane = lax.broadcasted_iota(jnp.int32, q.shape, 1)
    zero = jnp.zeros_like(q)
    qz = (jnp.where(lane < DIFF_DH, q, zero), jnp.where(lane >= DIFF_DH, q, zero))
    m_ref[...] = jnp.full(m_ref.shape, -jnp.inf, F32)
    l_ref[...] = jnp.zeros_like(l_ref)
    acc_ref[...] = jnp.zeros_like(acc_ref)

    def chunk(start, size):
        kb = k_ref[pl.ds(start, size), :]
        vb = v_ref[pl.ds(start, size), :]
        for c in range(2):
            s = lax.dot_general(qz[c], kb, NT_DIMS, preferred_element_type=F32)
            m_old = m_ref[c]
            m_new = jnp.maximum(m_old, jnp.max(s, axis=1, keepdims=True))
            alpha = jnp.exp(m_old - m_new)
            p = jnp.exp(s - m_new)
            l_ref[c] = alpha * l_ref[c] + jnp.sum(p, axis=1, keepdims=True)
            acc_ref[c] = alpha * acc_ref[c] + jnp.dot(p.astype(BF16), vb, preferred_element_type=F32)
            m_ref[c] = m_new

    chunk(0, n_ctx)

    @pl.when(i > 0)
    def _():
        def body(c, carry):
            chunk(pl.multiple_of(n_ctx + c * ATT_KC, TILE), ATT_KC)
            return carry
        lax.fori_loop(0, (t_all - n_ctx) // ATT_KC, body, 0)

    dl = dl_ref[...]
    lam = (jnp.exp(jnp.sum(dl[0:1] * dl[1:2], axis=1, keepdims=True))
           - jnp.exp(jnp.sum(dl[2:3] * dl[3:4], axis=1, keepdims=True)) + lam_init)
    o = acc_ref[0] / l_ref[0] - lam * (acc_ref[1] / l_ref[1])
    o_ref[...] = (_rms(o) * ng_ref[...] * (1.0 - lam_init)).astype(o_ref.dtype)


def _attn_call(dq, dk, dv, dlam, dng, lam_init):
    b, t, w = dq.shape
    nh = w // DIFF_DV
    nq = t // TILE
    kern = functools.partial(_attn_kernel, lam_init=lam_init, n_ctx=TILE)
    return pl.pallas_call(
        kern,
        grid=(b, nh, nq),
        in_specs=[pl.BlockSpec(dlam.shape, lambda bi, h, i: (0, 0)),
                  pl.BlockSpec(dng.shape, lambda bi, h, i: (0, 0)),
                  pl.BlockSpec((None, TILE, DIFF_DV), lambda bi, h, i: (bi, i, h)),
                  pl.BlockSpec((None, t, DIFF_DV), lambda bi, h, i: (bi, 0, h)),
                  pl.BlockSpec((None, t, DIFF_DV), lambda bi, h, i: (bi, 0, h))],
        out_specs=pl.BlockSpec((None, TILE, DIFF_DV), lambda bi, h, i: (bi, i, h)),
        out_shape=jax.ShapeDtypeStruct((b, t, w), BF16),
        scratch_shapes=[pltpu.VMEM((2, TILE, 1), F32), pltpu.VMEM((2, TILE, 1), F32),
                        pltpu.VMEM((2, TILE, DIFF_DV), F32)],
        compiler_params=_cparams(("parallel", "parallel", "arbitrary")),
        name="diff_attn",
    )(dlam, dng, dq, dk, dv)


def _outproj_kernel(x_ref, conv_ref, hf_ref, hb_ref, mo_ref, dy_ref, mod_ref, mng_ref, n2g_ref,
                    wo_ref, xo_ref, h2_ref):
    cw = conv_ref.shape[1]
    mw = hf_ref.shape[1]
    hh = hf_ref[...] + hb_ref[...]
    r = lax.broadcasted_iota(jnp.int32, (mw, mw), 0) // MLSTM_DH
    c = lax.broadcasted_iota(jnp.int32, (mw, mw), 1) // MLSTM_DH
    gm = jnp.where(r == c, 1.0 / MLSTM_DH, 0.0)
    ms = jnp.dot(hh * hh, gm, precision=HIGHEST, preferred_element_type=F32)
    ym = hh * lax.rsqrt(ms + EPS) * mng_ref[...] * jax.nn.sigmoid(mo_ref[...])
    o = (jnp.dot(conv_ref[...], wo_ref[0:cw, :], preferred_element_type=F32)
         + jnp.dot(ym.astype(BF16), wo_ref[cw:cw + mw, :], preferred_element_type=F32)
         + jnp.dot(dy_ref[...], wo_ref[cw + mw:, :], preferred_element_type=F32))
    x = x_ref[...] + mod_ref[2:3, :] * o
    xo_ref[...] = x
    h2 = _rms(x) * n2g_ref[...] * (1.0 + mod_ref[4:5, :]) + mod_ref[3:4, :]
    h2_ref[...] = h2.astype(BF16)


def _outproj_call(x, conv, hdir, mo, dy, mod, mng, n2g, wo):
    b, t, d = x.shape
    nt = t // TILE
    cw, mw, dw = conv.shape[-1], mo.shape[-1], dy.shape[-1]
    tok = lambda bi, i: (bi, i, 0)
    full2 = lambda arr: pl.BlockSpec(arr.shape, lambda bi, i: (0, 0))
    return pl.pallas_call(
        _outproj_kernel,
        grid=(b, nt),
        in_specs=[pl.BlockSpec((None, TILE, d), tok),
                  pl.BlockSpec((None, TILE, cw), tok),
                  pl.BlockSpec((None, None, TILE, mw), lambda bi, i: (0, bi, i, 0)),
                  pl.BlockSpec((None, None, TILE, mw), lambda bi, i: (1, bi, i, 0)),
                  pl.BlockSpec((None, TILE, mw), tok),
                  pl.BlockSpec((None, TILE, dw), tok),
                  pl.BlockSpec((None, None, 6, d), lambda bi, i: (bi, jnp.minimum(i, 1), 0, 0)),
                  full2(mng), full2(n2g), full2(wo)],
        out_specs=[pl.BlockSpec((None, TILE, d), tok), pl.BlockSpec((None, TILE, d), tok)],
        out_shape=[jax.ShapeDtypeStruct((b, t, d), F32), jax.ShapeDtypeStruct((b, t, d), BF16)],
        compiler_params=_cparams(("parallel", "parallel")),
        name="out_proj",
    )(x, conv, hdir, hdir, mo, dy, mod, mng, n2g, wo)


def _top_k_rows(s, k):
    idx = lax.broadcasted_iota(jnp.int32, s.shape, 0).astype(F32)
    vals, ids = [], []
    cur = s
    for _ in range(k):
        m = jnp.max(cur, axis=0, keepdims=True)
        am = jnp.min(jnp.where(cur == m, idx, float(s.shape[0])), axis=0, keepdims=True)
        vals.append(m)
        ids.append(am)
        cur = jnp.where(idx == am, -jnp.inf, cur)
    return vals, ids


def _peer_sel_kernel(h2_ref, wq_ref, kh_ref, a_ref, ni_ref, r1_ref, bv_ref):
    kk = PEER_TOPK
    q = jnp.dot(h2_ref[...], wq_ref[...], preferred_element_type=F32).astype(BF16)
    tt = q.shape[0]
    hw = q.shape[1] // PEER_HEADS
    iota_k = lax.broadcasted_iota(jnp.int32, (kk, tt), 0).astype(F32)
    iota_n = lax.broadcasted_iota(jnp.int32, (N_KEYS, tt), 0).astype(F32)
    for h in range(PEER_HEADS):
        st_ = lax.dot_general(kh_ref[h], q[:, h * hw:(h + 1) * hw], NT_DIMS,
                              preferred_element_type=F32)
        s0 = st_[0:N_KEYS]
        s1 = st_[N_KEYS:2 * N_KEYS]
        v0, i0 = _top_k_rows(s0, kk)
        v1, i1 = _top_k_rows(s1, kk)
        st0 = jnp.concatenate(v0, axis=0)
        n = jnp.zeros((kk, tt), F32)
        hv = st0 + v1[0]
        mx = v0[0] + v1[0]
        zsum = jnp.zeros((1, tt), F32)
        for _ in range(kk):
            m = jnp.max(hv, axis=0, keepdims=True)
            a_star = jnp.min(jnp.where(hv == m, iota_k, float(kk)), axis=0, keepdims=True)
            sel = iota_k == a_star
            zsum = zsum + jnp.exp(m - mx)
            n = n + jnp.where(sel, 1.0, 0.0)
            nxt = jnp.full((kk, tt), -jnp.inf, F32)
            for b in range(1, kk):
                nxt = jnp.where(n == float(b), v1[b], nxt)
            hv = jnp.where(sel, st0 + nxt, hv)
        ni = jnp.zeros((N_KEYS, tt), F32)
        r1 = jnp.full((N_KEYS, tt), float(kk), F32)
        for a in range(kk):
            ni = jnp.where(iota_n == i0[a], n[a:a + 1, :], ni)
            r1 = jnp.where(iota_n == i1[a], float(a), r1)
        a_ref[h] = jnp.exp(s0 - v0[0]) / zsum
        ni_ref[h] = ni
        r1_ref[h] = r1
        bv_ref[h] = jnp.exp(s1 - v1[0])


def _peer_sel_call(h2, wq, kh):
    ntok, d = h2.shape
    nt = ntok // TILE
    shp = jax.ShapeDtypeStruct((PEER_HEADS, N_KEYS, ntok), F32)
    ospec = pl.BlockSpec((PEER_HEADS, N_KEYS, TILE), lambda i: (0, 0, i))
    return pl.pallas_call(
        _peer_sel_kernel,
        grid=(nt,),
        in_specs=[pl.BlockSpec((TILE, d), lambda i: (i, 0)),
                  pl.BlockSpec(wq.shape, lambda i: (0, 0)),
                  pl.BlockSpec(kh.shape, lambda i: (0, 0, 0))],
        out_specs=[ospec] * 4,
        out_shape=[shp] * 4,
        compiler_params=_cparams(("parallel",)),
        name="peer_select",
    )(h2, wq, kh)


def _gelu(x):
    return 0.5 * x * (1.0 + lax.erf(x * (2.0 ** -0.5)))


def _peer_dense_kernel(h2_ref, a_ref, ni_ref, r1_ref, bv_ref, u_ref, vt_ref, o_ref, acc_ref):
    c = pl.program_id(1)

    @pl.when(c == 0)
    def _():
        acc_ref[...] = jnp.zeros_like(acc_ref)

    st_ = lax.dot_general(u_ref[...], h2_ref[...], NT_DIMS, preferred_element_type=F32)
    blocks = []
    for ib in range(u_ref.shape[0] // N_KEYS):
        g = jnp.zeros((N_KEYS, st_.shape[1]), F32)
        for h in range(PEER_HEADS):
            keep = r1_ref[h] < ni_ref[h, ib:ib + 1, :]
            g = g + a_ref[h, ib:ib + 1, :] * jnp.where(keep, bv_ref[h], 0.0)
        blocks.append((g * _gelu(st_[ib * N_KEYS:(ib + 1) * N_KEYS])).astype(BF16))
    wt = jnp.concatenate(blocks, axis=0)
    acc_ref[...] += jnp.dot(vt_ref[...], wt, preferred_element_type=F32)

    @pl.when(c == pl.num_programs(1) - 1)
    def _():
        o_ref[...] = acc_ref[...].T


def _peer_dense_call(h2, a, ni, r1, bv, u, vt):
    ntok, d = h2.shape
    ne = u.shape[0]
    tm, ec = PEER_TM, PEER_EC
    ib = ec // N_KEYS
    rowspec = pl.BlockSpec((PEER_HEADS, ib, tm), lambda t, c: (0, c, t))
    colspec = pl.BlockSpec((PEER_HEADS, N_KEYS, tm), lambda t, c: (0, 0, t))
    return pl.pallas_call(
        _peer_dense_kernel,
        grid=(ntok // tm, ne // ec),
        in_specs=[pl.BlockSpec((tm, d), lambda t, c: (t, 0)),
                  rowspec, rowspec, colspec, colspec,
                  pl.BlockSpec((ec, d), lambda t, c: (c, 0)),
                  pl.BlockSpec((d, ec), lambda t, c: (0, c))],
        out_specs=pl.BlockSpec((tm, d), lambda t, c: (t, 0)),
        out_shape=jax.ShapeDtypeStruct((ntok, d), F32),
        scratch_shapes=[pltpu.VMEM((d, tm), F32)],
        compiler_params=_cparams(("parallel", "arbitrary")),
        name="peer_dense",
    )(h2, a, ni, r1, bv, u, vt)


def _final_kernel(x_ref, peer_ref, mod_ref, g_ref, o_ref):
    x = x_ref[...] + mod_ref[5:6, :] * peer_ref[...]
    o_ref[...] = _rms(x) * g_ref[...]


def _final_call(x, peer, mod, fg, n_ctx_tiles):
    b, t, d = x.shape
    nl = t // TILE - n_ctx_tiles
    tok = lambda bi, i: (bi, i + n_ctx_tiles, 0)
    return pl.pallas_call(
        _final_kernel,
        grid=(b, nl),
        in_specs=[pl.BlockSpec((None, TILE, d), tok), pl.BlockSpec((None, TILE, d), tok),
                  pl.BlockSpec((None, None, 6, d), lambda bi, i: (bi, 1, 0, 0)),
                  pl.BlockSpec(fg.shape, lambda bi, i: (0, 0))],
        out_specs=pl.BlockSpec((None, TILE, d), lambda bi, i: (bi, i, 0)),
        out_shape=jax.ShapeDtypeStruct((b, nl * TILE, d), F32),
        compiler_params=_cparams(("parallel", "parallel")),
        name="final_norm",
    )(x, peer, mod, fg)


def _rope_tables(seq, n_ctx):
    rows = seq // GRID_W
    axis_rot = DIFF_DH // 2
    row = jnp.repeat(jnp.arange(rows), GRID_W).astype(F32)
    col = jnp.tile(jnp.arange(GRID_W), rows).astype(F32)
    inv = ROPE_BASE ** (-jnp.arange(0, axis_rot, 2, dtype=F32) / axis_rot)
    ang = jnp.concatenate([row[:, None] * inv, col[:, None] * inv], axis=-1)
    cos = jnp.repeat(jnp.cos(ang), 2, axis=-1)
    sin = jnp.repeat(jnp.sin(ang), 2, axis=-1)
    even = (jnp.arange(DIFF_DH) % 2 == 0)[None, :]
    sa = jnp.where(even, -sin, 0.0)
    sb = jnp.where(even, 0.0, sin)
    rep = LANES // DIFF_DH

    def full(tab, ctx_val):
        tab = jnp.tile(tab, (1, rep))
        return jnp.concatenate([jnp.full((n_ctx, LANES), ctx_val, F32), tab], axis=0)

    return full(cos, 1.0), full(sa, 0.0), full(sb, 0.0)


def kernel(x, c, ctx, c_ctx, ada_w, ada_b, norm1_g, norm2_g, w_in, conv_w, conv_b, conv_ln_g, conv_ln_b, mlstm_gate_b, mlstm_norm_g, diff_lambda, diff_norm_g, w_out, peer_wq, peer_keys, peer_u, peer_v, final_g):
    b, seq, d = x.shape
    n_ctx = ctx.shape[1]
    depth = ada_w.shape[0]
    cw, mw, dw = d // 4, d // 4, d // 2
    ng = 4 * MLSTM_HEADS
    assert n_ctx == TILE and seq % ATT_KC == 0 and seq % GRID_W == 0
    assert w_in.shape[-1] == 2 * cw + 4 * mw + ng + 3 * dw
    assert peer_keys.shape[1:] == (PEER_HEADS, 2, N_KEYS, d // PEER_HEADS // 2)
    assert conv_w.shape[1] == CONV_K and (b * (seq + n_ctx)) % PEER_TM == 0

    rows = -(-(b + 1) // SUBLANES) * SUBLANES
    cvec = jnp.zeros((rows, d), F32).at[:b].set(c).at[b].set(c_ctx)
    mods = _ada_call(cvec, ada_w, ada_b)
    mod_lat = mods[:, :b].reshape(depth, b, 1, 6, d)
    mod_ctx = jnp.broadcast_to(mods[:, b].reshape(depth, 1, 1, 6, d), (depth, b, 1, 6, d))
    mod_all = jnp.concatenate([mod_ctx, mod_lat], axis=2)

    cos, sa, sb = _rope_tables(seq, n_ctx)
    g0 = 2 * cw + 4 * mw
    w_main = jnp.concatenate([w_in[:, :, :g0], w_in[:, :, g0 + ng:]], axis=-1).astype(BF16)
    w_g = w_in[:, :, g0:g0 + ng]
    w_gt = jnp.swapaxes(w_g, 1, 2)
    gbias = mlstm_gate_b.reshape(depth, ng)
    w_out_b = w_out.astype(BF16)
    wq_b = peer_wq.astype(BF16)
    hw = d // PEER_HEADS
    kz = jnp.zeros((depth, PEER_HEADS, N_KEYS, hw // 2), F32)
    kh = jnp.concatenate([jnp.concatenate([peer_keys[:, :, 0], kz], axis=-1),
                          jnp.concatenate([kz, peer_keys[:, :, 1]], axis=-1)], axis=2).astype(BF16)
    u_b = peer_u.astype(BF16)
    vt_b = jnp.swapaxes(peer_v, 1, 2).astype(BF16)
    mng = jnp.tile(mlstm_norm_g, (1, MLSTM_HEADS))

    xs = jnp.concatenate([ctx, x], axis=1)
    t = xs.shape[1]
    peer = None
    for l in range(depth):
        lam_init = 0.8 - 0.6 * math.exp(-0.3 * l)
        xs, u, mqkv, mo, gcol, grow, dq, dk, dv = _inproj_call(
            xs, peer, mod_all[l - 1] if l else None, mod_all[l], norm1_g[l][None], w_main[l],
            w_g[l], w_gt[l], gbias[l][None], gbias[l][:, None], cos, sa, sb)
        conv = _conv_call(u, conv_w[l, :, 0, :], conv_b[l][None], conv_ln_g[l][None], conv_ln_b[l][None])
        gcol_d = gcol.reshape(b, t, 2, ng // 2).transpose(0, 2, 1, 3)
        grow_d = grow.reshape(b, 2, ng // 2, t)
        hdir = _mlstm_call(mqkv, gcol_d, grow_d)
        dy = _attn_call(dq, dk, dv, diff_lambda[l], diff_norm_g[l][None], lam_init)
        xs, h2 = _outproj_call(xs, conv, hdir, mo, dy, mod_all[l], mng[l][None], norm2_g[l][None], w_out_b[l])
        h2f = h2.reshape(b * t, d)
        a, ni, r1, bv = _peer_sel_call(h2f, wq_b[l], kh[l])
        peer = _peer_dense_call(h2f, a, ni, r1, bv, u_b[l], vt_b[l]).reshape(b, t, d)
    return _final_call(xs, peer, mod_all[depth - 1], final_g[None], n_ctx // TILE)
```

```python
import functools
import math

import jax
import jax.numpy as jnp
from jax import lax
from jax.experimental import pallas as pl
from jax.experimental.pallas import tpu as pltpu

F32 = jnp.float32
BF16 = jnp.bfloat16
HIGHEST = lax.Precision.HIGHEST

GRID_W = 64
EPS = 1e-6
CONV_K = 31
MLSTM_DH = 64
MLSTM_HEADS = 4
DIFF_DH = 64
DIFF_DV = 128
DIFF_HEADS = 4
ROPE_BASE = 10000.0
PEER_HEADS = 8
N_KEYS = 128
PEER_TOPK = 16

LANES = 128
SUBLANES = 8
TILE = 256
CONV_HALO = 16
ATT_KC = 1024
PEER_TM = 1024
BF16_SUBLANES = 16
PEER_EC = 1024
VMEM_LIMIT = 56 * 1024 * 1024

NT_DIMS = (((1,), (1,)), ((), ()))


def _cparams(sem):
    return pltpu.CompilerParams(dimension_semantics=sem, vmem_limit_bytes=VMEM_LIMIT)


def _rms(x, eps=EPS):
    return x * lax.rsqrt(jnp.mean(x * x, axis=-1, keepdims=True) + eps)


def _log_sigmoid(x):
    return jnp.minimum(x, 0.0) - jnp.log(1.0 + jnp.exp(-jnp.abs(x)))


def _ada_kernel(c_ref, w_ref, b_ref, o_ref):
    c = c_ref[...]
    s = (c * jax.nn.sigmoid(c)).astype(BF16)
    o_ref[...] = jnp.dot(s, w_ref[...].astype(BF16), preferred_element_type=F32) + b_ref[...]


def _ada_call(cvec, ada_w, ada_b):
    depth, d, n = ada_w.shape
    tn = 1536
    rows = cvec.shape[0]
    return pl.pallas_call(
        _ada_kernel,
        grid=(depth, n // tn),
        in_specs=[pl.BlockSpec((rows, d), lambda l, j: (0, 0)),
                  pl.BlockSpec((None, d, tn), lambda l, j: (l, 0, j)),
                  pl.BlockSpec((None, 1, tn), lambda l, j: (l, 0, j))],
        out_specs=pl.BlockSpec((None, rows, tn), lambda l, j: (l, 0, j)),
        out_shape=jax.ShapeDtypeStruct((depth, rows, n), F32),
        compiler_params=_cparams(("parallel", "parallel")),
        name="ada_mod",
    )(cvec, ada_w, ada_b.reshape(depth, 1, n))


def _rope(t, c, sa, sb):
    w = t.shape[1]
    rep = w // LANES
    c, sa, sb = (jnp.concatenate([z] * rep, axis=1) for z in (c, sa, sb))
    return t * c + pltpu.roll(t, w - 1, 1) * sa + pltpu.roll(t, 1, 1) * sb


def _inproj_kernel(*refs, has_peer, cw, mw):
    if has_peer:
        x_ref, peer_ref, modp_ref = refs[:3]
        refs = refs[3:]
    (mod_ref, n1g_ref, wm_ref, wg_ref, wgt_ref, wdvt_ref, gbc_ref, gbr_ref, cos_ref, sa_ref, sb_ref,
     xo_ref, u_ref, mqkv_ref, mo_ref, gcol_ref, grow_ref, dq_ref, dk_ref, dvt_ref) = refs[-20:]
    if not has_peer:
        x_ref = refs[0]
    x = x_ref[...]
    if has_peer:
        x = x + modp_ref[5:6, :] * peer_ref[...]
        xo_ref[...] = x
    else:
        xo_ref[...] = x
    h = _rms(x) * n1g_ref[...] * (1.0 + mod_ref[1:2, :]) + mod_ref[0:1, :]
    hb = h.astype(BF16)

    def proj(lo, hi):
        return jnp.dot(hb, wm_ref[:, lo:hi], preferred_element_type=F32)

    a = proj(0, 2 * cw)
    u_ref[...] = a[:, :cw] * jax.nn.sigmoid(a[:, cw:])
    o = 2 * cw
    mqkv_ref[:, 0:mw] = proj(o, o + mw).astype(BF16)
    mqkv_ref[:, mw:2 * mw] = (proj(o + mw, o + 2 * mw) * (MLSTM_DH ** -0.5)).astype(BF16)
    mqkv_ref[:, 2 * mw:3 * mw] = proj(o + 2 * mw, o + 3 * mw).astype(BF16)
    mo_ref[...] = proj(o + 3 * mw, o + 4 * mw)
    o = o + 4 * mw
    dw = dq_ref.shape[1]
    c, sa, sb = cos_ref[...], sa_ref[...], sb_ref[...]
    dq_ref[...] = (_rope(proj(o, o + dw), c, sa, sb) * (DIFF_DH ** -0.5 * math.log2(math.e))).astype(BF16)
    dk_ref[...] = _rope(proj(o + dw, o + 2 * dw), c, sa, sb).astype(BF16)
    dvt_ref[...] = lax.dot_general(wdvt_ref[...], hb, NT_DIMS, preferred_element_type=F32).astype(BF16)

    g = jnp.dot(h, wg_ref[...], precision=HIGHEST, preferred_element_type=F32) + gbc_ref[...]
    gt = lax.dot_general(wgt_ref[...], h, NT_DIMS, precision=HIGHEST,
                         preferred_element_type=F32) + gbr_ref[...]
    nh = MLSTM_HEADS
    cidx = lax.broadcasted_iota(jnp.int32, g.shape, 1)
    ridx = lax.broadcasted_iota(jnp.int32, gt.shape, 0)
    gcol_ref[...] = jnp.where((cidx // nh) % 2 == 1, _log_sigmoid(g), g)
    grow_ref[...] = jnp.where((ridx // nh) % 2 == 1, _log_sigmoid(gt), gt)


def _inproj_call(x, peer, modp, mod, n1g, wm, wg, wgt, wdvt, gbc, gbr, cos, sa, sb):
    b, t, d = x.shape
    nt = t // TILE
    cw = d // 4
    mw = d // 4
    dw = d // 2
    has_peer = peer is not None
    tok = lambda bi, i: (bi, i, 0)
    modspec = pl.BlockSpec((None, None, 6, d), lambda bi, i: (bi, jnp.minimum(i, 1), 0, 0))
    full2 = lambda arr: pl.BlockSpec(arr.shape, lambda bi, i: (0, 0))
    in_specs = [pl.BlockSpec((None, TILE, d), tok)]
    args = [x]
    if has_peer:
        in_specs += [pl.BlockSpec((None, TILE, d), tok), modspec]
        args += [peer, modp]
    in_specs += [modspec, full2(n1g), full2(wm), full2(wg), full2(wgt), full2(wdvt), full2(gbc), full2(gbr)]
    args += [mod, n1g, wm, wg, wgt, wdvt, gbc, gbr]
    in_specs += [pl.BlockSpec((TILE, LANES), lambda bi, i: (i, 0))] * 3
    args += [cos, sa, sb]
    ng = wg.shape[1]
    out_shape = [jax.ShapeDtypeStruct((b, t, d), F32),
                 jax.ShapeDtypeStruct((b, t, cw), F32),
                 jax.ShapeDtypeStruct((b, t, 3 * mw), BF16),
                 jax.ShapeDtypeStruct((b, t, mw), F32),
                 jax.ShapeDtypeStruct((b, t, ng), F32),
                 jax.ShapeDtypeStruct((b, ng, t), F32),
                 jax.ShapeDtypeStruct((b, t, dw), BF16),
                 jax.ShapeDtypeStruct((b, t, dw), BF16),
                 jax.ShapeDtypeStruct((b, dw, t), BF16)]
    out_specs = [pl.BlockSpec((None, TILE, d), tok),
                 pl.BlockSpec((None, TILE, cw), tok),
                 pl.BlockSpec((None, TILE, 3 * mw), tok),
                 pl.BlockSpec((None, TILE, mw), tok),
                 pl.BlockSpec((None, TILE, ng), tok),
                 pl.BlockSpec((None, ng, TILE), lambda bi, i: (bi, 0, i)),
                 pl.BlockSpec((None, TILE, dw), tok),
                 pl.BlockSpec((None, TILE, dw), tok),
                 pl.BlockSpec((None, dw, TILE), lambda bi, i: (bi, 0, i))]
    return pl.pallas_call(
        functools.partial(_inproj_kernel, has_peer=has_peer, cw=cw, mw=mw),
        grid=(b, nt), in_specs=in_specs, out_specs=out_specs, out_shape=out_shape,
        compiler_params=_cparams(("parallel", "parallel")),
        name="in_proj",
    )(*args)


def _conv_kernel(up_ref, uc_ref, un_ref, w_ref, b_ref, lg_ref, lb_ref, o_ref, ext_ref):
    i = pl.program_id(1)
    nt = pl.num_programs(1)
    lm = jnp.where(i >= 2, 1.0, 0.0)
    rm = jnp.where(jnp.logical_and(i >= 1, i < nt - 1), 1.0, 0.0)
    hl = CONV_HALO
    ext_ref[0:hl, :] = up_ref[TILE - hl:TILE, :] * lm
    ext_ref[hl:hl + TILE, :] = uc_ref[...]
    ext_ref[hl + TILE:2 * hl + TILE, :] = un_ref[0:hl, :] * rm
    off = hl - CONV_K // 2
    acc = jnp.zeros(uc_ref.shape, F32)
    for k in range(CONV_K):
        acc = acc + w_ref[k:k + 1, :] * ext_ref[off + k:off + k + TILE, :]
    y = acc + b_ref[...]
    mu = jnp.mean(y, axis=-1, keepdims=True)
    yc = y - mu
    var = jnp.mean(yc * yc, axis=-1, keepdims=True)
    z = yc * lax.rsqrt(var + EPS) * lg_ref[...] + lb_ref[...]
    o_ref[...] = (z * jax.nn.sigmoid(z)).astype(o_ref.dtype)


def _conv_call(u, w, bias, lg, lb):
    b, t, cw = u.shape
    nt = t // TILE
    full2 = lambda arr: pl.BlockSpec(arr.shape, lambda bi, i: (0, 0))
    return pl.pallas_call(
        _conv_kernel,
        grid=(b, nt),
        in_specs=[pl.BlockSpec((None, TILE, cw), lambda bi, i: (bi, jnp.maximum(i - 1, 0), 0)),
                  pl.BlockSpec((None, TILE, cw), lambda bi, i: (bi, i, 0)),
                  pl.BlockSpec((None, TILE, cw), lambda bi, i: (bi, jnp.minimum(i + 1, nt - 1), 0)),
                  full2(w), full2(bias), full2(lg), full2(lb)],
        out_specs=pl.BlockSpec((None, TILE, cw), lambda bi, i: (bi, i, 0)),
        out_shape=jax.ShapeDtypeStruct((b, t, cw), BF16),
        scratch_shapes=[pltpu.VMEM((TILE + 2 * CONV_HALO, cw), F32)],
        compiler_params=_cparams(("parallel", "parallel")),
        name="conv_module",
    )(u, u, u, w, bias, lg, lb)


def _mlstm_kernel(qkv_ref, gc_ref, gr_ref, h_ref, c_ref, n_ref, m_ref):
    d = pl.program_id(0)
    j = pl.program_id(2)

    @pl.when(j == 0)
    def _():
        c_ref[...] = jnp.zeros_like(c_ref)
        n_ref[...] = jnp.zeros_like(n_ref)
        m_ref[...] = jnp.zeros_like(m_ref)

    tc = qkv_ref.shape[0]
    mw = qkv_ref.shape[1] // 3
    nh = MLSTM_HEADS
    fwd = d == 0
    row = lax.broadcasted_iota(jnp.int32, (tc, tc), 0)
    col = lax.broadcasted_iota(jnp.int32, (tc, tc), 1)
    tri = (row - col) * jnp.where(fwd, 1, -1) >= 0
    trif = tri.astype(F32)
    gc = gc_ref[...]
    gr = gr_ref[...]
    bcol = jnp.dot(trif, gc, precision=HIGHEST, preferred_element_type=F32)
    brow = lax.dot_general(gr, trif, NT_DIMS, precision=HIGHEST, preferred_element_type=F32)
    bl = jnp.sum(gr, axis=1, keepdims=True)

    q = qkv_ref[:, 0:mw]
    k = qkv_ref[:, mw:2 * mw]
    v = qkv_ref[:, 2 * mw:3 * mw]
    lane_head = lax.broadcasted_iota(jnp.int32, (tc, mw), 1) // MLSTM_DH
    cb = c_ref[...]
    inter_c = lax.dot_general(q, cb.astype(BF16), NT_DIMS, preferred_element_type=F32)
    qf = q.astype(F32)
    qn = qf * n_ref[...]

    out = jnp.zeros((tc, mw), F32)
    wkfull = jnp.zeros((tc, mw), F32)
    rhead = lax.broadcasted_iota(jnp.int32, (mw, 1), 0) // MLSTM_DH
    chead = lax.broadcasted_iota(jnp.int32, (1, mw), 1) // MLSTM_DH
    decay_col = jnp.zeros((mw, 1), F32)
    decay_row = jnp.zeros((1, mw), F32)
    for h in range(nh):
        mh = m_ref[h:h + 1, 0:1]
        bc = bcol[:, nh + h:nh + h + 1]
        br = brow[nh + h:nh + h + 1, :]
        ir = gr[h:h + 1, :]
        dlog = jnp.where(tri, bc - br + ir, -jnp.inf)
        inter = bc + mh
        mt = jnp.maximum(inter, jnp.max(dlog, axis=1, keepdims=True))
        dwt = jnp.exp(dlog - mt)
        iw = jnp.exp(inter - mt)
        hm = lane_head == h
        qh = jnp.where(hm, qf, 0.0).astype(BF16)
        s = lax.dot_general(qh, k, NT_DIMS, preferred_element_type=F32) * dwt
        sv = jnp.dot(s.astype(BF16), v, preferred_element_type=F32)
        qn_h = jnp.sum(jnp.where(hm, qn, 0.0), axis=1, keepdims=True)
        den = jnp.sum(s, axis=1, keepdims=True) + iw * qn_h
        denom = jnp.maximum(jnp.abs(den), jnp.exp(-mt))
        out = jnp.where(hm, (sv + iw * inter_c) / denom, out)
        blh = bl[nh + h:nh + h + 1, :]
        wlog_r = blh - br + ir
        mn = jnp.maximum(blh + mh, jnp.max(wlog_r, axis=1, keepdims=True))
        decay = jnp.exp(blh + mh - mn)
        wk_c = jnp.exp(blh - bc + gc[:, h:h + 1] - mn)
        wkfull = jnp.where(hm, wk_c, wkfull)
        decay_col = jnp.where(rhead == h, decay, decay_col)
        decay_row = jnp.where(chead == h, decay, decay_row)
        m_ref[h:h + 1, :] = jnp.broadcast_to(mn, (1, m_ref.shape[1]))
    h_ref[...] = out

    vw = v.astype(F32) * wkfull
    upd = jnp.dot(vw.T.astype(BF16), k, preferred_element_type=F32)
    c_ref[...] = decay_col * cb + jnp.where(rhead == chead, upd, 0.0)
    n_ref[...] = decay_row * n_ref[...] + jnp.sum(k.astype(F32) * wkfull, axis=0, keepdims=True)


def _mlstm_call(mqkv, gcol, grow):
    b, t, w3 = mqkv.shape
    mw = w3 // 3
    nt = t // TILE
    ng = gcol.shape[-1]

    def tile(d, j):
        return jnp.where(d == 0, j, jnp.where(j == 0, 0, nt - j))

    return pl.pallas_call(
        _mlstm_kernel,
        grid=(2, b, nt),
        in_specs=[pl.BlockSpec((None, TILE, w3), lambda d, bi, j: (bi, tile(d, j), 0)),
                  pl.BlockSpec((None, None, TILE, ng), lambda d, bi, j: (bi, d, tile(d, j), 0)),
                  pl.BlockSpec((None, None, ng, TILE), lambda d, bi, j: (bi, d, 0, tile(d, j)))],
        out_specs=pl.BlockSpec((None, None, TILE, mw), lambda d, bi, j: (d, bi, tile(d, j), 0)),
        out_shape=jax.ShapeDtypeStruct((2, b, t, mw), F32),
        scratch_shapes=[pltpu.VMEM((mw, mw), F32), pltpu.VMEM((1, mw), F32),
                        pltpu.VMEM((SUBLANES, LANES), F32)],
        compiler_params=_cparams(("arbitrary", "arbitrary", "arbitrary")),
        name="mlstm_scan",
    )(mqkv, gcol, grow)


def _attn_kernel(dl_ref, ng_ref, q_ref, k_ref, vt_ref, o_ref, s_ref, m_ref, l_ref, acc_ref, *,
                 lam_init, n_ctx, kc):
    i = pl.program_id(2)
    t_all = k_ref.shape[0]
    q = q_ref[...]
    lane = lax.broadcasted_iota(jnp.int32, q.shape, 1)
    zero = jnp.zeros_like(q)
    qz = (jnp.where(lane < DIFF_DH, q, zero), jnp.where(lane >= DIFF_DH, q, zero))
    m_ref[...] = jnp.full(m_ref.shape, -jnp.inf, F32)
    l_ref[...] = jnp.zeros_like(l_ref)
    acc_ref[...] = jnp.zeros_like(acc_ref)

    def scores(start, size, slot):
        kb = k_ref[pl.ds(start, size), :]
        for c in range(2):
            s_ref[slot, c, 0:size, :] = lax.dot_general(kb, qz[c], NT_DIMS, preferred_element_type=F32)

    def softmax_pv(start, size, slot):
        vtb = vt_ref[:, pl.ds(start, size)]
        for c in range(2):
            s = s_ref[slot, c, 0:size, :]
            m_old = m_ref[c]
            m_new = jnp.maximum(m_old, jnp.max(s, axis=0, keepdims=True))
            alpha = jnp.exp2(m_old - m_new)
            p = jnp.exp2(s - m_new)
            l_ref[c] = alpha * l_ref[c] + jnp.sum(p, axis=0, keepdims=True)
            acc_ref[c] = alpha * acc_ref[c] + jnp.dot(vtb, p.astype(BF16), preferred_element_type=F32)
            m_ref[c] = m_new

    @pl.when(i == 0)
    def _():
        scores(0, n_ctx, 0)
        softmax_pv(0, n_ctx, 0)

    @pl.when(i > 0)
    def _():
        n = t_all // kc
        pairs = (n - 1) // 2
        scores(0, kc, 0)

        def body(jj, carry):
            base = pl.multiple_of(jj * (2 * kc), TILE)
            scores(base + kc, kc, 1)
            softmax_pv(base, kc, 0)
            scores(base + 2 * kc, kc, 0)
            softmax_pv(base + kc, kc, 1)
            return carry

        lax.fori_loop(0, pairs, body, 0)
        done = 2 * pairs
        if n - done == 2:
            scores((done + 1) * kc, kc, 1)
        softmax_pv(done * kc, kc, 0)
        if n - done == 2:
            softmax_pv((done + 1) * kc, kc, 1)

    dl = dl_ref[...]
    lam = (jnp.exp(jnp.sum(dl[0:1] * dl[1:2], axis=1, keepdims=True))
           - jnp.exp(jnp.sum(dl[2:3] * dl[3:4], axis=1, keepdims=True)) + lam_init)
    ot = acc_ref[0] / l_ref[0] - lam * (acc_ref[1] / l_ref[1])
    ms = jnp.mean(ot * ot, axis=0, keepdims=True)
    y = ot * lax.rsqrt(ms + EPS) * (ng_ref[...] * (1.0 - lam_init))
    o_ref[...] = y.T.astype(o_ref.dtype)


def _attn_key_chunk(t):
    return max(k for k in range(TILE, ATT_KC + 1, TILE) if t % k == 0)


def _attn_call(dq, dk, dvt, dlam, dng, lam_init):
    b, t, w = dq.shape
    nh = w // DIFF_DV
    nq = t // TILE
    kc = _attn_key_chunk(t)
    kern = functools.partial(_attn_kernel, lam_init=lam_init, n_ctx=TILE, kc=kc)
    return pl.pallas_call(
        kern,
        grid=(b, nh, nq),
        in_specs=[pl.BlockSpec(dlam.shape, lambda bi, h, i: (0, 0)),
                  pl.BlockSpec(dng.shape, lambda bi, h, i: (0, 0)),
                  pl.BlockSpec((None, TILE, DIFF_DV), lambda bi, h, i: (bi, i, h)),
                  pl.BlockSpec((None, t, DIFF_DV), lambda bi, h, i: (bi, 0, h)),
                  pl.BlockSpec((None, DIFF_DV, t), lambda bi, h, i: (bi, h, 0))],
        out_specs=pl.BlockSpec((None, TILE, DIFF_DV), lambda bi, h, i: (bi, i, h)),
        out_shape=jax.ShapeDtypeStruct((b, t, w), BF16),
        scratch_shapes=[pltpu.VMEM((2, 2, kc, TILE), F32),
                        pltpu.VMEM((2, 1, TILE), F32), pltpu.VMEM((2, 1, TILE), F32),
                        pltpu.VMEM((2, DIFF_DV, TILE), F32)],
        compiler_params=_cparams(("parallel", "parallel", "arbitrary")),
        name="diff_attn",
    )(dlam, dng, dq, dk, dvt)


def _outproj_kernel(x_ref, conv_ref, hf_ref, hb_ref, mo_ref, dy_ref, mod_ref, mng_ref, n2g_ref,
                    wo_ref, xo_ref, h2_ref, h2t_ref):
    cw = conv_ref.shape[1]
    mw = hf_ref.shape[1]
    hh = hf_ref[...] + hb_ref[...]
    r = lax.broadcasted_iota(jnp.int32, (mw, mw), 0) // MLSTM_DH
    c = lax.broadcasted_iota(jnp.int32, (mw, mw), 1) // MLSTM_DH
    gm = jnp.where(r == c, 1.0 / MLSTM_DH, 0.0)
    ms = jnp.dot(hh * hh, gm, precision=HIGHEST, preferred_element_type=F32)
    ym = hh * lax.rsqrt(ms + EPS) * mng_ref[...] * jax.nn.sigmoid(mo_ref[...])
    o = (jnp.dot(conv_ref[...], wo_ref[0:cw, :], preferred_element_type=F32)
         + jnp.dot(ym.astype(BF16), wo_ref[cw:cw + mw, :], preferred_element_type=F32)
         + jnp.dot(dy_ref[...], wo_ref[cw + mw:, :], preferred_element_type=F32))
    x = x_ref[...] + mod_ref[2:3, :] * o
    xo_ref[...] = x
    h2 = _rms(x) * n2g_ref[...] * (1.0 + mod_ref[4:5, :]) + mod_ref[3:4, :]
    h2_ref[...] = h2.astype(BF16)
    h2t_ref[...] = h2.T.astype(BF16)


def _outproj_call(x, conv, hdir, mo, dy, mod, mng, n2g, wo):
    b, t, d = x.shape
    nt = t // TILE
    cw, mw, dw = conv.shape[-1], mo.shape[-1], dy.shape[-1]
    tok = lambda bi, i: (bi, i, 0)
    full2 = lambda arr: pl.BlockSpec(arr.shape, lambda bi, i: (0, 0))
    return pl.pallas_call(
        _outproj_kernel,
        grid=(b, nt),
        in_specs=[pl.BlockSpec((None, TILE, d), tok),
                  pl.BlockSpec((None, TILE, cw), tok),
                  pl.BlockSpec((None, None, TILE, mw), lambda bi, i: (0, bi, i, 0)),
                  pl.BlockSpec((None, None, TILE, mw), lambda bi, i: (1, bi, i, 0)),
                  pl.BlockSpec((None, TILE, mw), tok),
                  pl.BlockSpec((None, TILE, dw), tok),
                  pl.BlockSpec((None, None, 6, d), lambda bi, i: (bi, jnp.minimum(i, 1), 0, 0)),
                  full2(mng), full2(n2g), full2(wo)],
        out_specs=[pl.BlockSpec((None, TILE, d), tok), pl.BlockSpec((None, TILE, d), tok),
                   pl.BlockSpec((d, TILE), lambda bi, i: (0, bi * nt + i))],
        out_shape=[jax.ShapeDtypeStruct((b, t, d), F32), jax.ShapeDtypeStruct((b, t, d), BF16),
                   jax.ShapeDtypeStruct((d, b * t), BF16)],
        compiler_params=_cparams(("parallel", "parallel")),
        name="out_proj",
    )(x, conv, hdir, hdir, mo, dy, mod, mng, n2g, wo)


def _top_k_rows(s, k):
    idx = lax.broadcasted_iota(jnp.int32, s.shape, 0).astype(F32)
    vals, ids = [], []
    cur = s
    for _ in range(k):
        m = jnp.max(cur, axis=0, keepdims=True)
        am = jnp.min(jnp.where(cur == m, idx, float(s.shape[0])), axis=0, keepdims=True)
        vals.append(m)
        ids.append(am)
        cur = jnp.where(idx == am, -jnp.inf, cur)
    return vals, ids


def _peer_sel_kernel(h2_ref, wq_ref, kh_ref, a_ref, ni_ref, r1_ref, bv_ref):
    kk = PEER_TOPK
    q = jnp.dot(h2_ref[...], wq_ref[...], preferred_element_type=F32).astype(BF16)
    tt = q.shape[0]
    hw = q.shape[1] // PEER_HEADS
    iota_k = lax.broadcasted_iota(jnp.int32, (kk, tt), 0).astype(F32)
    iota_n = lax.broadcasted_iota(jnp.int32, (N_KEYS, tt), 0).astype(F32)
    for h in range(PEER_HEADS):
        st_ = lax.dot_general(kh_ref[h], q[:, h * hw:(h + 1) * hw], NT_DIMS,
                              preferred_element_type=F32)
        s0 = st_[0:N_KEYS]
        s1 = st_[N_KEYS:2 * N_KEYS]
        v0, i0 = _top_k_rows(s0, kk)
        v1, i1 = _top_k_rows(s1, kk)
        st0 = jnp.concatenate(v0, axis=0)
        n = jnp.zeros((kk, tt), F32)
        hv = st0 + v1[0]
        mx = v0[0] + v1[0]
        zsum = jnp.zeros((1, tt), F32)
        for _ in range(kk):
            m = jnp.max(hv, axis=0, keepdims=True)
            a_star = jnp.min(jnp.where(hv == m, iota_k, float(kk)), axis=0, keepdims=True)
            sel = iota_k == a_star
            zsum = zsum + jnp.exp(m - mx)
            n = n + jnp.where(sel, 1.0, 0.0)
            nxt = jnp.full((kk, tt), -jnp.inf, F32)
            for b in range(1, kk):
                nxt = jnp.where(n == float(b), v1[b], nxt)
            hv = jnp.where(sel, st0 + nxt, hv)
        ni = jnp.zeros((N_KEYS, tt), F32)
        r1 = jnp.full((N_KEYS, tt), float(kk), F32)
        for a in range(kk):
            ni = jnp.where(iota_n == i0[a], n[a:a + 1, :], ni)
            r1 = jnp.where(iota_n == i1[a], float(a), r1)
        a_ref[h] = jnp.exp(s0 - v0[0]) / zsum
        ni_ref[h] = ni
        r1_ref[h] = r1.astype(BF16)
        bv_ref[h] = jnp.exp(s1 - v1[0]).astype(BF16)


def _peer_sel_call(h2, wq, kh):
    ntok, d = h2.shape
    nt = ntok // TILE
    shp = jax.ShapeDtypeStruct((PEER_HEADS, N_KEYS, ntok), F32)
    shp_b = jax.ShapeDtypeStruct((PEER_HEADS, N_KEYS, ntok), BF16)
    ospec = pl.BlockSpec((PEER_HEADS, N_KEYS, TILE), lambda i: (0, 0, i))
    return pl.pallas_call(
        _peer_sel_kernel,
        grid=(nt,),
        in_specs=[pl.BlockSpec((TILE, d), lambda i: (i, 0)),
                  pl.BlockSpec(wq.shape, lambda i: (0, 0)),
                  pl.BlockSpec(kh.shape, lambda i: (0, 0, 0))],
        out_specs=[ospec] * 4,
        out_shape=[shp, shp, shp_b, shp_b],
        compiler_params=_cparams(("parallel",)),
        name="peer_select",
    )(h2, wq, kh)


def _gelu(x):
    return 0.5 * x * (1.0 + lax.erf(x * (2.0 ** -0.5)))


def _peer_dense_kernel(h2t_ref, a_ref, ni_ref, r1_ref, bv_ref, u_ref, vt_ref, o_ref, acc_ref):
    c = pl.program_id(1)

    @pl.when(c == 0)
    def _():
        acc_ref[...] = jnp.zeros_like(acc_ref)

    tm = h2t_ref.shape[1]
    rep = N_KEYS // BF16_SUBLANES
    h2t = h2t_ref[...]

    def rows(ref, h, ib):
        r = jnp.broadcast_to(ref[h, ib:ib + 1, :], (BF16_SUBLANES, tm)).astype(BF16)
        return pltpu.repeat(r, rep, axis=0)

    st_ = jnp.dot(u_ref[...], h2t, preferred_element_type=F32)
    blocks = []
    for ib in range(u_ref.shape[0] // N_KEYS):
        g = jnp.zeros((N_KEYS, tm), BF16)
        for h in range(PEER_HEADS):
            bvh = bv_ref[h]
            keep = r1_ref[h] < rows(ni_ref, h, ib)
            g = g + rows(a_ref, h, ib) * jnp.where(keep, bvh, jnp.zeros_like(bvh))
        blocks.append(g * _gelu(st_[ib * N_KEYS:(ib + 1) * N_KEYS]).astype(BF16))
    wt = jnp.concatenate(blocks, axis=0)
    acc_ref[...] += jnp.dot(vt_ref[...], wt, preferred_element_type=F32)

    @pl.when(c == pl.num_programs(1) - 1)
    def _():
        o_ref[...] = acc_ref[...].T


def _peer_dense_call(h2t, a, ni, r1, bv, u, vt):
    d, ntok = h2t.shape
    ne = u.shape[0]
    tm, ec = PEER_TM, PEER_EC
    ib = ec // N_KEYS
    rowspec = pl.BlockSpec((PEER_HEADS, ib, tm), lambda t, c: (0, c, t))
    colspec = pl.BlockSpec((PEER_HEADS, N_KEYS, tm), lambda t, c: (0, 0, t))
    return pl.pallas_call(
        _peer_dense_kernel,
        grid=(ntok // tm, ne // ec),
        in_specs=[pl.BlockSpec((d, tm), lambda t, c: (0, t)),
                  rowspec, rowspec, colspec, colspec,
                  pl.BlockSpec((ec, d), lambda t, c: (c, 0)),
                  pl.BlockSpec((d, ec), lambda t, c: (0, c))],
        out_specs=pl.BlockSpec((tm, d), lambda t, c: (t, 0)),
        out_shape=jax.ShapeDtypeStruct((ntok, d), F32),
        scratch_shapes=[pltpu.VMEM((d, tm), F32)],
        compiler_params=_cparams(("parallel", "arbitrary")),
        name="peer_dense",
    )(h2t, a, ni, r1, bv, u, vt)


def _final_kernel(x_ref, peer_ref, mod_ref, g_ref, o_ref):
    x = x_ref[...] + mod_ref[5:6, :] * peer_ref[...]
    o_ref[...] = _rms(x) * g_ref[...]


def _final_call(x, peer, mod, fg, n_ctx_tiles):
    b, t, d = x.shape
    nl = t // TILE - n_ctx_tiles
    tok = lambda bi, i: (bi, i + n_ctx_tiles, 0)
    return pl.pallas_call(
        _final_kernel,
        grid=(b, nl),
        in_specs=[pl.BlockSpec((None, TILE, d), tok), pl.BlockSpec((None, TILE, d), tok),
                  pl.BlockSpec((None, None, 6, d), lambda bi, i: (bi, 1, 0, 0)),
                  pl.BlockSpec(fg.shape, lambda bi, i: (0, 0))],
        out_specs=pl.BlockSpec((None, TILE, d), lambda bi, i: (bi, i, 0)),
        out_shape=jax.ShapeDtypeStruct((b, nl * TILE, d), F32),
        compiler_params=_cparams(("parallel", "parallel")),
        name="final_norm",
    )(x, peer, mod, fg)


def _rope_tables(seq, n_ctx):
    rows = seq // GRID_W
    axis_rot = DIFF_DH // 2
    row = jnp.repeat(jnp.arange(rows), GRID_W).astype(F32)
    col = jnp.tile(jnp.arange(GRID_W), rows).astype(F32)
    inv = ROPE_BASE ** (-jnp.arange(0, axis_rot, 2, dtype=F32) / axis_rot)
    ang = jnp.concatenate([row[:, None] * inv, col[:, None] * inv], axis=-1)
    cos = jnp.repeat(jnp.cos(ang), 2, axis=-1)
    sin = jnp.repeat(jnp.sin(ang), 2, axis=-1)
    even = (jnp.arange(DIFF_DH) % 2 == 0)[None, :]
    sa = jnp.where(even, -sin, 0.0)
    sb = jnp.where(even, 0.0, sin)
    rep = LANES // DIFF_DH

    def full(tab, ctx_val):
        tab = jnp.tile(tab, (1, rep))
        return jnp.concatenate([jnp.full((n_ctx, LANES), ctx_val, F32), tab], axis=0)

    return full(cos, 1.0), full(sa, 0.0), full(sb, 0.0)


def kernel(x, c, ctx, c_ctx, ada_w, ada_b, norm1_g, norm2_g, w_in, conv_w, conv_b, conv_ln_g, conv_ln_b, mlstm_gate_b, mlstm_norm_g, diff_lambda, diff_norm_g, w_out, peer_wq, peer_keys, peer_u, peer_v, final_g):
    b, seq, d = x.shape
    n_ctx = ctx.shape[1]
    depth = ada_w.shape[0]
    cw, mw, dw = d // 4, d // 4, d // 2
    ng = 4 * MLSTM_HEADS
    assert n_ctx == TILE and seq % TILE == 0 and seq % GRID_W == 0
    assert w_in.shape[-1] == 2 * cw + 4 * mw + ng + 3 * dw
    assert peer_keys.shape[1:] == (PEER_HEADS, 2, N_KEYS, d // PEER_HEADS // 2)
    assert conv_w.shape[1] == CONV_K and (b * (seq + n_ctx)) % PEER_TM == 0

    rows = -(-(b + 1) // SUBLANES) * SUBLANES
    cvec = jnp.zeros((rows, d), F32).at[:b].set(c).at[b].set(c_ctx)
    mods = _ada_call(cvec, ada_w, ada_b)
    mod_lat = mods[:, :b].reshape(depth, b, 1, 6, d)
    mod_ctx = jnp.broadcast_to(mods[:, b].reshape(depth, 1, 1, 6, d), (depth, b, 1, 6, d))
    mod_all = jnp.concatenate([mod_ctx, mod_lat], axis=2)

    cos, sa, sb = _rope_tables(seq, n_ctx)
    g0 = 2 * cw + 4 * mw
    w_main = jnp.concatenate([w_in[:, :, :g0], w_in[:, :, g0 + ng:g0 + ng + 2 * dw]], axis=-1).astype(BF16)
    w_dvt = jnp.swapaxes(w_in[:, :, g0 + ng + 2 * dw:], 1, 2).astype(BF16)
    w_g = w_in[:, :, g0:g0 + ng]
    w_gt = jnp.swapaxes(w_g, 1, 2)
    gbias = mlstm_gate_b.reshape(depth, ng)
    w_out_b = w_out.astype(BF16)
    wq_b = peer_wq.astype(BF16)
    hw = d // PEER_HEADS
    kz = jnp.zeros((depth, PEER_HEADS, N_KEYS, hw // 2), F32)
    kh = jnp.concatenate([jnp.concatenate([peer_keys[:, :, 0], kz], axis=-1),
                          jnp.concatenate([kz, peer_keys[:, :, 1]], axis=-1)], axis=2).astype(BF16)
    u_b = peer_u.astype(BF16)
    vt_b = jnp.swapaxes(peer_v, 1, 2).astype(BF16)
    mng = jnp.tile(mlstm_norm_g, (1, MLSTM_HEADS))

    xs = jnp.concatenate([ctx, x], axis=1)
    t = xs.shape[1]
    peer = None
    for l in range(depth):
        lam_init = 0.8 - 0.6 * math.exp(-0.3 * l)
        xs, u, mqkv, mo, gcol, grow, dq, dk, dvt = _inproj_call(
            xs, peer, mod_all[l - 1] if l else None, mod_all[l], norm1_g[l][None], w_main[l],
            w_g[l], w_gt[l], w_dvt[l], gbias[l][None], gbias[l][:, None], cos, sa, sb)
        conv = _conv_call(u, conv_w[l, :, 0, :], conv_b[l][None], conv_ln_g[l][None], conv_ln_b[l][None])
        gcol_d = gcol.reshape(b, t, 2, ng // 2).transpose(0, 2, 1, 3)
        grow_d = grow.reshape(b, 2, ng // 2, t)
        hdir = _mlstm_call(mqkv, gcol_d, grow_d)
        dy = _attn_call(dq, dk, dvt, diff_lambda[l], diff_norm_g[l][:, None], lam_init)
        xs, h2, h2t = _outproj_call(xs, conv, hdir, mo, dy, mod_all[l], mng[l][None], norm2_g[l][None], w_out_b[l])
        h2f = h2.reshape(b * t, d)
        a, ni, r1, bv = _peer_sel_call(h2f, wq_b[l], kh[l])
        peer = _peer_dense_call(h2t, a, ni, r1, bv, u_b[l], vt_b[l]).reshape(b, t, d)
    return _final_call(xs, peer, mod_all[depth - 1], final_g[None], n_ctx // TILE)
```

```python
import functools
import math

import jax
import jax.numpy as jnp
from jax import lax
from jax.experimental import pallas as pl
from jax.experimental.pallas import tpu as pltpu

F32 = jnp.float32
BF16 = jnp.bfloat16
HIGHEST = lax.Precision.HIGHEST

GRID_W = 64
EPS = 1e-6
CONV_K = 31
MLSTM_DH = 64
MLSTM_HEADS = 4
DIFF_DH = 64
DIFF_DV = 128
DIFF_HEADS = 4
ROPE_BASE = 10000.0
PEER_HEADS = 8
N_KEYS = 128
PEER_TOPK = 16

LANES = 128
SUBLANES = 8
TILE = 256
CONV_HALO = 16
ATT_KC = 1024
ATT_TQ = 512
PEER_TM = 1024
BF16_SUBLANES = 16
PEER_EC = 1024
VMEM_LIMIT = 56 * 1024 * 1024

NT_DIMS = (((1,), (1,)), ((), ()))


def _cparams(sem):
    return pltpu.CompilerParams(dimension_semantics=sem, vmem_limit_bytes=VMEM_LIMIT)


def _rms(x, eps=EPS):
    return x * lax.rsqrt(jnp.mean(x * x, axis=-1, keepdims=True) + eps)


def _log_sigmoid(x):
    return jnp.minimum(x, 0.0) - jnp.log(1.0 + jnp.exp(-jnp.abs(x)))


def _ada_kernel(c_ref, w_ref, b_ref, o_ref):
    c = c_ref[...]
    s = (c * jax.nn.sigmoid(c)).astype(BF16)
    o_ref[...] = jnp.dot(s, w_ref[...].astype(BF16), preferred_element_type=F32) + b_ref[...]


def _ada_call(cvec, ada_w, ada_b):
    depth, d, n = ada_w.shape
    tn = 1536
    rows = cvec.shape[0]
    return pl.pallas_call(
        _ada_kernel,
        grid=(depth, n // tn),
        in_specs=[pl.BlockSpec((rows, d), lambda l, j: (0, 0)),
                  pl.BlockSpec((None, d, tn), lambda l, j: (l, 0, j)),
                  pl.BlockSpec((None, 1, tn), lambda l, j: (l, 0, j))],
        out_specs=pl.BlockSpec((None, rows, tn), lambda l, j: (l, 0, j)),
        out_shape=jax.ShapeDtypeStruct((depth, rows, n), F32),
        compiler_params=_cparams(("parallel", "parallel")),
        name="ada_mod",
    )(cvec, ada_w, ada_b.reshape(depth, 1, n))


def _rope(t, c, sa, sb):
    w = t.shape[1]
    rep = w // LANES
    c, sa, sb = (jnp.concatenate([z] * rep, axis=1) for z in (c, sa, sb))
    return t * c + pltpu.roll(t, w - 1, 1) * sa + pltpu.roll(t, 1, 1) * sb


def _inproj_kernel(*refs, has_peer, cw, mw):
    if has_peer:
        x_ref, peer_ref, modp_ref = refs[:3]
        refs = refs[3:]
    (mod_ref, n1g_ref, wm_ref, wg_ref, wgt_ref, wdvt_ref, gbc_ref, gbr_ref, cos_ref, sa_ref, sb_ref,
     xo_ref, u_ref, mqkv_ref, mo_ref, gcol_ref, grow_ref, dq_ref, dk_ref, dvt_ref) = refs[-20:]
    if not has_peer:
        x_ref = refs[0]
    x = x_ref[...]
    if has_peer:
        x = x + modp_ref[5:6, :] * peer_ref[...]
        xo_ref[...] = x
    else:
        xo_ref[...] = x
    h = _rms(x) * n1g_ref[...] * (1.0 + mod_ref[1:2, :]) + mod_ref[0:1, :]
    hb = h.astype(BF16)

    def proj(lo, hi):
        return jnp.dot(hb, wm_ref[:, lo:hi], preferred_element_type=F32)

    a = proj(0, 2 * cw)
    u_ref[...] = a[:, :cw] * jax.nn.sigmoid(a[:, cw:])
    o = 2 * cw
    mqkv_ref[:, 0:mw] = proj(o, o + mw).astype(BF16)
    mqkv_ref[:, mw:2 * mw] = (proj(o + mw, o + 2 * mw) * (MLSTM_DH ** -0.5)).astype(BF16)
    mqkv_ref[:, 2 * mw:3 * mw] = proj(o + 2 * mw, o + 3 * mw).astype(BF16)
    mo_ref[...] = proj(o + 3 * mw, o + 4 * mw)
    o = o + 4 * mw
    dw = dq_ref.shape[1]
    c, sa, sb = cos_ref[...], sa_ref[...], sb_ref[...]
    dq_ref[...] = (_rope(proj(o, o + dw), c, sa, sb) * (DIFF_DH ** -0.5 * math.log2(math.e))).astype(BF16)
    dk_ref[...] = _rope(proj(o + dw, o + 2 * dw), c, sa, sb).astype(BF16)
    dvt_ref[...] = lax.dot_general(wdvt_ref[...], hb, NT_DIMS, preferred_element_type=F32).astype(BF16)

    g = jnp.dot(h, wg_ref[...], precision=HIGHEST, preferred_element_type=F32) + gbc_ref[...]
    gt = lax.dot_general(wgt_ref[...], h, NT_DIMS, precision=HIGHEST,
                         preferred_element_type=F32) + gbr_ref[...]
    nh = MLSTM_HEADS
    cidx = lax.broadcasted_iota(jnp.int32, g.shape, 1)
    ridx = lax.broadcasted_iota(jnp.int32, gt.shape, 0)
    gcol_ref[...] = jnp.where((cidx // nh) % 2 == 1, _log_sigmoid(g), g)
    grow_ref[...] = jnp.where((ridx // nh) % 2 == 1, _log_sigmoid(gt), gt)


def _inproj_call(x, peer, modp, mod, n1g, wm, wg, wgt, wdvt, gbc, gbr, cos, sa, sb):
    b, t, d = x.shape
    nt = t // TILE
    cw = d // 4
    mw = d // 4
    dw = d // 2
    has_peer = peer is not None
    tok = lambda bi, i: (bi, i, 0)
    modspec = pl.BlockSpec((None, None, 6, d), lambda bi, i: (bi, jnp.minimum(i, 1), 0, 0))
    full2 = lambda arr: pl.BlockSpec(arr.shape, lambda bi, i: (0, 0))
    in_specs = [pl.BlockSpec((None, TILE, d), tok)]
    args = [x]
    if has_peer:
        in_specs += [pl.BlockSpec((None, TILE, d), tok), modspec]
        args += [peer, modp]
    in_specs += [modspec, full2(n1g), full2(wm), full2(wg), full2(wgt), full2(wdvt), full2(gbc), full2(gbr)]
    args += [mod, n1g, wm, wg, wgt, wdvt, gbc, gbr]
    in_specs += [pl.BlockSpec((TILE, LANES), lambda bi, i: (i, 0))] * 3
    args += [cos, sa, sb]
    ng = wg.shape[1]
    out_shape = [jax.ShapeDtypeStruct((b, t, d), F32),
                 jax.ShapeDtypeStruct((b, t, cw), F32),
                 jax.ShapeDtypeStruct((b, t, 3 * mw), BF16),
                 jax.ShapeDtypeStruct((b, t, mw), F32),
                 jax.ShapeDtypeStruct((b, t, ng), F32),
                 jax.ShapeDtypeStruct((b, ng, t), F32),
                 jax.ShapeDtypeStruct((b, t, dw), BF16),
                 jax.ShapeDtypeStruct((b, t, dw), BF16),
                 jax.ShapeDtypeStruct((b, dw, t), BF16)]
    out_specs = [pl.BlockSpec((None, TILE, d), tok),
                 pl.BlockSpec((None, TILE, cw), tok),
                 pl.BlockSpec((None, TILE, 3 * mw), tok),
                 pl.BlockSpec((None, TILE, mw), tok),
                 pl.BlockSpec((None, TILE, ng), tok),
                 pl.BlockSpec((None, ng, TILE), lambda bi, i: (bi, 0, i)),
                 pl.BlockSpec((None, TILE, dw), tok),
                 pl.BlockSpec((None, TILE, dw), tok),
                 pl.BlockSpec((None, dw, TILE), lambda bi, i: (bi, 0, i))]
    return pl.pallas_call(
        functools.partial(_inproj_kernel, has_peer=has_peer, cw=cw, mw=mw),
        grid=(b, nt), in_specs=in_specs, out_specs=out_specs, out_shape=out_shape,
        compiler_params=_cparams(("parallel", "parallel")),
        name="in_proj",
    )(*args)


def _conv_kernel(up_ref, uc_ref, un_ref, w_ref, b_ref, lg_ref, lb_ref, o_ref, ext_ref):
    i = pl.program_id(1)
    nt = pl.num_programs(1)
    lm = jnp.where(i >= 2, 1.0, 0.0)
    rm = jnp.where(jnp.logical_and(i >= 1, i < nt - 1), 1.0, 0.0)
    hl = CONV_HALO
    ext_ref[0:hl, :] = up_ref[TILE - hl:TILE, :] * lm
    ext_ref[hl:hl + TILE, :] = uc_ref[...]
    ext_ref[hl + TILE:2 * hl + TILE, :] = un_ref[0:hl, :] * rm
    off = hl - CONV_K // 2
    acc = jnp.zeros(uc_ref.shape, F32)
    for k in range(CONV_K):
        acc = acc + w_ref[k:k + 1, :] * ext_ref[off + k:off + k + TILE, :]
    y = acc + b_ref[...]
    mu = jnp.mean(y, axis=-1, keepdims=True)
    yc = y - mu
    var = jnp.mean(yc * yc, axis=-1, keepdims=True)
    z = yc * lax.rsqrt(var + EPS) * lg_ref[...] + lb_ref[...]
    o_ref[...] = (z * jax.nn.sigmoid(z)).astype(o_ref.dtype)


def _conv_call(u, w, bias, lg, lb):
    b, t, cw = u.shape
    nt = t // TILE
    full2 = lambda arr: pl.BlockSpec(arr.shape, lambda bi, i: (0, 0))
    return pl.pallas_call(
        _conv_kernel,
        grid=(b, nt),
        in_specs=[pl.BlockSpec((None, TILE, cw), lambda bi, i: (bi, jnp.maximum(i - 1, 0), 0)),
                  pl.BlockSpec((None, TILE, cw), lambda bi, i: (bi, i, 0)),
                  pl.BlockSpec((None, TILE, cw), lambda bi, i: (bi, jnp.minimum(i + 1, nt - 1), 0)),
                  full2(w), full2(bias), full2(lg), full2(lb)],
        out_specs=pl.BlockSpec((None, TILE, cw), lambda bi, i: (bi, i, 0)),
        out_shape=jax.ShapeDtypeStruct((b, t, cw), BF16),
        scratch_shapes=[pltpu.VMEM((TILE + 2 * CONV_HALO, cw), F32)],
        compiler_params=_cparams(("parallel", "parallel")),
        name="conv_module",
    )(u, u, u, w, bias, lg, lb)


def _mlstm_kernel(qkv_ref, gc_ref, gr_ref, h_ref, c_ref, n_ref, m_ref):
    d = pl.program_id(0)
    j = pl.program_id(2)

    @pl.when(j == 0)
    def _():
        c_ref[...] = jnp.zeros_like(c_ref)
        n_ref[...] = jnp.zeros_like(n_ref)
        m_ref[...] = jnp.zeros_like(m_ref)

    tc = qkv_ref.shape[0]
    mw = qkv_ref.shape[1] // 3
    nh = MLSTM_HEADS
    fwd = d == 0
    row = lax.broadcasted_iota(jnp.int32, (tc, tc), 0)
    col = lax.broadcasted_iota(jnp.int32, (tc, tc), 1)
    tri = (row - col) * jnp.where(fwd, 1, -1) >= 0
    trif = tri.astype(F32)
    gc = gc_ref[...]
    gr = gr_ref[...]
    bcol = jnp.dot(trif, gc, precision=HIGHEST, preferred_element_type=F32)
    brow = lax.dot_general(gr, trif, NT_DIMS, precision=HIGHEST, preferred_element_type=F32)
    bl = jnp.sum(gr, axis=1, keepdims=True)

    q = qkv_ref[:, 0:mw]
    k = qkv_ref[:, mw:2 * mw]
    v = qkv_ref[:, 2 * mw:3 * mw]
    lane_head = lax.broadcasted_iota(jnp.int32, (tc, mw), 1) // MLSTM_DH
    cb = c_ref[...]
    inter_c = lax.dot_general(q, cb.astype(BF16), NT_DIMS, preferred_element_type=F32)
    qf = q.astype(F32)
    qn = qf * n_ref[...]

    out = jnp.zeros((tc, mw), F32)
    wkfull = jnp.zeros((tc, mw), F32)
    rhead = lax.broadcasted_iota(jnp.int32, (mw, 1), 0) // MLSTM_DH
    chead = lax.broadcasted_iota(jnp.int32, (1, mw), 1) // MLSTM_DH
    decay_col = jnp.zeros((mw, 1), F32)
    decay_row = jnp.zeros((1, mw), F32)
    for h in range(nh):
        mh = m_ref[h:h + 1, 0:1]
        bc = bcol[:, nh + h:nh + h + 1]
        br = brow[nh + h:nh + h + 1, :]
        ir = gr[h:h + 1, :]
        dlog = jnp.where(tri, bc - br + ir, -jnp.inf)
        inter = bc + mh
        mt = jnp.maximum(inter, jnp.max(dlog, axis=1, keepdims=True))
        dwt = jnp.exp(dlog - mt)
        iw = jnp.exp(inter - mt)
        hm = lane_head == h
        qh = jnp.where(hm, qf, 0.0).astype(BF16)
        s = lax.dot_general(qh, k, NT_DIMS, preferred_element_type=F32) * dwt
        sv = jnp.dot(s.astype(BF16), v, preferred_element_type=F32)
        qn_h = jnp.sum(jnp.where(hm, qn, 0.0), axis=1, keepdims=True)
        den = jnp.sum(s, axis=1, keepdims=True) + iw * qn_h
        denom = jnp.maximum(jnp.abs(den), jnp.exp(-mt))
        out = jnp.where(hm, (sv + iw * inter_c) / denom, out)
        blh = bl[nh + h:nh + h + 1, :]
        wlog_r = blh - br + ir
        mn = jnp.maximum(blh + mh, jnp.max(wlog_r, axis=1, keepdims=True))
        decay = jnp.exp(blh + mh - mn)
        wk_c = jnp.exp(blh - bc + gc[:, h:h + 1] - mn)
        wkfull = jnp.where(hm, wk_c, wkfull)
        decay_col = jnp.where(rhead == h, decay, decay_col)
        decay_row = jnp.where(chead == h, decay, decay_row)
        m_ref[h:h + 1, :] = jnp.broadcast_to(mn, (1, m_ref.shape[1]))
    h_ref[...] = out

    vw = v.astype(F32) * wkfull
    upd = jnp.dot(vw.T.astype(BF16), k, preferred_element_type=F32)
    c_ref[...] = decay_col * cb + jnp.where(rhead == chead, upd, 0.0)
    n_ref[...] = decay_row * n_ref[...] + jnp.sum(k.astype(F32) * wkfull, axis=0, keepdims=True)


def _mlstm_call(mqkv, gcol, grow):
    b, t, w3 = mqkv.shape
    mw = w3 // 3
    nt = t // TILE
    ng = gcol.shape[-1]

    def tile(d, j):
        return jnp.where(d == 0, j, jnp.where(j == 0, 0, nt - j))

    return pl.pallas_call(
        _mlstm_kernel,
        grid=(2, b, nt),
        in_specs=[pl.BlockSpec((None, TILE, w3), lambda d, bi, j: (bi, tile(d, j), 0)),
                  pl.BlockSpec((None, None, TILE, ng), lambda d, bi, j: (bi, d, tile(d, j), 0)),
                  pl.BlockSpec((None, None, ng, TILE), lambda d, bi, j: (bi, d, 0, tile(d, j)))],
        out_specs=pl.BlockSpec((None, None, TILE, mw), lambda d, bi, j: (d, bi, tile(d, j), 0)),
        out_shape=jax.ShapeDtypeStruct((2, b, t, mw), F32),
        scratch_shapes=[pltpu.VMEM((mw, mw), F32), pltpu.VMEM((1, mw), F32),
                        pltpu.VMEM((SUBLANES, LANES), F32)],
        compiler_params=_cparams(("arbitrary", "arbitrary", "arbitrary")),
        name="mlstm_scan",
    )(mqkv, gcol, grow)


def _attn_kernel(*refs, lam_init, kc, nq):
    dl_ref, ng_ref = refs[:2]
    q_refs = refs[2:2 + nq]
    k_ref, vt_ref, o_ref, s_ref, m_ref, l_ref, acc_ref = refs[2 + nq:]
    n_keys = k_ref.shape[0]
    q = jnp.concatenate([r[...] for r in q_refs], axis=0) if nq > 1 else q_refs[0][...]
    lane = lax.broadcasted_iota(jnp.int32, q.shape, 1)
    zero = jnp.zeros_like(q)
    qz = (jnp.where(lane < DIFF_DH, q, zero), jnp.where(lane >= DIFF_DH, q, zero))
    m_ref[...] = jnp.full(m_ref.shape, -jnp.inf, F32)
    l_ref[...] = jnp.zeros_like(l_ref)
    acc_ref[...] = jnp.zeros_like(acc_ref)

    def scores(start, slot):
        kb = k_ref[pl.ds(start, kc), :]
        for c in range(2):
            s_ref[slot, c] = lax.dot_general(kb, qz[c], NT_DIMS, preferred_element_type=F32)

    def softmax_pv(start, slot):
        vtb = vt_ref[:, pl.ds(start, kc)]
        for c in range(2):
            s = s_ref[slot, c]
            m_old = m_ref[c]
            m_new = jnp.maximum(m_old, jnp.max(s, axis=0, keepdims=True))
            alpha = jnp.exp2(m_old - m_new)
            p = jnp.exp2(s - m_new)
            l_ref[c] = alpha * l_ref[c] + jnp.sum(p, axis=0, keepdims=True)
            acc_ref[c] = alpha * acc_ref[c] + jnp.dot(vtb, p.astype(BF16), preferred_element_type=F32)
            m_ref[c] = m_new

    n = n_keys // kc
    pairs = (n - 1) // 2
    scores(0, 0)
    if pairs:
        def body(jj, carry):
            base = pl.multiple_of(jj * (2 * kc), TILE)
            scores(base + kc, 1)
            softmax_pv(base, 0)
            scores(base + 2 * kc, 0)
            softmax_pv(base + kc, 1)
            return carry

        lax.fori_loop(0, pairs, body, 0)
    done = 2 * pairs
    if n - done == 2:
        scores((done + 1) * kc, 1)
    softmax_pv(done * kc, 0)
    if n - done == 2:
        softmax_pv((done + 1) * kc, 1)

    dl = dl_ref[...]
    lam = (jnp.exp(jnp.sum(dl[0:1] * dl[1:2], axis=1, keepdims=True))
           - jnp.exp(jnp.sum(dl[2:3] * dl[3:4], axis=1, keepdims=True)) + lam_init)
    ot = acc_ref[0] / l_ref[0] - lam * (acc_ref[1] / l_ref[1])
    ms = jnp.mean(ot * ot, axis=0, keepdims=True)
    y = ot * lax.rsqrt(ms + EPS) * (ng_ref[...] * (1.0 - lam_init))
    o_ref[...] = y.T.astype(o_ref.dtype)


def _attn_key_chunk(t):
    return max(k for k in range(TILE, ATT_KC + 1, TILE) if t % k == 0)


def _attn_call(dq, dk, dvt, dlam, dng, lam_init, *, ctx):
    b, t, w = dq.shape
    nh = w // DIFF_DV
    if ctx:
        n_keys, nq, steps, q0 = TILE, 1, 1, 0
    else:
        n_keys, nq, q0 = t, ATT_TQ // TILE, 1
        steps = (t - TILE) // ATT_TQ
    tq = nq * TILE
    kc = _attn_key_chunk(n_keys)
    kern = functools.partial(_attn_kernel, lam_init=lam_init, kc=kc, nq=nq)
    q_specs = [pl.BlockSpec((None, TILE, DIFF_DV), functools.partial(
        lambda bi, h, i, j: (bi, q0 + nq * i + j, h), j=j)) for j in range(nq)]
    return pl.pallas_call(
        kern,
        grid=(b, nh, steps),
        in_specs=[pl.BlockSpec(dlam.shape, lambda bi, h, i: (0, 0)),
                  pl.BlockSpec(dng.shape, lambda bi, h, i: (0, 0))] + q_specs + [
                  pl.BlockSpec((None, n_keys, DIFF_DV), lambda bi, h, i: (bi, 0, h)),
                  pl.BlockSpec((None, DIFF_DV, n_keys), lambda bi, h, i: (bi, h, 0))],
        out_specs=pl.BlockSpec((None, tq, DIFF_DV), lambda bi, h, i: (bi, i, h)),
        out_shape=jax.ShapeDtypeStruct((b, steps * tq, w), BF16),
        scratch_shapes=[pltpu.VMEM((2, 2, kc, tq), F32),
                        pltpu.VMEM((2, 1, tq), F32), pltpu.VMEM((2, 1, tq), F32),
                        pltpu.VMEM((2, DIFF_DV, tq), F32)],
        compiler_params=_cparams(("parallel", "parallel", "arbitrary")),
        name="diff_attn_ctx" if ctx else "diff_attn",
    )(dlam, dng, *([dq] * nq), dk, dvt)


def _outproj_kernel(x_ref, conv_ref, hf_ref, hb_ref, mo_ref, dyc_ref, dyl_ref, mod_ref, mng_ref, n2g_ref,
                    wo_ref, xo_ref, h2_ref, h2t_ref):
    cw = conv_ref.shape[1]
    mw = hf_ref.shape[1]
    hh = hf_ref[...] + hb_ref[...]
    r = lax.broadcasted_iota(jnp.int32, (mw, mw), 0) // MLSTM_DH
    c = lax.broadcasted_iota(jnp.int32, (mw, mw), 1) // MLSTM_DH
    gm = jnp.where(r == c, 1.0 / MLSTM_DH, 0.0)
    ms = jnp.dot(hh * hh, gm, precision=HIGHEST, preferred_element_type=F32)
    ym = hh * lax.rsqrt(ms + EPS) * mng_ref[...] * jax.nn.sigmoid(mo_ref[...])
    dy = jnp.where(pl.program_id(1) == 0, dyc_ref[...], dyl_ref[...])
    o = (jnp.dot(conv_ref[...], wo_ref[0:cw, :], preferred_element_type=F32)
         + jnp.dot(ym.astype(BF16), wo_ref[cw:cw + mw, :], preferred_element_type=F32)
         + jnp.dot(dy, wo_ref[cw + mw:, :], preferred_element_type=F32))
    x = x_ref[...] + mod_ref[2:3, :] * o
    xo_ref[...] = x
    h2 = _rms(x) * n2g_ref[...] * (1.0 + mod_ref[4:5, :]) + mod_ref[3:4, :]
    h2_ref[...] = h2.astype(BF16)
    h2t_ref[...] = h2.T.astype(BF16)


def _outproj_call(x, conv, hdir, mo, dyc, dyl, mod, mng, n2g, wo):
    b, t, d = x.shape
    nt = t // TILE
    cw, mw, dw = conv.shape[-1], mo.shape[-1], dyl.shape[-1]
    tok = lambda bi, i: (bi, i, 0)
    full2 = lambda arr: pl.BlockSpec(arr.shape, lambda bi, i: (0, 0))
    return pl.pallas_call(
        _outproj_kernel,
        grid=(b, nt),
        in_specs=[pl.BlockSpec((None, TILE, d), tok),
                  pl.BlockSpec((None, TILE, cw), tok),
                  pl.BlockSpec((None, None, TILE, mw), lambda bi, i: (0, bi, i, 0)),
                  pl.BlockSpec((None, None, TILE, mw), lambda bi, i: (1, bi, i, 0)),
                  pl.BlockSpec((None, TILE, mw), tok),
                  pl.BlockSpec((None, TILE, dw), lambda bi, i: (bi, 0, 0)),
                  pl.BlockSpec((None, TILE, dw), lambda bi, i: (bi, jnp.maximum(i - 1, 0), 0)),
                  pl.BlockSpec((None, None, 6, d), lambda bi, i: (bi, jnp.minimum(i, 1), 0, 0)),
                  full2(mng), full2(n2g), full2(wo)],
        out_specs=[pl.BlockSpec((None, TILE, d), tok), pl.BlockSpec((None, TILE, d), tok),
                   pl.BlockSpec((d, TILE), lambda bi, i: (0, bi * nt + i))],
        out_shape=[jax.ShapeDtypeStruct((b, t, d), F32), jax.ShapeDtypeStruct((b, t, d), BF16),
                   jax.ShapeDtypeStruct((d, b * t), BF16)],
        compiler_params=_cparams(("parallel", "parallel")),
        name="out_proj",
    )(x, conv, hdir, hdir, mo, dyc, dyl, mod, mng, n2g, wo)


def _top_k_rows(s, k):
    idx = lax.broadcasted_iota(jnp.int32, s.shape, 0).astype(F32)
    vals, ids = [], []
    cur = s
    for _ in range(k):
        m = jnp.max(cur, axis=0, keepdims=True)
        am = jnp.min(jnp.where(cur == m, idx, float(s.shape[0])), axis=0, keepdims=True)
        vals.append(m)
        ids.append(am)
        cur = jnp.where(idx == am, -jnp.inf, cur)
    return vals, ids


def _select_exact(s0, s1):
    kk = PEER_TOPK
    tt = s0.shape[1]
    iota_k = lax.broadcasted_iota(jnp.int32, (kk, tt), 0).astype(F32)
    iota_n = lax.broadcasted_iota(jnp.int32, (N_KEYS, tt), 0).astype(F32)
    v0, i0 = _top_k_rows(s0, kk)
    v1, i1 = _top_k_rows(s1, kk)
    st0 = jnp.concatenate(v0, axis=0)
    n = jnp.zeros((kk, tt), F32)
    hv = st0 + v1[0]
    mx = v0[0] + v1[0]
    zsum = jnp.zeros((1, tt), F32)
    for _ in range(kk):
        m = jnp.max(hv, axis=0, keepdims=True)
        a_star = jnp.min(jnp.where(hv == m, iota_k, float(kk)), axis=0, keepdims=True)
        sel = iota_k == a_star
        zsum = zsum + jnp.exp(m - mx)
        n = n + jnp.where(sel, 1.0, 0.0)
        nxt = jnp.full((kk, tt), -jnp.inf, F32)
        for b in range(1, kk):
            nxt = jnp.where(n == float(b), v1[b], nxt)
        hv = jnp.where(sel, st0 + nxt, hv)
    ni = jnp.zeros((N_KEYS, tt), F32)
    r1 = jnp.full((N_KEYS, tt), float(kk), F32)
    for a in range(kk):
        ni = jnp.where(iota_n == i0[a], n[a:a + 1, :], ni)
        r1 = jnp.where(iota_n == i1[a], float(a), r1)
    return jnp.exp(s0 - v0[0]) / zsum, ni, r1, jnp.exp(s1 - v1[0])


def _top_k_ranks(s, k):
    cur = s
    rank = jnp.full(s.shape, float(k), F32)
    vals = []
    for j in range(k):
        m = jnp.max(cur, axis=0, keepdims=True)
        eq = cur == m
        vals.append(m)
        rank = jnp.where(eq, float(j), rank)
        cur = jnp.where(eq, -jnp.inf, cur)
    cnt = jnp.sum(jnp.where(rank < float(k), 1.0, 0.0), axis=0, keepdims=True)
    return vals, rank, cnt


def _select_fast(s0, s1):
    kk = PEER_TOPK
    tt = s0.shape[1]
    v0, rank0, cnt0 = _top_k_ranks(s0, kk)
    v1, rank1, cnt1 = _top_k_ranks(s1, kk)
    half = SUBLANES
    st1 = jnp.concatenate(v1, axis=0)
    st1_h = st1[0:half]
    riota = lax.broadcasted_iota(jnp.int32, (half, tt), 0)
    blocks = [v0[0] + st1]
    for a in range(1, half):
        blocks.append(jnp.where(riota < kk // (a + 1), v0[a] + st1_h, -jnp.inf))
    blocks.append(jnp.concatenate(v0[half:], axis=0) + v1[0])
    p = jnp.concatenate(blocks, axis=0)
    cur = p
    for _ in range(kk):
        m = jnp.max(cur, axis=0, keepdims=True)
        cur = jnp.where(cur == m, -jnp.inf, cur)
    picked = jnp.where(cur != p, 1.0, 0.0)
    mx = v0[0] + v1[0]
    zsum = jnp.sum(picked * jnp.exp(p - mx), axis=0, keepdims=True)
    cntm = jnp.sum(picked, axis=0, keepdims=True)
    n_rows = [jnp.sum(picked[0:kk], axis=0, keepdims=True)]
    for a in range(1, half):
        lo = kk + (a - 1) * half
        n_rows.append(jnp.sum(picked[lo:lo + half], axis=0, keepdims=True))
    last = kk + (half - 1) * half
    n_rows += [picked[last + r:last + r + 1] for r in range(kk - half)]
    ni = jnp.zeros((N_KEYS, tt), F32)
    for a in range(kk):
        ni = jnp.where(rank0 == float(a), n_rows[a], ni)
    want = float(kk)
    bad = jnp.where((cnt0 != want) | (cnt1 != want) | (cntm != want), 1.0, 0.0)
    return jnp.exp(s0 - v0[0]) / zsum, ni, rank1, jnp.exp(s1 - v1[0]), bad


def _peer_sel_kernel(h2_ref, wq_ref, kh_ref, a_ref, ni_ref, r1_ref, bv_ref):
    q = jnp.dot(h2_ref[...], wq_ref[...], preferred_element_type=F32).astype(BF16)
    tt = q.shape[0]
    hw = q.shape[1] // PEER_HEADS

    def head_scores(h):
        st_ = lax.dot_general(kh_ref[h], q[:, h * hw:(h + 1) * hw], NT_DIMS,
                              preferred_element_type=F32)
        return st_[0:N_KEYS], st_[N_KEYS:2 * N_KEYS]

    def store(h, a, ni, r1, bv):
        a_ref[h] = a
        ni_ref[h] = ni
        r1_ref[h] = r1.astype(BF16)
        bv_ref[h] = bv.astype(BF16)

    bad = jnp.zeros((1, tt), F32)
    for h in range(PEER_HEADS):
        a, ni, r1, bv, b = _select_fast(*head_scores(h))
        store(h, a, ni, r1, bv)
        bad = jnp.maximum(bad, b)

    @pl.when(jnp.max(bad) > 0.0)
    def _():
        for h in range(PEER_HEADS):
            store(h, *_select_exact(*head_scores(h)))


def _peer_sel_call(h2, wq, kh):
    ntok, d = h2.shape
    nt = ntok // TILE
    shp = jax.ShapeDtypeStruct((PEER_HEADS, N_KEYS, ntok), F32)
    shp_b = jax.ShapeDtypeStruct((PEER_HEADS, N_KEYS, ntok), BF16)
    ospec = pl.BlockSpec((PEER_HEADS, N_KEYS, TILE), lambda i: (0, 0, i))
    return pl.pallas_call(
        _peer_sel_kernel,
        grid=(nt,),
        in_specs=[pl.BlockSpec((TILE, d), lambda i: (i, 0)),
                  pl.BlockSpec(wq.shape, lambda i: (0, 0)),
                  pl.BlockSpec(kh.shape, lambda i: (0, 0, 0))],
        out_specs=[ospec] * 4,
        out_shape=[shp, shp, shp_b, shp_b],
        compiler_params=_cparams(("parallel",)),
        name="peer_select",
    )(h2, wq, kh)


def _gelu(x):
    return 0.5 * x * (1.0 + lax.erf(x * (2.0 ** -0.5)))


def _peer_dense_kernel(h2t_ref, a_ref, ni_ref, r1_ref, bv_ref, u_ref, vt_ref, o_ref, acc_ref):
    c = pl.program_id(1)

    @pl.when(c == 0)
    def _():
        acc_ref[...] = jnp.zeros_like(acc_ref)

    tm = h2t_ref.shape[1]
    nib = u_ref.shape[0] // N_KEYS
    rep = N_KEYS // BF16_SUBLANES

    def rows(ref, h, ib):
        r = jnp.broadcast_to(ref[h, ib:ib + 1, :], (BF16_SUBLANES, tm)).astype(BF16)
        return jnp.concatenate([r] * rep, axis=0)

    st_ = jnp.dot(u_ref[...], h2t_ref[...], preferred_element_type=F32)
    blocks = []
    for ib in range(nib):
        g = None
        for h in range(PEER_HEADS):
            bvh = bv_ref[h]
            keep = r1_ref[h] < rows(ni_ref, h, ib)
            term = rows(a_ref, h, ib) * jnp.where(keep, bvh, jnp.zeros_like(bvh))
            g = term if g is None else g + term
        blocks.append(g * _gelu(st_[ib * N_KEYS:(ib + 1) * N_KEYS]).astype(BF16))
    wt = jnp.concatenate(blocks, axis=0)
    acc_ref[...] += jnp.dot(vt_ref[...], wt, preferred_element_type=F32)

    @pl.when(c == pl.num_programs(1) - 1)
    def _():
        o_ref[...] = acc_ref[...].T


def _peer_dense_call(h2t, a, ni, r1, bv, u, vt):
    d, ntok = h2t.shape
    ne = u.shape[0]
    tm, ec = PEER_TM, PEER_EC
    ib = ec // N_KEYS
    rowspec = pl.BlockSpec((PEER_HEADS, ib, tm), lambda t, c: (0, c, t))
    colspec = pl.BlockSpec((PEER_HEADS, N_KEYS, tm), lambda t, c: (0, 0, t))
    return pl.pallas_call(
        _peer_dense_kernel,
        grid=(ntok // tm, ne // ec),
        in_specs=[pl.BlockSpec((d, tm), lambda t, c: (0, t)),
                  rowspec, rowspec, colspec, colspec,
                  pl.BlockSpec((ec, d), lambda t, c: (c, 0)),
                  pl.BlockSpec((d, ec), lambda t, c: (0, c))],
        out_specs=pl.BlockSpec((tm, d), lambda t, c: (t, 0)),
        out_shape=jax.ShapeDtypeStruct((ntok, d), F32),
        scratch_shapes=[pltpu.VMEM((d, tm), F32)],
        compiler_params=_cparams(("parallel", "arbitrary")),
        name="peer_dense",
    )(h2t, a, ni, r1, bv, u, vt)


def _final_kernel(x_ref, peer_ref, mod_ref, g_ref, o_ref):
    x = x_ref[...] + mod_ref[5:6, :] * peer_ref[...]
    o_ref[...] = _rms(x) * g_ref[...]


def _final_call(x, peer, mod, fg, n_ctx_tiles):
    b, t, d = x.shape
    nl = t // TILE - n_ctx_tiles
    tok = lambda bi, i: (bi, i + n_ctx_tiles, 0)
    return pl.pallas_call(
        _final_kernel,
        grid=(b, nl),
        in_specs=[pl.BlockSpec((None, TILE, d), tok), pl.BlockSpec((None, TILE, d), tok),
                  pl.BlockSpec((None, None, 6, d), lambda bi, i: (bi, 1, 0, 0)),
                  pl.BlockSpec(fg.shape, lambda bi, i: (0, 0))],
        out_specs=pl.BlockSpec((None, TILE, d), lambda bi, i: (bi, i, 0)),
        out_shape=jax.ShapeDtypeStruct((b, nl * TILE, d), F32),
        compiler_params=_cparams(("parallel", "parallel")),
        name="final_norm",
    )(x, peer, mod, fg)


def _rope_tables(seq, n_ctx):
    rows = seq // GRID_W
    axis_rot = DIFF_DH // 2
    row = jnp.repeat(jnp.arange(rows), GRID_W).astype(F32)
    col = jnp.tile(jnp.arange(GRID_W), rows).astype(F32)
    inv = ROPE_BASE ** (-jnp.arange(0, axis_rot, 2, dtype=F32) / axis_rot)
    ang = jnp.concatenate([row[:, None] * inv, col[:, None] * inv], axis=-1)
    cos = jnp.repeat(jnp.cos(ang), 2, axis=-1)
    sin = jnp.repeat(jnp.sin(ang), 2, axis=-1)
    even = (jnp.arange(DIFF_DH) % 2 == 0)[None, :]
    sa = jnp.where(even, -sin, 0.0)
    sb = jnp.where(even, 0.0, sin)
    rep = LANES // DIFF_DH

    def full(tab, ctx_val):
        tab = jnp.tile(tab, (1, rep))
        return jnp.concatenate([jnp.full((n_ctx, LANES), ctx_val, F32), tab], axis=0)

    return full(cos, 1.0), full(sa, 0.0), full(sb, 0.0)


def kernel(x, c, ctx, c_ctx, ada_w, ada_b, norm1_g, norm2_g, w_in, conv_w, conv_b, conv_ln_g, conv_ln_b, mlstm_gate_b, mlstm_norm_g, diff_lambda, diff_norm_g, w_out, peer_wq, peer_keys, peer_u, peer_v, final_g):
    b, seq, d = x.shape
    n_ctx = ctx.shape[1]
    depth = ada_w.shape[0]
    cw, mw, dw = d // 4, d // 4, d // 2
    ng = 4 * MLSTM_HEADS
    assert n_ctx == TILE and seq % ATT_TQ == 0 and seq % GRID_W == 0
    assert w_in.shape[-1] == 2 * cw + 4 * mw + ng + 3 * dw
    assert peer_keys.shape[1:] == (PEER_HEADS, 2, N_KEYS, d // PEER_HEADS // 2)
    assert conv_w.shape[1] == CONV_K and (b * (seq + n_ctx)) % PEER_TM == 0

    rows = -(-(b + 1) // SUBLANES) * SUBLANES
    cvec = jnp.zeros((rows, d), F32).at[:b].set(c).at[b].set(c_ctx)
    mods = _ada_call(cvec, ada_w, ada_b)
    mod_lat = mods[:, :b].reshape(depth, b, 1, 6, d)
    mod_ctx = jnp.broadcast_to(mods[:, b].reshape(depth, 1, 1, 6, d), (depth, b, 1, 6, d))
    mod_all = jnp.concatenate([mod_ctx, mod_lat], axis=2)

    cos, sa, sb = _rope_tables(seq, n_ctx)
    g0 = 2 * cw + 4 * mw
    w_main = jnp.concatenate([w_in[:, :, :g0], w_in[:, :, g0 + ng:g0 + ng + 2 * dw]], axis=-1).astype(BF16)
    w_dvt = jnp.swapaxes(w_in[:, :, g0 + ng + 2 * dw:], 1, 2).astype(BF16)
    w_g = w_in[:, :, g0:g0 + ng]
    w_gt = jnp.swapaxes(w_g, 1, 2)
    gbias = mlstm_gate_b.reshape(depth, ng)
    w_out_b = w_out.astype(BF16)
    wq_b = peer_wq.astype(BF16)
    hw = d // PEER_HEADS
    kz = jnp.zeros((depth, PEER_HEADS, N_KEYS, hw // 2), F32)
    kh = jnp.concatenate([jnp.concatenate([peer_keys[:, :, 0], kz], axis=-1),
                          jnp.concatenate([kz, peer_keys[:, :, 1]], axis=-1)], axis=2).astype(BF16)
    u_b = peer_u.astype(BF16)
    vt_b = jnp.swapaxes(peer_v, 1, 2).astype(BF16)
    mng = jnp.tile(mlstm_norm_g, (1, MLSTM_HEADS))

    xs = jnp.concatenate([ctx, x], axis=1)
    t = xs.shape[1]
    peer = None
    for l in range(depth):
        lam_init = 0.8 - 0.6 * math.exp(-0.3 * l)
        xs, u, mqkv, mo, gcol, grow, dq, dk, dvt = _inproj_call(
            xs, peer, mod_all[l - 1] if l else None, mod_all[l], norm1_g[l][None], w_main[l],
            w_g[l], w_gt[l], w_dvt[l], gbias[l][None], gbias[l][:, None], cos, sa, sb)
        conv = _conv_call(u, conv_w[l, :, 0, :], conv_b[l][None], conv_ln_g[l][None], conv_ln_b[l][None])
        gcol_d = gcol.reshape(b, t, 2, ng // 2).transpose(0, 2, 1, 3)
        grow_d = grow.reshape(b, 2, ng // 2, t)
        hdir = _mlstm_call(mqkv, gcol_d, grow_d)
        dyc = _attn_call(dq, dk, dvt, diff_lambda[l], diff_norm_g[l][:, None], lam_init, ctx=True)
        dyl = _attn_call(dq, dk, dvt, diff_lambda[l], diff_norm_g[l][:, None], lam_init, ctx=False)
        xs, h2, h2t = _outproj_call(xs, conv, hdir, mo, dyc, dyl, mod_all[l], mng[l][None], norm2_g[l][None],
                                    w_out_b[l])
        h2f = h2.reshape(b * t, d)
        a, ni, r1, bv = _peer_sel_call(h2f, wq_b[l], kh[l])
        peer = _peer_dense_call(h2t, a, ni, r1, bv, u_b[l], vt_b[l]).reshape(b, t, d)
    return _final_call(xs, peer, mod_all[depth - 1], final_g[None], n_ctx // TILE)
```

```python
import functools
import math

import jax
import jax.numpy as jnp
from jax import lax
from jax.experimental import pallas as pl
from jax.experimental.pallas import tpu as pltpu

F32 = jnp.float32
BF16 = jnp.bfloat16
HIGHEST = lax.Precision.HIGHEST

GRID_W = 64
EPS = 1e-6
CONV_K = 31
MLSTM_DH = 64
MLSTM_HEADS = 4
DIFF_DH = 64
DIFF_DV = 128
DIFF_HEADS = 4
ROPE_BASE = 10000.0
PEER_HEADS = 8
N_KEYS = 128
PEER_TOPK = 16

LANES = 128
SUBLANES = 8
TILE = 256
CONV_HALO = 16
ATT_KC = 1024
ATT_TQ = 512
PEER_TM = 1024
BF16_SUBLANES = 16
PEER_EC = 1024
VMEM_LIMIT = 56 * 1024 * 1024

NT_DIMS = (((1,), (1,)), ((), ()))


def _cparams(sem):
    return pltpu.CompilerParams(dimension_semantics=sem, vmem_limit_bytes=VMEM_LIMIT)


def _rms(x, eps=EPS):
    return x * lax.rsqrt(jnp.mean(x * x, axis=-1, keepdims=True) + eps)


def _log_sigmoid(x):
    return jnp.minimum(x, 0.0) - jnp.log(1.0 + jnp.exp(-jnp.abs(x)))


def _ada_kernel(c_ref, w_ref, b_ref, o_ref):
    c = c_ref[...]
    s = (c * jax.nn.sigmoid(c)).astype(BF16)
    o_ref[...] = jnp.dot(s, w_ref[...].astype(BF16), preferred_element_type=F32) + b_ref[...]


def _ada_call(cvec, ada_w, ada_b):
    depth, d, n = ada_w.shape
    tn = 1536
    rows = cvec.shape[0]
    return pl.pallas_call(
        _ada_kernel,
        grid=(depth, n // tn),
        in_specs=[pl.BlockSpec((rows, d), lambda l, j: (0, 0)),
                  pl.BlockSpec((None, d, tn), lambda l, j: (l, 0, j)),
                  pl.BlockSpec((None, 1, tn), lambda l, j: (l, 0, j))],
        out_specs=pl.BlockSpec((None, rows, tn), lambda l, j: (l, 0, j)),
        out_shape=jax.ShapeDtypeStruct((depth, rows, n), F32),
        compiler_params=_cparams(("parallel", "parallel")),
        name="ada_mod",
    )(cvec, ada_w, ada_b.reshape(depth, 1, n))


def _rope(t, c, sa, sb):
    w = t.shape[1]
    rep = w // LANES
    c, sa, sb = (jnp.concatenate([z] * rep, axis=1) for z in (c, sa, sb))
    return t * c + pltpu.roll(t, w - 1, 1) * sa + pltpu.roll(t, 1, 1) * sb


def _inproj_kernel(*refs, has_peer, cw, mw):
    if has_peer:
        x_ref, peer_ref, modp_ref = refs[:3]
        refs = refs[3:]
    (mod_ref, n1g_ref, wm_ref, wdvt_ref, gb_ref, cos_ref, sa_ref, sb_ref,
     xo_ref, u_ref, mqkv_ref, mo_ref, gcol_ref, grow_ref, dq_ref, dk_ref, dvt_ref) = refs[-17:]
    if not has_peer:
        x_ref = refs[0]
    x = x_ref[...]
    if has_peer:
        x = x + modp_ref[5:6, :] * peer_ref[...]
        xo_ref[...] = x
    else:
        xo_ref[...] = x
    h = _rms(x) * n1g_ref[...] * (1.0 + mod_ref[1:2, :]) + mod_ref[0:1, :]
    hb = h.astype(BF16)

    def proj(lo, hi):
        return jnp.dot(hb, wm_ref[:, lo:hi], preferred_element_type=F32)

    a = proj(0, 2 * cw)
    u_ref[...] = a[:, :cw] * jax.nn.sigmoid(a[:, cw:])
    o = 2 * cw
    mqkv_ref[:, 0:mw] = proj(o, o + mw).astype(BF16)
    mqkv_ref[:, mw:2 * mw] = (proj(o + mw, o + 2 * mw) * (MLSTM_DH ** -0.5)).astype(BF16)
    mqkv_ref[:, 2 * mw:3 * mw] = proj(o + 2 * mw, o + 3 * mw).astype(BF16)
    mo_ref[...] = proj(o + 3 * mw, o + 4 * mw)
    o = o + 4 * mw
    dw = dq_ref.shape[1]
    c, sa, sb = cos_ref[...], sa_ref[...], sb_ref[...]
    dq_ref[...] = (_rope(proj(o, o + dw), c, sa, sb) * (DIFF_DH ** -0.5 * math.log2(math.e))).astype(BF16)
    dk_ref[...] = _rope(proj(o + dw, o + 2 * dw), c, sa, sb).astype(BF16)
    dvt_ref[...] = lax.dot_general(wdvt_ref[...], hb, NT_DIMS, preferred_element_type=F32).astype(BF16)

    o = o + 2 * dw
    ng = gcol_ref.shape[1]
    g = proj(o, o + LANES) + gb_ref[...]
    cidx = lax.broadcasted_iota(jnp.int32, g.shape, 1)
    g = jnp.where((cidx // MLSTM_HEADS) % 2 == 1, _log_sigmoid(g), g)
    gcol_ref[...] = g[:, :ng]
    grow_ref[...] = g.T[:ng, :]


def _inproj_call(x, peer, modp, mod, n1g, wm, wdvt, gb, ng, cos, sa, sb):
    b, t, d = x.shape
    nt = t // TILE
    cw = d // 4
    mw = d // 4
    dw = d // 2
    has_peer = peer is not None
    tok = lambda bi, i: (bi, i, 0)
    modspec = pl.BlockSpec((None, None, 6, d), lambda bi, i: (bi, jnp.minimum(i, 1), 0, 0))
    full2 = lambda arr: pl.BlockSpec(arr.shape, lambda bi, i: (0, 0))
    in_specs = [pl.BlockSpec((None, TILE, d), tok)]
    args = [x]
    if has_peer:
        in_specs += [pl.BlockSpec((None, TILE, d), tok), modspec]
        args += [peer, modp]
    in_specs += [modspec, full2(n1g), full2(wm), full2(wdvt), full2(gb)]
    args += [mod, n1g, wm, wdvt, gb]
    in_specs += [pl.BlockSpec((TILE, LANES), lambda bi, i: (i, 0))] * 3
    args += [cos, sa, sb]
    out_shape = [jax.ShapeDtypeStruct((b, t, d), F32),
                 jax.ShapeDtypeStruct((b, t, cw), F32),
                 jax.ShapeDtypeStruct((b, t, 3 * mw), BF16),
                 jax.ShapeDtypeStruct((b, t, mw), F32),
                 jax.ShapeDtypeStruct((b, t, ng), F32),
                 jax.ShapeDtypeStruct((b, ng, t), F32),
                 jax.ShapeDtypeStruct((b, t, dw), BF16),
                 jax.ShapeDtypeStruct((b, t, dw), BF16),
                 jax.ShapeDtypeStruct((b, dw, t), BF16)]
    out_specs = [pl.BlockSpec((None, TILE, d), tok),
                 pl.BlockSpec((None, TILE, cw), tok),
                 pl.BlockSpec((None, TILE, 3 * mw), tok),
                 pl.BlockSpec((None, TILE, mw), tok),
                 pl.BlockSpec((None, TILE, ng), tok),
                 pl.BlockSpec((None, ng, TILE), lambda bi, i: (bi, 0, i)),
                 pl.BlockSpec((None, TILE, dw), tok),
                 pl.BlockSpec((None, TILE, dw), tok),
                 pl.BlockSpec((None, dw, TILE), lambda bi, i: (bi, 0, i))]
    return pl.pallas_call(
        functools.partial(_inproj_kernel, has_peer=has_peer, cw=cw, mw=mw),
        grid=(b, nt), in_specs=in_specs, out_specs=out_specs, out_shape=out_shape,
        compiler_params=_cparams(("parallel", "parallel")),
        name="in_proj",
    )(*args)


def _conv_kernel(up_ref, uc_ref, un_ref, w_ref, b_ref, lg_ref, lb_ref, o_ref, ext_ref):
    i = pl.program_id(1)
    nt = pl.num_programs(1)
    lm = jnp.where(i >= 2, 1.0, 0.0)
    rm = jnp.where(jnp.logical_and(i >= 1, i < nt - 1), 1.0, 0.0)
    hl = CONV_HALO
    ext_ref[0:hl, :] = up_ref[TILE - hl:TILE, :] * lm
    ext_ref[hl:hl + TILE, :] = uc_ref[...]
    ext_ref[hl + TILE:2 * hl + TILE, :] = un_ref[0:hl, :] * rm
    off = hl - CONV_K // 2
    acc = jnp.zeros(uc_ref.shape, F32)
    for k in range(CONV_K):
        acc = acc + w_ref[k:k + 1, :] * ext_ref[off + k:off + k + TILE, :]
    y = acc + b_ref[...]
    mu = jnp.mean(y, axis=-1, keepdims=True)
    yc = y - mu
    var = jnp.mean(yc * yc, axis=-1, keepdims=True)
    z = yc * lax.rsqrt(var + EPS) * lg_ref[...] + lb_ref[...]
    o_ref[...] = (z * jax.nn.sigmoid(z)).astype(o_ref.dtype)


def _conv_call(u, w, bias, lg, lb):
    b, t, cw = u.shape
    nt = t // TILE
    full2 = lambda arr: pl.BlockSpec(arr.shape, lambda bi, i: (0, 0))
    return pl.pallas_call(
        _conv_kernel,
        grid=(b, nt),
        in_specs=[pl.BlockSpec((None, TILE, cw), lambda bi, i: (bi, jnp.maximum(i - 1, 0), 0)),
                  pl.BlockSpec((None, TILE, cw), lambda bi, i: (bi, i, 0)),
                  pl.BlockSpec((None, TILE, cw), lambda bi, i: (bi, jnp.minimum(i + 1, nt - 1), 0)),
                  full2(w), full2(bias), full2(lg), full2(lb)],
        out_specs=pl.BlockSpec((None, TILE, cw), lambda bi, i: (bi, i, 0)),
        out_shape=jax.ShapeDtypeStruct((b, t, cw), BF16),
        scratch_shapes=[pltpu.VMEM((TILE + 2 * CONV_HALO, cw), F32)],
        compiler_params=_cparams(("parallel", "parallel")),
        name="conv_module",
    )(u, u, u, w, bias, lg, lb)


def _mlstm_chunk(qkv_ref, gc_ref, gr_ref, h_ref, c_ref, n_ref, m_ref, fwd):
    tc = qkv_ref.shape[0]
    mw = qkv_ref.shape[1] // 3
    nh = MLSTM_HEADS
    row = lax.broadcasted_iota(jnp.int32, (tc, tc), 0)
    col = lax.broadcasted_iota(jnp.int32, (tc, tc), 1)
    tri = row >= col if fwd else row <= col
    trif = tri.astype(F32)
    gc = gc_ref[...]
    gr = gr_ref[...]
    bcol = jnp.dot(trif, gc, precision=HIGHEST, preferred_element_type=F32)
    brow = lax.dot_general(gr, trif, NT_DIMS, precision=HIGHEST, preferred_element_type=F32)
    bl = jnp.sum(gr, axis=1, keepdims=True)

    q = qkv_ref[:, 0:mw]
    k = qkv_ref[:, mw:2 * mw]
    v = qkv_ref[:, 2 * mw:3 * mw]
    lane_head = lax.broadcasted_iota(jnp.int32, (tc, mw), 1) // MLSTM_DH
    cb = c_ref[...]
    inter_c = lax.dot_general(q, cb.astype(BF16), NT_DIMS, preferred_element_type=F32)
    qf = q.astype(F32)
    qn = qf * n_ref[...]

    out = jnp.zeros((tc, mw), F32)
    wkfull = jnp.zeros((tc, mw), F32)
    rhead = lax.broadcasted_iota(jnp.int32, (mw, 1), 0) // MLSTM_DH
    chead = lax.broadcasted_iota(jnp.int32, (1, mw), 1) // MLSTM_DH
    decay_col = jnp.zeros((mw, 1), F32)
    decay_row = jnp.zeros((1, mw), F32)
    for h in range(nh):
        mh = m_ref[h:h + 1, 0:1]
        bc = bcol[:, nh + h:nh + h + 1]
        br = brow[nh + h:nh + h + 1, :]
        ir = gr[h:h + 1, :]
        dlog = jnp.where(tri, bc - br + ir, -jnp.inf)
        inter = bc + mh
        mt = jnp.maximum(inter, jnp.max(dlog, axis=1, keepdims=True))
        dwt = jnp.exp(dlog - mt)
        iw = jnp.exp(inter - mt)
        hm = lane_head == h
        qh = jnp.where(hm, qf, 0.0).astype(BF16)
        s = lax.dot_general(qh, k, NT_DIMS, preferred_element_type=F32) * dwt
        sv = jnp.dot(s.astype(BF16), v, preferred_element_type=F32)
        qn_h = jnp.sum(jnp.where(hm, qn, 0.0), axis=1, keepdims=True)
        den = jnp.sum(s, axis=1, keepdims=True) + iw * qn_h
        denom = jnp.maximum(jnp.abs(den), jnp.exp(-mt))
        out = jnp.where(hm, (sv + iw * inter_c) / denom, out)
        blh = bl[nh + h:nh + h + 1, :]
        wlog_r = blh - br + ir
        mn = jnp.maximum(blh + mh, jnp.max(wlog_r, axis=1, keepdims=True))
        decay = jnp.exp(blh + mh - mn)
        wk_c = jnp.exp(blh - bc + gc[:, h:h + 1] - mn)
        wkfull = jnp.where(hm, wk_c, wkfull)
        decay_col = jnp.where(rhead == h, decay, decay_col)
        decay_row = jnp.where(chead == h, decay, decay_row)
        m_ref[h:h + 1, :] = jnp.broadcast_to(mn, (1, m_ref.shape[1]))
    h_ref[...] = out

    vw = v.astype(F32) * wkfull
    upd = jnp.dot(vw.T.astype(BF16), k, preferred_element_type=F32)
    c_ref[...] = decay_col * cb + jnp.where(rhead == chead, upd, 0.0)
    n_ref[...] = decay_row * n_ref[...] + jnp.sum(k.astype(F32) * wkfull, axis=0, keepdims=True)


def _mlstm_kernel(qkvf_ref, gcf_ref, grf_ref, qkvb_ref, gcb_ref, grb_ref, hf_ref, hb_ref,
                  c_ref, n_ref, m_ref):
    @pl.when(pl.program_id(1) == 0)
    def _():
        c_ref[...] = jnp.zeros_like(c_ref)
        n_ref[...] = jnp.zeros_like(n_ref)
        m_ref[...] = jnp.zeros_like(m_ref)

    _mlstm_chunk(qkvf_ref, gcf_ref, grf_ref, hf_ref, c_ref.at[0], n_ref.at[0], m_ref.at[0], True)
    _mlstm_chunk(qkvb_ref, gcb_ref, grb_ref, hb_ref, c_ref.at[1], n_ref.at[1], m_ref.at[1], False)


def _mlstm_call(mqkv, gcol, grow):
    b, t, w3 = mqkv.shape
    mw = w3 // 3
    nt = t // TILE
    ng = gcol.shape[-1]

    def rev(j):
        return jnp.where(j == 0, 0, nt - j)

    hshape = jax.ShapeDtypeStruct((b, t, mw), F32)
    return pl.pallas_call(
        _mlstm_kernel,
        grid=(b, nt),
        in_specs=[pl.BlockSpec((None, TILE, w3), lambda bi, j: (bi, j, 0)),
                  pl.BlockSpec((None, None, TILE, ng), lambda bi, j: (bi, 0, j, 0)),
                  pl.BlockSpec((None, None, ng, TILE), lambda bi, j: (bi, 0, 0, j)),
                  pl.BlockSpec((None, TILE, w3), lambda bi, j: (bi, rev(j), 0)),
                  pl.BlockSpec((None, None, TILE, ng), lambda bi, j: (bi, 1, rev(j), 0)),
                  pl.BlockSpec((None, None, ng, TILE), lambda bi, j: (bi, 1, 0, rev(j)))],
        out_specs=[pl.BlockSpec((None, TILE, mw), lambda bi, j: (bi, j, 0)),
                   pl.BlockSpec((None, TILE, mw), lambda bi, j: (bi, rev(j), 0))],
        out_shape=[hshape, hshape],
        scratch_shapes=[pltpu.VMEM((2, mw, mw), F32), pltpu.VMEM((2, 1, mw), F32),
                        pltpu.VMEM((2, SUBLANES, LANES), F32)],
        compiler_params=_cparams(("arbitrary", "arbitrary")),
        name="mlstm_scan",
    )(mqkv, gcol, grow, mqkv, gcol, grow)


def _attn_kernel(*refs, lam_init, kc, nq):
    dl_ref, ng_ref = refs[:2]
    q_refs = refs[2:2 + nq]
    k_ref, vt_ref, o_ref, s_ref, m_ref, l_ref, acc_ref = refs[2 + nq:]
    n_keys = k_ref.shape[0]
    q = jnp.concatenate([r[...] for r in q_refs], axis=0) if nq > 1 else q_refs[0][...]
    lane = lax.broadcasted_iota(jnp.int32, q.shape, 1)
    zero = jnp.zeros_like(q)
    qz = (jnp.where(lane < DIFF_DH, q, zero), jnp.where(lane >= DIFF_DH, q, zero))
    m_ref[...] = jnp.full(m_ref.shape, -jnp.inf, F32)
    l_ref[...] = jnp.zeros_like(l_ref)
    acc_ref[...] = jnp.zeros_like(acc_ref)

    def scores(start, slot):
        kb = k_ref[pl.ds(start, kc), :]
        for c in range(2):
            s_ref[slot, c] = lax.dot_general(kb, qz[c], NT_DIMS, preferred_element_type=F32)

    def softmax_pv(start, slot):
        vtb = vt_ref[:, pl.ds(start, kc)]
        for c in range(2):
            s = s_ref[slot, c]
            m_old = m_ref[c]
            m_new = jnp.maximum(m_old, jnp.max(s, axis=0, keepdims=True))
            alpha = jnp.exp2(m_old - m_new)
            p = jnp.exp2(s - m_new)
            l_ref[c] = alpha * l_ref[c] + jnp.sum(p, axis=0, keepdims=True)
            acc_ref[c] = alpha * acc_ref[c] + jnp.dot(vtb, p.astype(BF16), preferred_element_type=F32)
            m_ref[c] = m_new

    n = n_keys // kc
    pairs = (n - 1) // 2
    scores(0, 0)
    if pairs:
        def body(jj, carry):
            base = pl.multiple_of(jj * (2 * kc), TILE)
            scores(base + kc, 1)
            softmax_pv(base, 0)
            scores(base + 2 * kc, 0)
            softmax_pv(base + kc, 1)
            return carry

        lax.fori_loop(0, pairs, body, 0)
    done = 2 * pairs
    if n - done == 2:
        scores((done + 1) * kc, 1)
    softmax_pv(done * kc, 0)
    if n - done == 2:
        softmax_pv((done + 1) * kc, 1)

    dl = dl_ref[...]
    lam = (jnp.exp(jnp.sum(dl[0:1] * dl[1:2], axis=1, keepdims=True))
           - jnp.exp(jnp.sum(dl[2:3] * dl[3:4], axis=1, keepdims=True)) + lam_init)
    ot = acc_ref[0] / l_ref[0] - lam * (acc_ref[1] / l_ref[1])
    ms = jnp.mean(ot * ot, axis=0, keepdims=True)
    y = ot * lax.rsqrt(ms + EPS) * (ng_ref[...] * (1.0 - lam_init))
    o_ref[...] = y.T.astype(o_ref.dtype)


def _attn_key_chunk(t):
    return max(k for k in range(TILE, ATT_KC + 1, TILE) if t % k == 0)


def _attn_call(dq, dk, dvt, dlam, dng, lam_init, *, ctx):
    b, t, w = dq.shape
    nh = w // DIFF_DV
    if ctx:
        n_keys, nq, steps, q0 = TILE, 1, 1, 0
    else:
        n_keys, nq, q0 = t, ATT_TQ // TILE, 1
        steps = (t - TILE) // ATT_TQ
    tq = nq * TILE
    kc = _attn_key_chunk(n_keys)
    kern = functools.partial(_attn_kernel, lam_init=lam_init, kc=kc, nq=nq)
    q_specs = [pl.BlockSpec((None, TILE, DIFF_DV), functools.partial(
        lambda bi, h, i, j: (bi, q0 + nq * i + j, h), j=j)) for j in range(nq)]
    return pl.pallas_call(
        kern,
        grid=(b, nh, steps),
        in_specs=[pl.BlockSpec(dlam.shape, lambda bi, h, i: (0, 0)),
                  pl.BlockSpec(dng.shape, lambda bi, h, i: (0, 0))] + q_specs + [
                  pl.BlockSpec((None, n_keys, DIFF_DV), lambda bi, h, i: (bi, 0, h)),
                  pl.BlockSpec((None, DIFF_DV, n_keys), lambda bi, h, i: (bi, h, 0))],
        out_specs=pl.BlockSpec((None, tq, DIFF_DV), lambda bi, h, i: (bi, i, h)),
        out_shape=jax.ShapeDtypeStruct((b, steps * tq, w), BF16),
        scratch_shapes=[pltpu.VMEM((2, 2, kc, tq), F32),
                        pltpu.VMEM((2, 1, tq), F32), pltpu.VMEM((2, 1, tq), F32),
                        pltpu.VMEM((2, DIFF_DV, tq), F32)],
        compiler_params=_cparams(("parallel", "parallel", "arbitrary")),
        name="diff_attn_ctx" if ctx else "diff_attn",
    )(dlam, dng, *([dq] * nq), dk, dvt)


def _outproj_kernel(x_ref, conv_ref, hf_ref, hb_ref, mo_ref, dyc_ref, dyl_ref, mod_ref, mng_ref, n2g_ref,
                    wo_ref, xo_ref, h2_ref, h2t_ref):
    cw = conv_ref.shape[1]
    mw = hf_ref.shape[1]
    hh = hf_ref[...] + hb_ref[...]
    r = lax.broadcasted_iota(jnp.int32, (mw, mw), 0) // MLSTM_DH
    c = lax.broadcasted_iota(jnp.int32, (mw, mw), 1) // MLSTM_DH
    gm = jnp.where(r == c, 1.0 / MLSTM_DH, 0.0)
    ms = jnp.dot(hh * hh, gm, precision=HIGHEST, preferred_element_type=F32)
    ym = hh * lax.rsqrt(ms + EPS) * mng_ref[...] * jax.nn.sigmoid(mo_ref[...])
    dy = jnp.where(pl.program_id(1) == 0, dyc_ref[...], dyl_ref[...])
    o = (jnp.dot(conv_ref[...], wo_ref[0:cw, :], preferred_element_type=F32)
         + jnp.dot(ym.astype(BF16), wo_ref[cw:cw + mw, :], preferred_element_type=F32)
         + jnp.dot(dy, wo_ref[cw + mw:, :], preferred_element_type=F32))
    x = x_ref[...] + mod_ref[2:3, :] * o
    xo_ref[...] = x
    h2 = _rms(x) * n2g_ref[...] * (1.0 + mod_ref[4:5, :]) + mod_ref[3:4, :]
    h2_ref[...] = h2.astype(BF16)
    h2t_ref[...] = h2.T.astype(BF16)


def _outproj_call(x, conv, hf, hb, mo, dyc, dyl, mod, mng, n2g, wo):
    b, t, d = x.shape
    nt = t // TILE
    cw, mw, dw = conv.shape[-1], mo.shape[-1], dyl.shape[-1]
    tok = lambda bi, i: (bi, i, 0)
    full2 = lambda arr: pl.BlockSpec(arr.shape, lambda bi, i: (0, 0))
    return pl.pallas_call(
        _outproj_kernel,
        grid=(b, nt),
        in_specs=[pl.BlockSpec((None, TILE, d), tok),
                  pl.BlockSpec((None, TILE, cw), tok),
                  pl.BlockSpec((None, TILE, mw), tok),
                  pl.BlockSpec((None, TILE, mw), tok),
                  pl.BlockSpec((None, TILE, mw), tok),
                  pl.BlockSpec((None, TILE, dw), lambda bi, i: (bi, 0, 0)),
                  pl.BlockSpec((None, TILE, dw), lambda bi, i: (bi, jnp.maximum(i - 1, 0), 0)),
                  pl.BlockSpec((None, None, 6, d), lambda bi, i: (bi, jnp.minimum(i, 1), 0, 0)),
                  full2(mng), full2(n2g), full2(wo)],
        out_specs=[pl.BlockSpec((None, TILE, d), tok), pl.BlockSpec((None, TILE, d), tok),
                   pl.BlockSpec((d, TILE), lambda bi, i: (0, bi * nt + i))],
        out_shape=[jax.ShapeDtypeStruct((b, t, d), F32), jax.ShapeDtypeStruct((b, t, d), BF16),
                   jax.ShapeDtypeStruct((d, b * t), BF16)],
        compiler_params=_cparams(("parallel", "parallel")),
        name="out_proj",
    )(x, conv, hf, hb, mo, dyc, dyl, mod, mng, n2g, wo)


def _top_k_rows(s, k):
    idx = lax.broadcasted_iota(jnp.int32, s.shape, 0).astype(F32)
    vals, ids = [], []
    cur = s
    for _ in range(k):
        m = jnp.max(cur, axis=0, keepdims=True)
        am = jnp.min(jnp.where(cur == m, idx, float(s.shape[0])), axis=0, keepdims=True)
        vals.append(m)
        ids.append(am)
        cur = jnp.where(idx == am, -jnp.inf, cur)
    return vals, ids


def _select_exact(s0, s1):
    kk = PEER_TOPK
    tt = s0.shape[1]
    iota_k = lax.broadcasted_iota(jnp.int32, (kk, tt), 0).astype(F32)
    iota_n = lax.broadcasted_iota(jnp.int32, (N_KEYS, tt), 0).astype(F32)
    v0, i0 = _top_k_rows(s0, kk)
    v1, i1 = _top_k_rows(s1, kk)
    st0 = jnp.concatenate(v0, axis=0)
    n = jnp.zeros((kk, tt), F32)
    hv = st0 + v1[0]
    mx = v0[0] + v1[0]
    zsum = jnp.zeros((1, tt), F32)
    for _ in range(kk):
        m = jnp.max(hv, axis=0, keepdims=True)
        a_star = jnp.min(jnp.where(hv == m, iota_k, float(kk)), axis=0, keepdims=True)
        sel = iota_k == a_star
        zsum = zsum + jnp.exp(m - mx)
        n = n + jnp.where(sel, 1.0, 0.0)
        nxt = jnp.full((kk, tt), -jnp.inf, F32)
        for b in range(1, kk):
            nxt = jnp.where(n == float(b), v1[b], nxt)
        hv = jnp.where(sel, st0 + nxt, hv)
    ni = jnp.zeros((N_KEYS, tt), F32)
    r1 = jnp.full((N_KEYS, tt), float(kk), F32)
    for a in range(kk):
        ni = jnp.where(iota_n == i0[a], n[a:a + 1, :], ni)
        r1 = jnp.where(iota_n == i1[a], float(a), r1)
    return jnp.exp(s0 - v0[0]) / zsum, ni, r1, jnp.exp(s1 - v1[0])


def _top_k_ranks(s, k):
    cur = s
    rank = jnp.full(s.shape, float(k), F32)
    vals = []
    for j in range(k):
        m = jnp.max(cur, axis=0, keepdims=True)
        eq = cur == m
        vals.append(m)
        rank = jnp.where(eq, float(j), rank)
        cur = jnp.where(eq, -jnp.inf, cur)
    cnt = jnp.sum(jnp.where(rank < float(k), 1.0, 0.0), axis=0, keepdims=True)
    return vals, rank, cnt


def _select_fast(s0, s1):
    kk = PEER_TOPK
    tt = s0.shape[1]
    v0, rank0, cnt0 = _top_k_ranks(s0, kk)
    v1, rank1, cnt1 = _top_k_ranks(s1, kk)
    half = SUBLANES
    st1 = jnp.concatenate(v1, axis=0)
    st1_h = st1[0:half]
    riota = lax.broadcasted_iota(jnp.int32, (half, tt), 0)
    blocks = [v0[0] + st1]
    for a in range(1, half):
        blocks.append(jnp.where(riota < kk // (a + 1), v0[a] + st1_h, -jnp.inf))
    blocks.append(jnp.concatenate(v0[half:], axis=0) + v1[0])
    p = jnp.concatenate(blocks, axis=0)
    cur = p
    for _ in range(kk):
        m = jnp.max(cur, axis=0, keepdims=True)
        cur = jnp.where(cur == m, -jnp.inf, cur)
    picked = jnp.where(cur != p, 1.0, 0.0)
    mx = v0[0] + v1[0]
    zsum = jnp.sum(picked * jnp.exp(p - mx), axis=0, keepdims=True)
    cntm = jnp.sum(picked, axis=0, keepdims=True)
    n_rows = [jnp.sum(picked[0:kk], axis=0, keepdims=True)]
    for a in range(1, half):
        lo = kk + (a - 1) * half
        n_rows.append(jnp.sum(picked[lo:lo + half], axis=0, keepdims=True))
    last = kk + (half - 1) * half
    n_rows += [picked[last + r:last + r + 1] for r in range(kk - half)]
    ni = jnp.zeros((N_KEYS, tt), F32)
    for a in range(kk):
        ni = jnp.where(rank0 == float(a), n_rows[a], ni)
    want = float(kk)
    bad = jnp.where((cnt0 != want) | (cnt1 != want) | (cntm != want), 1.0, 0.0)
    return jnp.exp(s0 - v0[0]) / zsum, ni, rank1, jnp.exp(s1 - v1[0]), bad


def _peer_sel_kernel(h2_ref, wq_ref, kh_ref, a_ref, ni_ref, r1_ref, bv_ref):
    q = jnp.dot(h2_ref[...], wq_ref[...], preferred_element_type=F32).astype(BF16)
    tt = q.shape[0]
    hw = q.shape[1] // PEER_HEADS

    def head_scores(h):
        st_ = lax.dot_general(kh_ref[h], q[:, h * hw:(h + 1) * hw], NT_DIMS,
                              preferred_element_type=F32)
        return st_[0:N_KEYS], st_[N_KEYS:2 * N_KEYS]

    def store(h, a, ni, r1, bv):
        a_ref[h] = a
        ni_ref[h] = ni
        r1_ref[h] = r1.astype(BF16)
        bv_ref[h] = bv.astype(BF16)

    for h in range(PEER_HEADS):
        s0, s1 = head_scores(h)
        a, ni, r1, bv, bad = _select_fast(s0, s1)
        store(h, a, ni, r1, bv)

        @pl.when(jnp.max(bad) > 0.0)
        def _(h=h, s0=s0, s1=s1):
            store(h, *_select_exact(s0, s1))


def _peer_sel_call(h2, wq, kh):
    ntok, d = h2.shape
    nt = ntok // TILE
    shp = jax.ShapeDtypeStruct((PEER_HEADS, N_KEYS, ntok), F32)
    shp_b = jax.ShapeDtypeStruct((PEER_HEADS, N_KEYS, ntok), BF16)
    ospec = pl.BlockSpec((PEER_HEADS, N_KEYS, TILE), lambda i: (0, 0, i))
    return pl.pallas_call(
        _peer_sel_kernel,
        grid=(nt,),
        in_specs=[pl.BlockSpec((TILE, d), lambda i: (i, 0)),
                  pl.BlockSpec(wq.shape, lambda i: (0, 0)),
                  pl.BlockSpec(kh.shape, lambda i: (0, 0, 0))],
        out_specs=[ospec] * 4,
        out_shape=[shp, shp, shp_b, shp_b],
        compiler_params=_cparams(("parallel",)),
        name="peer_select",
    )(h2, wq, kh)


def _gelu(x):
    return 0.5 * x * (1.0 + lax.erf(x * (2.0 ** -0.5)))


def _peer_dense_kernel(h2t_ref, a_ref, ni_ref, r1_ref, bv_ref, u_ref, vt_ref, o_ref, acc_ref):
    c = pl.program_id(1)

    @pl.when(c == 0)
    def _():
        acc_ref[...] = jnp.zeros_like(acc_ref)

    tm = h2t_ref.shape[1]
    nib = u_ref.shape[0] // N_KEYS
    rep = N_KEYS // BF16_SUBLANES

    def rows(ref, h, ib):
        r = jnp.broadcast_to(ref[h, ib:ib + 1, :], (BF16_SUBLANES, tm)).astype(BF16)
        return jnp.concatenate([r] * rep, axis=0)

    st_ = jnp.dot(u_ref[...], h2t_ref[...], preferred_element_type=F32)
    blocks = []
    for ib in range(nib):
        g = None
        for h in range(PEER_HEADS):
            bvh = bv_ref[h]
            keep = r1_ref[h] < rows(ni_ref, h, ib)
            term = rows(a_ref, h, ib) * jnp.where(keep, bvh, jnp.zeros_like(bvh))
            g = term if g is None else g + term
        blocks.append(g * _gelu(st_[ib * N_KEYS:(ib + 1) * N_KEYS]).astype(BF16))
    wt = jnp.concatenate(blocks, axis=0)
    acc_ref[...] += jnp.dot(vt_ref[...], wt, preferred_element_type=F32)

    @pl.when(c == pl.num_programs(1) - 1)
    def _():
        o_ref[...] = acc_ref[...].T


def _peer_dense_call(h2t, a, ni, r1, bv, u, vt):
    d, ntok = h2t.shape
    ne = u.shape[0]
    tm, ec = PEER_TM, PEER_EC
    ib = ec // N_KEYS
    rowspec = pl.BlockSpec((PEER_HEADS, ib, tm), lambda t, c: (0, c, t))
    colspec = pl.BlockSpec((PEER_HEADS, N_KEYS, tm), lambda t, c: (0, 0, t))
    return pl.pallas_call(
        _peer_dense_kernel,
        grid=(ntok // tm, ne // ec),
        in_specs=[pl.BlockSpec((d, tm), lambda t, c: (0, t)),
                  rowspec, rowspec, colspec, colspec,
                  pl.BlockSpec((ec, d), lambda t, c: (c, 0)),
                  pl.BlockSpec((d, ec), lambda t, c: (0, c))],
        out_specs=pl.BlockSpec((tm, d), lambda t, c: (t, 0)),
        out_shape=jax.ShapeDtypeStruct((ntok, d), F32),
        scratch_shapes=[pltpu.VMEM((d, tm), F32)],
        compiler_params=_cparams(("parallel", "arbitrary")),
        name="peer_dense",
    )(h2t, a, ni, r1, bv, u, vt)


def _final_kernel(x_ref, peer_ref, mod_ref, g_ref, o_ref):
    x = x_ref[...] + mod_ref[5:6, :] * peer_ref[...]
    o_ref[...] = _rms(x) * g_ref[...]


def _final_call(x, peer, mod, fg, n_ctx_tiles):
    b, t, d = x.shape
    nl = t // TILE - n_ctx_tiles
    tok = lambda bi, i: (bi, i + n_ctx_tiles, 0)
    return pl.pallas_call(
        _final_kernel,
        grid=(b, nl),
        in_specs=[pl.BlockSpec((None, TILE, d), tok), pl.BlockSpec((None, TILE, d), tok),
                  pl.BlockSpec((None, None, 6, d), lambda bi, i: (bi, 1, 0, 0)),
                  pl.BlockSpec(fg.shape, lambda bi, i: (0, 0))],
        out_specs=pl.BlockSpec((None, TILE, d), lambda bi, i: (bi, i, 0)),
        out_shape=jax.ShapeDtypeStruct((b, nl * TILE, d), F32),
        compiler_params=_cparams(("parallel", "parallel")),
        name="final_norm",
    )(x, peer, mod, fg)


def _rope_tables(seq, n_ctx):
    rows = seq // GRID_W
    axis_rot = DIFF_DH // 2
    row = jnp.repeat(jnp.arange(rows), GRID_W).astype(F32)
    col = jnp.tile(jnp.arange(GRID_W), rows).astype(F32)
    inv = ROPE_BASE ** (-jnp.arange(0, axis_rot, 2, dtype=F32) / axis_rot)
    ang = jnp.concatenate([row[:, None] * inv, col[:, None] * inv], axis=-1)
    cos = jnp.repeat(jnp.cos(ang), 2, axis=-1)
    sin = jnp.repeat(jnp.sin(ang), 2, axis=-1)
    even = (jnp.arange(DIFF_DH) % 2 == 0)[None, :]
    sa = jnp.where(even, -sin, 0.0)
    sb = jnp.where(even, 0.0, sin)
    rep = LANES // DIFF_DH

    def full(tab, ctx_val):
        tab = jnp.tile(tab, (1, rep))
        return jnp.concatenate([jnp.full((n_ctx, LANES), ctx_val, F32), tab], axis=0)

    return full(cos, 1.0), full(sa, 0.0), full(sb, 0.0)


def kernel(x, c, ctx, c_ctx, ada_w, ada_b, norm1_g, norm2_g, w_in, conv_w, conv_b, conv_ln_g, conv_ln_b, mlstm_gate_b, mlstm_norm_g, diff_lambda, diff_norm_g, w_out, peer_wq, peer_keys, peer_u, peer_v, final_g):
    b, seq, d = x.shape
    n_ctx = ctx.shape[1]
    depth = ada_w.shape[0]
    cw, mw, dw = d // 4, d // 4, d // 2
    ng = 4 * MLSTM_HEADS
    assert n_ctx == TILE and seq % ATT_TQ == 0 and seq % GRID_W == 0
    assert w_in.shape[-1] == 2 * cw + 4 * mw + ng + 3 * dw
    assert peer_keys.shape[1:] == (PEER_HEADS, 2, N_KEYS, d // PEER_HEADS // 2)
    assert conv_w.shape[1] == CONV_K and (b * (seq + n_ctx)) % PEER_TM == 0

    rows = -(-(b + 1) // SUBLANES) * SUBLANES
    cvec = jnp.zeros((rows, d), F32).at[:b].set(c).at[b].set(c_ctx)
    mods = _ada_call(cvec, ada_w, ada_b)
    mod_lat = mods[:, :b].reshape(depth, b, 1, 6, d)
    mod_ctx = jnp.broadcast_to(mods[:, b].reshape(depth, 1, 1, 6, d), (depth, b, 1, 6, d))
    mod_all = jnp.concatenate([mod_ctx, mod_lat], axis=2)

    cos, sa, sb = _rope_tables(seq, n_ctx)
    g0 = 2 * cw + 4 * mw
    w_main = jnp.concatenate([w_in[:, :, :g0], w_in[:, :, g0 + ng:g0 + ng + 2 * dw], w_in[:, :, g0:g0 + ng],
                              jnp.zeros((depth, d, LANES - ng), F32)], axis=-1).astype(BF16)
    w_dvt = jnp.swapaxes(w_in[:, :, g0 + ng + 2 * dw:], 1, 2).astype(BF16)
    gbias = jnp.pad(mlstm_gate_b.reshape(depth, 1, ng), ((0, 0), (0, 0), (0, LANES - ng)))
    w_out_b = w_out.astype(BF16)
    wq_b = peer_wq.astype(BF16)
    hw = d // PEER_HEADS
    kz = jnp.zeros((depth, PEER_HEADS, N_KEYS, hw // 2), F32)
    kh = jnp.concatenate([jnp.concatenate([peer_keys[:, :, 0], kz], axis=-1),
                          jnp.concatenate([kz, peer_keys[:, :, 1]], axis=-1)], axis=2).astype(BF16)
    u_b = peer_u.astype(BF16)
    vt_b = jnp.swapaxes(peer_v, 1, 2).astype(BF16)
    mng = jnp.tile(mlstm_norm_g, (1, MLSTM_HEADS))

    xs = jnp.concatenate([ctx, x], axis=1)
    t = xs.shape[1]
    peer = None
    for l in range(depth):
        lam_init = 0.8 - 0.6 * math.exp(-0.3 * l)
        xs, u, mqkv, mo, gcol, grow, dq, dk, dvt = _inproj_call(
            xs, peer, mod_all[l - 1] if l else None, mod_all[l], norm1_g[l][None], w_main[l],
            w_dvt[l], gbias[l], ng, cos, sa, sb)
        conv = _conv_call(u, conv_w[l, :, 0, :], conv_b[l][None], conv_ln_g[l][None], conv_ln_b[l][None])
        gcol_d = gcol.reshape(b, t, 2, ng // 2).transpose(0, 2, 1, 3)
        grow_d = grow.reshape(b, 2, ng // 2, t)
        hf, hb = _mlstm_call(mqkv, gcol_d, grow_d)
        dyc = _attn_call(dq, dk, dvt, diff_lambda[l], diff_norm_g[l][:, None], lam_init, ctx=True)
        dyl = _attn_call(dq, dk, dvt, diff_lambda[l], diff_norm_g[l][:, None], lam_init, ctx=False)
        xs, h2, h2t = _outproj_call(xs, conv, hf, hb, mo, dyc, dyl, mod_all[l], mng[l][None], norm2_g[l][None],
                                    w_out_b[l])
        h2f = h2.reshape(b * t, d)
        a, ni, r1, bv = _peer_sel_call(h2f, wq_b[l], kh[l])
        peer = _peer_dense_call(h2t, a, ni, r1, bv, u_b[l], vt_b[l]).reshape(b, t, d)
    return _final_call(xs, peer, mod_all[depth - 1], final_g[None], n_ctx // TILE)
```

```python
import functools
import math

import jax
import jax.numpy as jnp
from jax import lax
from jax.experimental import pallas as pl
from jax.experimental.pallas import tpu as pltpu

F32 = jnp.float32
BF16 = jnp.bfloat16
HIGHEST = lax.Precision.HIGHEST

GRID_W = 64
EPS = 1e-6
CONV_K = 31
MLSTM_DH = 64
MLSTM_HEADS = 4
DIFF_DH = 64
DIFF_DV = 128
DIFF_HEADS = 4
ROPE_BASE = 10000.0
PEER_HEADS = 8
N_KEYS = 128
PEER_TOPK = 16

LANES = 128
SUBLANES = 8
TILE = 256
CONV_HALO = 16
ATT_KC = 1024
ATT_TQ = 512
PEER_SEL_TILE = 256
PEER_TM = 1024
BF16_SUBLANES = 16
PEER_EC = 1024
VMEM_LIMIT = 56 * 1024 * 1024

NT_DIMS = (((1,), (1,)), ((), ()))


def _cparams(sem):
    return pltpu.CompilerParams(dimension_semantics=sem, vmem_limit_bytes=VMEM_LIMIT)


def _rms(x, eps=EPS):
    return x * lax.rsqrt(jnp.mean(x * x, axis=-1, keepdims=True) + eps)


def _log_sigmoid(x):
    return jnp.minimum(x, 0.0) - jnp.log(1.0 + jnp.exp(-jnp.abs(x)))


def _ada_kernel(c_ref, w_ref, b_ref, o_ref):
    c = c_ref[...]
    s = (c * jax.nn.sigmoid(c)).astype(BF16)
    o_ref[...] = jnp.dot(s, w_ref[...].astype(BF16), preferred_element_type=F32) + b_ref[...]


def _ada_call(cvec, ada_w, ada_b):
    depth, d, n = ada_w.shape
    tn = 1536
    rows = cvec.shape[0]
    return pl.pallas_call(
        _ada_kernel,
        grid=(depth, n // tn),
        in_specs=[pl.BlockSpec((rows, d), lambda l, j: (0, 0)),
                  pl.BlockSpec((None, d, tn), lambda l, j: (l, 0, j)),
                  pl.BlockSpec((None, 1, tn), lambda l, j: (l, 0, j))],
        out_specs=pl.BlockSpec((None, rows, tn), lambda l, j: (l, 0, j)),
        out_shape=jax.ShapeDtypeStruct((depth, rows, n), F32),
        compiler_params=_cparams(("parallel", "parallel")),
        name="ada_mod",
    )(cvec, ada_w, ada_b.reshape(depth, 1, n))


def _rope(t, c, sa, sb):
    w = t.shape[1]
    rep = w // LANES
    c, sa, sb = (jnp.concatenate([z] * rep, axis=1) for z in (c, sa, sb))
    return t * c + pltpu.roll(t, w - 1, 1) * sa + pltpu.roll(t, 1, 1) * sb


def _inproj_kernel(*refs, has_peer, cw, mw):
    if has_peer:
        x_ref, peer_ref, modp_ref = refs[:3]
        refs = refs[3:]
    (mod_ref, n1g_ref, wm_ref, wdvt_ref, gb_ref, cos_ref, sa_ref, sb_ref,
     xo_ref, u_ref, mqkv_ref, mo_ref, gcol_ref, grow_ref, dq_ref, dk_ref, dvt_ref) = refs[-17:]
    if not has_peer:
        x_ref = refs[0]
    x = x_ref[...]
    if has_peer:
        x = x + modp_ref[5:6, :] * peer_ref[...]
        xo_ref[...] = x
    else:
        xo_ref[...] = x
    h = _rms(x) * n1g_ref[...] * (1.0 + mod_ref[1:2, :]) + mod_ref[0:1, :]
    hb = h.astype(BF16)

    def proj(lo, hi):
        return jnp.dot(hb, wm_ref[:, lo:hi], preferred_element_type=F32)

    a = proj(0, 2 * cw)
    u_ref[...] = a[:, :cw] * jax.nn.sigmoid(a[:, cw:])
    o = 2 * cw
    mqkv_ref[:, 0:mw] = proj(o, o + mw).astype(BF16)
    mqkv_ref[:, mw:2 * mw] = (proj(o + mw, o + 2 * mw) * (MLSTM_DH ** -0.5)).astype(BF16)
    mqkv_ref[:, 2 * mw:3 * mw] = proj(o + 2 * mw, o + 3 * mw).astype(BF16)
    mo_ref[...] = proj(o + 3 * mw, o + 4 * mw)
    o = o + 4 * mw
    dw = dq_ref.shape[1]
    c, sa, sb = cos_ref[...], sa_ref[...], sb_ref[...]
    dq_ref[...] = (_rope(proj(o, o + dw), c, sa, sb) * (DIFF_DH ** -0.5 * math.log2(math.e))).astype(BF16)
    dk_ref[...] = _rope(proj(o + dw, o + 2 * dw), c, sa, sb).astype(BF16)
    dvt_ref[...] = lax.dot_general(wdvt_ref[...], hb, NT_DIMS, preferred_element_type=F32).astype(BF16)

    o = o + 2 * dw
    ng = gcol_ref.shape[1]
    g = proj(o, o + LANES) + gb_ref[...]
    cidx = lax.broadcasted_iota(jnp.int32, g.shape, 1)
    g = jnp.where((cidx // MLSTM_HEADS) % 2 == 1, _log_sigmoid(g), g)
    gcol_ref[...] = g[:, :ng]
    grow_ref[...] = g.T[:ng, :]


def _inproj_call(x, peer, modp, mod, n1g, wm, wdvt, gb, ng, cos, sa, sb):
    b, t, d = x.shape
    nt = t // TILE
    cw = d // 4
    mw = d // 4
    dw = d // 2
    has_peer = peer is not None
    tok = lambda bi, i: (bi, i, 0)
    modspec = pl.BlockSpec((None, None, 6, d), lambda bi, i: (bi, jnp.minimum(i, 1), 0, 0))
    full2 = lambda arr: pl.BlockSpec(arr.shape, lambda bi, i: (0, 0))
    in_specs = [pl.BlockSpec((None, TILE, d), tok)]
    args = [x]
    if has_peer:
        in_specs += [pl.BlockSpec((None, TILE, d), tok), modspec]
        args += [peer, modp]
    in_specs += [modspec, full2(n1g), full2(wm), full2(wdvt), full2(gb)]
    args += [mod, n1g, wm, wdvt, gb]
    in_specs += [pl.BlockSpec((TILE, LANES), lambda bi, i: (i, 0))] * 3
    args += [cos, sa, sb]
    out_shape = [jax.ShapeDtypeStruct((b, t, d), F32),
                 jax.ShapeDtypeStruct((b, t, cw), F32),
                 jax.ShapeDtypeStruct((b, t, 3 * mw), BF16),
                 jax.ShapeDtypeStruct((b, t, mw), F32),
                 jax.ShapeDtypeStruct((b, t, ng), F32),
                 jax.ShapeDtypeStruct((b, ng, t), F32),
                 jax.ShapeDtypeStruct((b, t, dw), BF16),
                 jax.ShapeDtypeStruct((b, t, dw), BF16),
                 jax.ShapeDtypeStruct((b, dw, t), BF16)]
    out_specs = [pl.BlockSpec((None, TILE, d), tok),
                 pl.BlockSpec((None, TILE, cw), tok),
                 pl.BlockSpec((None, TILE, 3 * mw), tok),
                 pl.BlockSpec((None, TILE, mw), tok),
                 pl.BlockSpec((None, TILE, ng), tok),
                 pl.BlockSpec((None, ng, TILE), lambda bi, i: (bi, 0, i)),
                 pl.BlockSpec((None, TILE, dw), tok),
                 pl.BlockSpec((None, TILE, dw), tok),
                 pl.BlockSpec((None, dw, TILE), lambda bi, i: (bi, 0, i))]
    return pl.pallas_call(
        functools.partial(_inproj_kernel, has_peer=has_peer, cw=cw, mw=mw),
        grid=(b, nt), in_specs=in_specs, out_specs=out_specs, out_shape=out_shape,
        compiler_params=_cparams(("parallel", "parallel")),
        name="in_proj",
    )(*args)


def _conv_kernel(up_ref, uc_ref, un_ref, w_ref, b_ref, lg_ref, lb_ref, o_ref, ext_ref):
    i = pl.program_id(1)
    nt = pl.num_programs(1)
    lm = jnp.where(i >= 2, 1.0, 0.0)
    rm = jnp.where(jnp.logical_and(i >= 1, i < nt - 1), 1.0, 0.0)
    hl = CONV_HALO
    ext_ref[0:hl, :] = up_ref[TILE - hl:TILE, :] * lm
    ext_ref[hl:hl + TILE, :] = uc_ref[...]
    ext_ref[hl + TILE:2 * hl + TILE, :] = un_ref[0:hl, :] * rm
    off = hl - CONV_K // 2
    acc = jnp.zeros(uc_ref.shape, F32)
    for k in range(CONV_K):
        acc = acc + w_ref[k:k + 1, :] * ext_ref[off + k:off + k + TILE, :]
    y = acc + b_ref[...]
    mu = jnp.mean(y, axis=-1, keepdims=True)
    yc = y - mu
    var = jnp.mean(yc * yc, axis=-1, keepdims=True)
    z = yc * lax.rsqrt(var + EPS) * lg_ref[...] + lb_ref[...]
    o_ref[...] = (z * jax.nn.sigmoid(z)).astype(o_ref.dtype)


def _conv_call(u, w, bias, lg, lb):
    b, t, cw = u.shape
    nt = t // TILE
    full2 = lambda arr: pl.BlockSpec(arr.shape, lambda bi, i: (0, 0))
    return pl.pallas_call(
        _conv_kernel,
        grid=(b, nt),
        in_specs=[pl.BlockSpec((None, TILE, cw), lambda bi, i: (bi, jnp.maximum(i - 1, 0), 0)),
                  pl.BlockSpec((None, TILE, cw), lambda bi, i: (bi, i, 0)),
                  pl.BlockSpec((None, TILE, cw), lambda bi, i: (bi, jnp.minimum(i + 1, nt - 1), 0)),
                  full2(w), full2(bias), full2(lg), full2(lb)],
        out_specs=pl.BlockSpec((None, TILE, cw), lambda bi, i: (bi, i, 0)),
        out_shape=jax.ShapeDtypeStruct((b, t, cw), BF16),
        scratch_shapes=[pltpu.VMEM((TILE + 2 * CONV_HALO, cw), F32)],
        compiler_params=_cparams(("parallel", "parallel")),
        name="conv_module",
    )(u, u, u, w, bias, lg, lb)


def _mlstm_chunk(qkv_ref, gc_ref, gr_ref, h_ref, c_ref, n_ref, m_ref, fwd):
    tc = qkv_ref.shape[0]
    mw = qkv_ref.shape[1] // 3
    nh = MLSTM_HEADS
    row = lax.broadcasted_iota(jnp.int32, (tc, tc), 0)
    col = lax.broadcasted_iota(jnp.int32, (tc, tc), 1)
    tri = row >= col if fwd else row <= col
    trif = tri.astype(F32)
    gc = gc_ref[...]
    gr = gr_ref[...]
    bcol = jnp.dot(trif, gc, precision=HIGHEST, preferred_element_type=F32)
    brow = lax.dot_general(gr, trif, NT_DIMS, precision=HIGHEST, preferred_element_type=F32)
    bl = jnp.sum(gr, axis=1, keepdims=True)

    q = qkv_ref[:, 0:mw]
    k = qkv_ref[:, mw:2 * mw]
    v = qkv_ref[:, 2 * mw:3 * mw]
    lane_head = lax.broadcasted_iota(jnp.int32, (tc, mw), 1) // MLSTM_DH
    cb = c_ref[...]
    inter_c = lax.dot_general(q, cb.astype(BF16), NT_DIMS, preferred_element_type=F32)
    qf = q.astype(F32)
    qn = qf * n_ref[...]

    out = jnp.zeros((tc, mw), F32)
    wkfull = jnp.zeros((tc, mw), F32)
    rhead = lax.broadcasted_iota(jnp.int32, (mw, 1), 0) // MLSTM_DH
    chead = lax.broadcasted_iota(jnp.int32, (1, mw), 1) // MLSTM_DH
    decay_col = jnp.zeros((mw, 1), F32)
    decay_row = jnp.zeros((1, mw), F32)
    for h in range(nh):
        mh = m_ref[h:h + 1, 0:1]
        bc = bcol[:, nh + h:nh + h + 1]
        br = brow[nh + h:nh + h + 1, :]
        ir = gr[h:h + 1, :]
        dlog = jnp.where(tri, bc - br + ir, -jnp.inf)
        inter = bc + mh
        mt = jnp.maximum(inter, jnp.max(dlog, axis=1, keepdims=True))
        dwt = jnp.exp(dlog - mt)
        iw = jnp.exp(inter - mt)
        hm = lane_head == h
        qh = jnp.where(hm, qf, 0.0).astype(BF16)
        s = lax.dot_general(qh, k, NT_DIMS, preferred_element_type=F32) * dwt
        sv = jnp.dot(s.astype(BF16), v, preferred_element_type=F32)
        qn_h = jnp.sum(jnp.where(hm, qn, 0.0), axis=1, keepdims=True)
        den = jnp.sum(s, axis=1, keepdims=True) + iw * qn_h
        denom = jnp.maximum(jnp.abs(den), jnp.exp(-mt))
        out = jnp.where(hm, (sv + iw * inter_c) / denom, out)
        blh = bl[nh + h:nh + h + 1, :]
        wlog_r = blh - br + ir
        mn = jnp.maximum(blh + mh, jnp.max(wlog_r, axis=1, keepdims=True))
        decay = jnp.exp(blh + mh - mn)
        wk_c = jnp.exp(blh - bc + gc[:, h:h + 1] - mn)
        wkfull = jnp.where(hm, wk_c, wkfull)
        decay_col = jnp.where(rhead == h, decay, decay_col)
        decay_row = jnp.where(chead == h, decay, decay_row)
        m_ref[h:h + 1, :] = jnp.broadcast_to(mn, (1, m_ref.shape[1]))
    h_ref[...] = out

    vw = v.astype(F32) * wkfull
    upd = jnp.dot(vw.T.astype(BF16), k, preferred_element_type=F32)
    c_ref[...] = decay_col * cb + jnp.where(rhead == chead, upd, 0.0)
    n_ref[...] = decay_row * n_ref[...] + jnp.sum(k.astype(F32) * wkfull, axis=0, keepdims=True)


def _mlstm_kernel(qkvf_ref, gcf_ref, grf_ref, qkvb_ref, gcb_ref, grb_ref, hf_ref, hb_ref,
                  c_ref, n_ref, m_ref):
    @pl.when(pl.program_id(1) == 0)
    def _():
        c_ref[...] = jnp.zeros_like(c_ref)
        n_ref[...] = jnp.zeros_like(n_ref)
        m_ref[...] = jnp.zeros_like(m_ref)

    _mlstm_chunk(qkvf_ref, gcf_ref, grf_ref, hf_ref, c_ref.at[0], n_ref.at[0], m_ref.at[0], True)
    _mlstm_chunk(qkvb_ref, gcb_ref, grb_ref, hb_ref, c_ref.at[1], n_ref.at[1], m_ref.at[1], False)


def _mlstm_call(mqkv, gcol, grow):
    b, t, w3 = mqkv.shape
    mw = w3 // 3
    nt = t // TILE
    ng = gcol.shape[-1]

    def rev(j):
        return jnp.where(j == 0, 0, nt - j)

    hshape = jax.ShapeDtypeStruct((b, t, mw), F32)
    return pl.pallas_call(
        _mlstm_kernel,
        grid=(b, nt),
        in_specs=[pl.BlockSpec((None, TILE, w3), lambda bi, j: (bi, j, 0)),
                  pl.BlockSpec((None, None, TILE, ng), lambda bi, j: (bi, 0, j, 0)),
                  pl.BlockSpec((None, None, ng, TILE), lambda bi, j: (bi, 0, 0, j)),
                  pl.BlockSpec((None, TILE, w3), lambda bi, j: (bi, rev(j), 0)),
                  pl.BlockSpec((None, None, TILE, ng), lambda bi, j: (bi, 1, rev(j), 0)),
                  pl.BlockSpec((None, None, ng, TILE), lambda bi, j: (bi, 1, 0, rev(j)))],
        out_specs=[pl.BlockSpec((None, TILE, mw), lambda bi, j: (bi, j, 0)),
                   pl.BlockSpec((None, TILE, mw), lambda bi, j: (bi, rev(j), 0))],
        out_shape=[hshape, hshape],
        scratch_shapes=[pltpu.VMEM((2, mw, mw), F32), pltpu.VMEM((2, 1, mw), F32),
                        pltpu.VMEM((2, SUBLANES, LANES), F32)],
        compiler_params=_cparams(("arbitrary", "arbitrary")),
        name="mlstm_scan",
    )(mqkv, gcol, grow, mqkv, gcol, grow)


def _attn_kernel(*refs, lam_init, kc, nq):
    dl_ref, ng_ref = refs[:2]
    q_refs = refs[2:2 + nq]
    k_ref, vt_ref, o_ref, s_ref, m_ref, l_ref, acc_ref = refs[2 + nq:]
    n_keys = k_ref.shape[0]
    q = jnp.concatenate([r[...] for r in q_refs], axis=0) if nq > 1 else q_refs[0][...]
    lane = lax.broadcasted_iota(jnp.int32, q.shape, 1)
    zero = jnp.zeros_like(q)
    qz = (jnp.where(lane < DIFF_DH, q, zero), jnp.where(lane >= DIFF_DH, q, zero))
    m_ref[...] = jnp.full(m_ref.shape, -jnp.inf, F32)
    l_ref[...] = jnp.zeros_like(l_ref)
    acc_ref[...] = jnp.zeros_like(acc_ref)

    def scores(start, slot):
        kb = k_ref[pl.ds(start, kc), :]
        for c in range(2):
            s_ref[slot, c] = lax.dot_general(kb, qz[c], NT_DIMS, preferred_element_type=F32)

    def softmax_pv(start, slot):
        vtb = vt_ref[:, pl.ds(start, kc)]
        for c in range(2):
            s = s_ref[slot, c]
            m_old = m_ref[c]
            m_new = jnp.maximum(m_old, jnp.max(s, axis=0, keepdims=True))
            alpha = jnp.exp2(m_old - m_new)
            p = jnp.exp2(s - m_new)
            l_ref[c] = alpha * l_ref[c] + jnp.sum(p, axis=0, keepdims=True)
            acc_ref[c] = alpha * acc_ref[c] + jnp.dot(vtb, p.astype(BF16), preferred_element_type=F32)
            m_ref[c] = m_new

    n = n_keys // kc
    pairs = (n - 1) // 2
    scores(0, 0)
    if pairs:
        def body(jj, carry):
            base = pl.multiple_of(jj * (2 * kc), TILE)
            scores(base + kc, 1)
            softmax_pv(base, 0)
            scores(base + 2 * kc, 0)
            softmax_pv(base + kc, 1)
            return carry

        lax.fori_loop(0, pairs, body, 0)
    done = 2 * pairs
    if n - done == 2:
        scores((done + 1) * kc, 1)
    softmax_pv(done * kc, 0)
    if n - done == 2:
        softmax_pv((done + 1) * kc, 1)

    dl = dl_ref[...]
    lam = (jnp.exp(jnp.sum(dl[0:1] * dl[1:2], axis=1, keepdims=True))
           - jnp.exp(jnp.sum(dl[2:3] * dl[3:4], axis=1, keepdims=True)) + lam_init)
    ot = acc_ref[0] / l_ref[0] - lam * (acc_ref[1] / l_ref[1])
    ms = jnp.mean(ot * ot, axis=0, keepdims=True)
    y = ot * lax.rsqrt(ms + EPS) * (ng_ref[...] * (1.0 - lam_init))
    o_ref[...] = y.T.astype(o_ref.dtype)


def _attn_key_chunk(t):
    return max(k for k in range(TILE, ATT_KC + 1, TILE) if t % k == 0)


def _attn_call(dq, dk, dvt, dlam, dng, lam_init, *, ctx):
    b, t, w = dq.shape
    nh = w // DIFF_DV
    if ctx:
        n_keys, nq, steps, q0 = TILE, 1, 1, 0
    else:
        n_keys, nq, q0 = t, ATT_TQ // TILE, 1
        steps = (t - TILE) // ATT_TQ
    tq = nq * TILE
    kc = _attn_key_chunk(n_keys)
    kern = functools.partial(_attn_kernel, lam_init=lam_init, kc=kc, nq=nq)
    q_specs = [pl.BlockSpec((None, TILE, DIFF_DV), functools.partial(
        lambda bi, h, i, j: (bi, q0 + nq * i + j, h), j=j)) for j in range(nq)]
    return pl.pallas_call(
        kern,
        grid=(b, nh, steps),
        in_specs=[pl.BlockSpec(dlam.shape, lambda bi, h, i: (0, 0)),
                  pl.BlockSpec(dng.shape, lambda bi, h, i: (0, 0))] + q_specs + [
                  pl.BlockSpec((None, n_keys, DIFF_DV), lambda bi, h, i: (bi, 0, h)),
                  pl.BlockSpec((None, DIFF_DV, n_keys), lambda bi, h, i: (bi, h, 0))],
        out_specs=pl.BlockSpec((None, tq, DIFF_DV), lambda bi, h, i: (bi, i, h)),
        out_shape=jax.ShapeDtypeStruct((b, steps * tq, w), BF16),
        scratch_shapes=[pltpu.VMEM((2, 2, kc, tq), F32),
                        pltpu.VMEM((2, 1, tq), F32), pltpu.VMEM((2, 1, tq), F32),
                        pltpu.VMEM((2, DIFF_DV, tq), F32)],
        compiler_params=_cparams(("parallel", "parallel", "arbitrary")),
        name="diff_attn_ctx" if ctx else "diff_attn",
    )(dlam, dng, *([dq] * nq), dk, dvt)


def _outproj_kernel(x_ref, conv_ref, hf_ref, hb_ref, mo_ref, dyc_ref, dyl_ref, mod_ref, mng_ref, n2g_ref,
                    wo_ref, xo_ref, h2_ref, h2t_ref):
    cw = conv_ref.shape[1]
    mw = hf_ref.shape[1]
    hh = hf_ref[...] + hb_ref[...]
    r = lax.broadcasted_iota(jnp.int32, (mw, mw), 0) // MLSTM_DH
    c = lax.broadcasted_iota(jnp.int32, (mw, mw), 1) // MLSTM_DH
    gm = jnp.where(r == c, 1.0 / MLSTM_DH, 0.0)
    ms = jnp.dot(hh * hh, gm, precision=HIGHEST, preferred_element_type=F32)
    ym = hh * lax.rsqrt(ms + EPS) * mng_ref[...] * jax.nn.sigmoid(mo_ref[...])
    dy = jnp.where(pl.program_id(1) == 0, dyc_ref[...], dyl_ref[...])
    o = (jnp.dot(conv_ref[...], wo_ref[0:cw, :], preferred_element_type=F32)
         + jnp.dot(ym.astype(BF16), wo_ref[cw:cw + mw, :], preferred_element_type=F32)
         + jnp.dot(dy, wo_ref[cw + mw:, :], preferred_element_type=F32))
    x = x_ref[...] + mod_ref[2:3, :] * o
    xo_ref[...] = x
    h2 = _rms(x) * n2g_ref[...] * (1.0 + mod_ref[4:5, :]) + mod_ref[3:4, :]
    h2_ref[...] = h2.astype(BF16)
    h2t_ref[...] = h2.T.astype(BF16)


def _outproj_call(x, conv, hf, hb, mo, dyc, dyl, mod, mng, n2g, wo):
    b, t, d = x.shape
    nt = t // TILE
    cw, mw, dw = conv.shape[-1], mo.shape[-1], dyl.shape[-1]
    tok = lambda bi, i: (bi, i, 0)
    full2 = lambda arr: pl.BlockSpec(arr.shape, lambda bi, i: (0, 0))
    return pl.pallas_call(
        _outproj_kernel,
        grid=(b, nt),
        in_specs=[pl.BlockSpec((None, TILE, d), tok),
                  pl.BlockSpec((None, TILE, cw), tok),
                  pl.BlockSpec((None, TILE, mw), tok),
                  pl.BlockSpec((None, TILE, mw), tok),
                  pl.BlockSpec((None, TILE, mw), tok),
                  pl.BlockSpec((None, TILE, dw), lambda bi, i: (bi, 0, 0)),
                  pl.BlockSpec((None, TILE, dw), lambda bi, i: (bi, jnp.maximum(i - 1, 0), 0)),
                  pl.BlockSpec((None, None, 6, d), lambda bi, i: (bi, jnp.minimum(i, 1), 0, 0)),
                  full2(mng), full2(n2g), full2(wo)],
        out_specs=[pl.BlockSpec((None, TILE, d), tok), pl.BlockSpec((None, TILE, d), tok),
                   pl.BlockSpec((d, TILE), lambda bi, i: (0, bi * nt + i))],
        out_shape=[jax.ShapeDtypeStruct((b, t, d), F32), jax.ShapeDtypeStruct((b, t, d), BF16),
                   jax.ShapeDtypeStruct((d, b * t), BF16)],
        compiler_params=_cparams(("parallel", "parallel")),
        name="out_proj",
    )(x, conv, hf, hb, mo, dyc, dyl, mod, mng, n2g, wo)


def _top_k_rows(s, k):
    idx = lax.broadcasted_iota(jnp.int32, s.shape, 0).astype(F32)
    vals, ids = [], []
    cur = s
    for _ in range(k):
        m = jnp.max(cur, axis=0, keepdims=True)
        am = jnp.min(jnp.where(cur == m, idx, float(s.shape[0])), axis=0, keepdims=True)
        vals.append(m)
        ids.append(am)
        cur = jnp.where(idx == am, -jnp.inf, cur)
    return vals, ids


def _select_exact(s0, s1):
    kk = PEER_TOPK
    tt = s0.shape[1]
    iota_k = lax.broadcasted_iota(jnp.int32, (kk, tt), 0).astype(F32)
    iota_n = lax.broadcasted_iota(jnp.int32, (N_KEYS, tt), 0).astype(F32)
    v0, i0 = _top_k_rows(s0, kk)
    v1, i1 = _top_k_rows(s1, kk)
    st0 = jnp.concatenate(v0, axis=0)
    n = jnp.zeros((kk, tt), F32)
    hv = st0 + v1[0]
    mx = v0[0] + v1[0]
    zsum = jnp.zeros((1, tt), F32)
    for _ in range(kk):
        m = jnp.max(hv, axis=0, keepdims=True)
        a_star = jnp.min(jnp.where(hv == m, iota_k, float(kk)), axis=0, keepdims=True)
        sel = iota_k == a_star
        zsum = zsum + jnp.exp(m - mx)
        n = n + jnp.where(sel, 1.0, 0.0)
        nxt = jnp.full((kk, tt), -jnp.inf, F32)
        for b in range(1, kk):
            nxt = jnp.where(n == float(b), v1[b], nxt)
        hv = jnp.where(sel, st0 + nxt, hv)
    ni = jnp.zeros((N_KEYS, tt), F32)
    r1 = jnp.full((N_KEYS, tt), float(kk), F32)
    for a in range(kk):
        ni = jnp.where(iota_n == i0[a], n[a:a + 1, :], ni)
        r1 = jnp.where(iota_n == i1[a], float(a), r1)
    return jnp.exp(s0 - v0[0]) / zsum, ni, r1, jnp.exp(s1 - v1[0])


def _top_k_ranks(s, k):
    cur = s
    rank = jnp.full(s.shape, float(k), F32)
    vals = []
    for j in range(k):
        m = jnp.max(cur, axis=0, keepdims=True)
        eq = cur == m
        vals.append(m)
        rank = jnp.where(eq, float(j), rank)
        cur = jnp.where(eq, -jnp.inf, cur)
    cnt = jnp.sum(jnp.where(rank < float(k), 1.0, 0.0), axis=0, keepdims=True)
    return vals, rank, cnt


def _select_fast(s0, s1):
    kk = PEER_TOPK
    tt = s0.shape[1]
    v0, rank0, cnt0 = _top_k_ranks(s0, kk)
    v1, rank1, cnt1 = _top_k_ranks(s1, kk)
    half = SUBLANES
    st1 = jnp.concatenate(v1, axis=0)
    st1_h = st1[0:half]
    riota = lax.broadcasted_iota(jnp.int32, (half, tt), 0)
    blocks = [v0[0] + st1]
    for a in range(1, half):
        blocks.append(jnp.where(riota < kk // (a + 1), v0[a] + st1_h, -jnp.inf))
    blocks.append(jnp.concatenate(v0[half:], axis=0) + v1[0])
    p = jnp.concatenate(blocks, axis=0)
    cur = p
    for _ in range(kk):
        m = jnp.max(cur, axis=0, keepdims=True)
        cur = jnp.where(cur == m, -jnp.inf, cur)
    picked = jnp.where(cur != p, 1.0, 0.0)
    mx = v0[0] + v1[0]
    zsum = jnp.sum(picked * jnp.exp(p - mx), axis=0, keepdims=True)
    cntm = jnp.sum(picked, axis=0, keepdims=True)
    n_rows = [jnp.sum(picked[0:kk], axis=0, keepdims=True)]
    for a in range(1, half):
        lo = kk + (a - 1) * half
        n_rows.append(jnp.sum(picked[lo:lo + half], axis=0, keepdims=True))
    last = kk + (half - 1) * half
    n_rows += [picked[last + r:last + r + 1] for r in range(kk - half)]
    ni = jnp.zeros((N_KEYS, tt), F32)
    for a in range(kk):
        ni = jnp.where(rank0 == float(a), n_rows[a], ni)
    want = float(kk)
    bad = jnp.where((cnt0 != want) | (cnt1 != want) | (cntm != want), 1.0, 0.0)
    return jnp.exp(s0 - v0[0]) / zsum, ni, rank1, jnp.exp(s1 - v1[0]), bad


def _peer_sel_kernel(h2_ref, wq_ref, kh_ref, a_ref, ni_ref, r1_ref, bv_ref):
    q = jnp.dot(h2_ref[...], wq_ref[...], preferred_element_type=F32).astype(BF16)
    tt = q.shape[0]
    hw = q.shape[1] // PEER_HEADS

    def head_scores(h):
        st_ = lax.dot_general(kh_ref[h], q[:, h * hw:(h + 1) * hw], NT_DIMS,
                              preferred_element_type=F32)
        return st_[0:N_KEYS], st_[N_KEYS:2 * N_KEYS]

    def store(h, a, ni, r1, bv):
        a_ref[h] = a
        ni_ref[h] = ni
        r1_ref[h] = r1.astype(BF16)
        bv_ref[h] = bv.astype(BF16)

    for h in range(PEER_HEADS):
        s0, s1 = head_scores(h)
        a, ni, r1, bv, bad = _select_fast(s0, s1)
        store(h, a, ni, r1, bv)

        @pl.when(jnp.max(bad) > 0.0)
        def _(h=h, s0=s0, s1=s1):
            store(h, *_select_exact(s0, s1))


def _peer_sel_call(h2, wq, kh):
    ntok, d = h2.shape
    nt = ntok // PEER_SEL_TILE
    shp = jax.ShapeDtypeStruct((PEER_HEADS, N_KEYS, ntok), F32)
    shp_b = jax.ShapeDtypeStruct((PEER_HEADS, N_KEYS, ntok), BF16)
    ospec = pl.BlockSpec((PEER_HEADS, N_KEYS, PEER_SEL_TILE), lambda i: (0, 0, i))
    return pl.pallas_call(
        _peer_sel_kernel,
        grid=(nt,),
        in_specs=[pl.BlockSpec((PEER_SEL_TILE, d), lambda i: (i, 0)),
                  pl.BlockSpec(wq.shape, lambda i: (0, 0)),
                  pl.BlockSpec(kh.shape, lambda i: (0, 0, 0))],
        out_specs=[ospec] * 4,
        out_shape=[shp, shp, shp_b, shp_b],
        compiler_params=_cparams(("parallel",)),
        name="peer_select",
    )(h2, wq, kh)


def _gelu(x):
    return 0.5 * x * (1.0 + lax.erf(x * (2.0 ** -0.5)))


def _peer_dense_kernel(h2t_ref, a_ref, ni_ref, r1_ref, bv_ref, u_ref, vt_ref, o_ref, acc_ref):
    c = pl.program_id(1)

    @pl.when(c == 0)
    def _():
        acc_ref[...] = jnp.zeros_like(acc_ref)

    tm = h2t_ref.shape[1]
    nib = u_ref.shape[0] // N_KEYS
    rep = N_KEYS // BF16_SUBLANES

    def rows(ref, h, ib):
        r = jnp.broadcast_to(ref[h, ib:ib + 1, :], (BF16_SUBLANES, tm)).astype(BF16)
        return jnp.concatenate([r] * rep, axis=0)

    st_ = jnp.dot(u_ref[...], h2t_ref[...], preferred_element_type=F32)
    blocks = []
    for ib in range(nib):
        g = None
        for h in range(PEER_HEADS):
            bvh = bv_ref[h]
            keep = r1_ref[h] < rows(ni_ref, h, ib)
            term = rows(a_ref, h, ib) * jnp.where(keep, bvh, jnp.zeros_like(bvh))
            g = term if g is None else g + term
        blocks.append(g * _gelu(st_[ib * N_KEYS:(ib + 1) * N_KEYS].astype(BF16)))
    wt = jnp.concatenate(blocks, axis=0)
    acc_ref[...] += jnp.dot(vt_ref[...], wt, preferred_element_type=F32)

    @pl.when(c == pl.num_programs(1) - 1)
    def _():
        o_ref[...] = acc_ref[...].T


def _peer_dense_call(h2t, a, ni, r1, bv, u, vt):
    d, ntok = h2t.shape
    ne = u.shape[0]
    tm, ec = PEER_TM, PEER_EC
    ib = ec // N_KEYS
    rowspec = pl.BlockSpec((PEER_HEADS, ib, tm), lambda t, c: (0, c, t))
    colspec = pl.BlockSpec((PEER_HEADS, N_KEYS, tm), lambda t, c: (0, 0, t))
    return pl.pallas_call(
        _peer_dense_kernel,
        grid=(ntok // tm, ne // ec),
        in_specs=[pl.BlockSpec((d, tm), lambda t, c: (0, t)),
                  rowspec, rowspec, colspec, colspec,
                  pl.BlockSpec((ec, d), lambda t, c: (c, 0)),
                  pl.BlockSpec((d, ec), lambda t, c: (0, c))],
        out_specs=pl.BlockSpec((tm, d), lambda t, c: (t, 0)),
        out_shape=jax.ShapeDtypeStruct((ntok, d), F32),
        scratch_shapes=[pltpu.VMEM((d, tm), F32)],
        compiler_params=_cparams(("parallel", "arbitrary")),
        name="peer_dense",
    )(h2t, a, ni, r1, bv, u, vt)


def _final_kernel(x_ref, peer_ref, mod_ref, g_ref, o_ref):
    x = x_ref[...] + mod_ref[5:6, :] * peer_ref[...]
    o_ref[...] = _rms(x) * g_ref[...]


def _final_call(x, peer, mod, fg, n_ctx_tiles):
    b, t, d = x.shape
    nl = t // TILE - n_ctx_tiles
    tok = lambda bi, i: (bi, i + n_ctx_tiles, 0)
    return pl.pallas_call(
        _final_kernel,
        grid=(b, nl),
        in_specs=[pl.BlockSpec((None, TILE, d), tok), pl.BlockSpec((None, TILE, d), tok),
                  pl.BlockSpec((None, None, 6, d), lambda bi, i: (bi, 1, 0, 0)),
                  pl.BlockSpec(fg.shape, lambda bi, i: (0, 0))],
        out_specs=pl.BlockSpec((None, TILE, d), lambda bi, i: (bi, i, 0)),
        out_shape=jax.ShapeDtypeStruct((b, nl * TILE, d), F32),
        compiler_params=_cparams(("parallel", "parallel")),
        name="final_norm",
    )(x, peer, mod, fg)


def _rope_tables(seq, n_ctx):
    rows = seq // GRID_W
    axis_rot = DIFF_DH // 2
    row = jnp.repeat(jnp.arange(rows), GRID_W).astype(F32)
    col = jnp.tile(jnp.arange(GRID_W), rows).astype(F32)
    inv = ROPE_BASE ** (-jnp.arange(0, axis_rot, 2, dtype=F32) / axis_rot)
    ang = jnp.concatenate([row[:, None] * inv, col[:, None] * inv], axis=-1)
    cos = jnp.repeat(jnp.cos(ang), 2, axis=-1)
    sin = jnp.repeat(jnp.sin(ang), 2, axis=-1)
    even = (jnp.arange(DIFF_DH) % 2 == 0)[None, :]
    sa = jnp.where(even, -sin, 0.0)
    sb = jnp.where(even, 0.0, sin)
    rep = LANES // DIFF_DH

    def full(tab, ctx_val):
        tab = jnp.tile(tab, (1, rep))
        return jnp.concatenate([jnp.full((n_ctx, LANES), ctx_val, F32), tab], axis=0)

    return full(cos, 1.0), full(sa, 0.0), full(sb, 0.0)


def kernel(x, c, ctx, c_ctx, ada_w, ada_b, norm1_g, norm2_g, w_in, conv_w, conv_b, conv_ln_g, conv_ln_b, mlstm_gate_b, mlstm_norm_g, diff_lambda, diff_norm_g, w_out, peer_wq, peer_keys, peer_u, peer_v, final_g):
    b, seq, d = x.shape
    n_ctx = ctx.shape[1]
    depth = ada_w.shape[0]
    cw, mw, dw = d // 4, d // 4, d // 2
    ng = 4 * MLSTM_HEADS
    assert n_ctx == TILE and seq % ATT_TQ == 0 and seq % GRID_W == 0
    assert w_in.shape[-1] == 2 * cw + 4 * mw + ng + 3 * dw
    assert peer_keys.shape[1:] == (PEER_HEADS, 2, N_KEYS, d // PEER_HEADS // 2)
    assert conv_w.shape[1] == CONV_K and (b * (seq + n_ctx)) % PEER_TM == 0

    rows = -(-(b + 1) // SUBLANES) * SUBLANES
    cvec = jnp.zeros((rows, d), F32).at[:b].set(c).at[b].set(c_ctx)
    mods = _ada_call(cvec, ada_w, ada_b)
    mod_lat = mods[:, :b].reshape(depth, b, 1, 6, d)
    mod_ctx = jnp.broadcast_to(mods[:, b].reshape(depth, 1, 1, 6, d), (depth, b, 1, 6, d))
    mod_all = jnp.concatenate([mod_ctx, mod_lat], axis=2)

    cos, sa, sb = _rope_tables(seq, n_ctx)
    g0 = 2 * cw + 4 * mw
    w_main = jnp.concatenate([w_in[:, :, :g0], w_in[:, :, g0 + ng:g0 + ng + 2 * dw], w_in[:, :, g0:g0 + ng],
                              jnp.zeros((depth, d, LANES - ng), F32)], axis=-1).astype(BF16)
    w_dvt = jnp.swapaxes(w_in[:, :, g0 + ng + 2 * dw:], 1, 2).astype(BF16)
    gbias = jnp.pad(mlstm_gate_b.reshape(depth, 1, ng), ((0, 0), (0, 0), (0, LANES - ng)))
    w_out_b = w_out.astype(BF16)
    wq_b = peer_wq.astype(BF16)
    hw = d // PEER_HEADS
    kz = jnp.zeros((depth, PEER_HEADS, N_KEYS, hw // 2), F32)
    kh = jnp.concatenate([jnp.concatenate([peer_keys[:, :, 0], kz], axis=-1),
                          jnp.concatenate([kz, peer_keys[:, :, 1]], axis=-1)], axis=2).astype(BF16)
    u_b = peer_u.astype(BF16)
    vt_b = jnp.swapaxes(peer_v, 1, 2).astype(BF16)
    mng = jnp.tile(mlstm_norm_g, (1, MLSTM_HEADS))

    xs = jnp.concatenate([ctx, x], axis=1)
    t = xs.shape[1]
    peer = None
    for l in range(depth):
        lam_init = 0.8 - 0.6 * math.exp(-0.3 * l)
        xs, u, mqkv, mo, gcol, grow, dq, dk, dvt = _inproj_call(
            xs, peer, mod_all[l - 1] if l else None, mod_all[l], norm1_g[l][None], w_main[l],
            w_dvt[l], gbias[l], ng, cos, sa, sb)
        conv = _conv_call(u, conv_w[l, :, 0, :], conv_b[l][None], conv_ln_g[l][None], conv_ln_b[l][None])
        gcol_d = gcol.reshape(b, t, 2, ng // 2).transpose(0, 2, 1, 3)
        grow_d = grow.reshape(b, 2, ng // 2, t)
        hf, hb = _mlstm_call(mqkv, gcol_d, grow_d)
        dyc = _attn_call(dq, dk, dvt, diff_lambda[l], diff_norm_g[l][:, None], lam_init, ctx=True)
        dyl = _attn_call(dq, dk, dvt, diff_lambda[l], diff_norm_g[l][:, None], lam_init, ctx=False)
        xs, h2, h2t = _outproj_call(xs, conv, hf, hb, mo, dyc, dyl, mod_all[l], mng[l][None], norm2_g[l][None],
                                    w_out_b[l])
        h2f = h2.reshape(b * t, d)
        a, ni, r1, bv = _peer_sel_call(h2f, wq_b[l], kh[l])
        peer = _peer_dense_call(h2t, a, ni, r1, bv, u_b[l], vt_b[l]).reshape(b, t, d)
    return _final_call(xs, peer, mod_all[depth - 1], final_g[None], n_ctx // TILE)
```

```python
import functools
import math

import jax
import jax.numpy as jnp
from jax import lax
from jax.experimental import pallas as pl
from jax.experimental.pallas import tpu as pltpu

F32 = jnp.float32
BF16 = jnp.bfloat16
HIGHEST = lax.Precision.HIGHEST

GRID_W = 64
EPS = 1e-6
CONV_K = 31
MLSTM_DH = 64
MLSTM_HEADS = 4
DIFF_DH = 64
DIFF_DV = 128
DIFF_HEADS = 4
ROPE_BASE = 10000.0
PEER_HEADS = 8
N_KEYS = 128
PEER_TOPK = 16

LANES = 128
SUBLANES = 8
TILE = 256
CONV_HALO = 16
ATT_KC = 1024
ATT_TQ = 512
PEER_SEL_TILE = 256
PEER_TM = 1024
BF16_SUBLANES = 16
PEER_EC = 1024
VMEM_LIMIT = 56 * 1024 * 1024

NT_DIMS = (((1,), (1,)), ((), ()))


def _cparams(sem):
    return pltpu.CompilerParams(dimension_semantics=sem, vmem_limit_bytes=VMEM_LIMIT)


def _rms(x, eps=EPS):
    return x * lax.rsqrt(jnp.mean(x * x, axis=-1, keepdims=True) + eps)


def _log_sigmoid(x):
    return jnp.minimum(x, 0.0) - jnp.log(1.0 + jnp.exp(-jnp.abs(x)))


def _ada_kernel(c_ref, w_ref, b_ref, o_ref):
    c = c_ref[...]
    s = (c * jax.nn.sigmoid(c)).astype(BF16)
    o_ref[...] = jnp.dot(s, w_ref[...].astype(BF16), preferred_element_type=F32) + b_ref[...]


def _ada_call(cvec, ada_w, ada_b):
    depth, d, n = ada_w.shape
    tn = 1536
    rows = cvec.shape[0]
    return pl.pallas_call(
        _ada_kernel,
        grid=(depth, n // tn),
        in_specs=[pl.BlockSpec((rows, d), lambda l, j: (0, 0)),
                  pl.BlockSpec((None, d, tn), lambda l, j: (l, 0, j)),
                  pl.BlockSpec((None, 1, tn), lambda l, j: (l, 0, j))],
        out_specs=pl.BlockSpec((None, rows, tn), lambda l, j: (l, 0, j)),
        out_shape=jax.ShapeDtypeStruct((depth, rows, n), F32),
        compiler_params=_cparams(("parallel", "parallel")),
        name="ada_mod",
    )(cvec, ada_w, ada_b.reshape(depth, 1, n))


def _rope(t, c, sa, sb):
    w = t.shape[1]
    rep = w // LANES
    c, sa, sb = (jnp.concatenate([z] * rep, axis=1) for z in (c, sa, sb))
    return t * c + pltpu.roll(t, w - 1, 1) * sa + pltpu.roll(t, 1, 1) * sb


def _inproj_kernel(*refs, has_peer, cw, mw, bb):
    if has_peer:
        x_ref, peer_ref, modp_ref = refs[:3]
        refs = refs[3:]
    (mod_ref, n1g_ref, wm_ref, wdvt_ref, gb_ref, cos_ref, sa_ref, sb_ref,
     xo_ref, u_ref, mqkv_ref, mo_ref, gcol_ref, grow_ref, dq_ref, dk_ref, dvt_ref) = refs[-17:]
    if not has_peer:
        x_ref = refs[0]
    hs = []
    for k in range(bb):
        x = x_ref[k]
        if has_peer:
            x = x + modp_ref[k, 5:6, :] * peer_ref[k]
        xo_ref[k] = x
        hs.append(_rms(x) * n1g_ref[...] * (1.0 + mod_ref[k, 1:2, :]) + mod_ref[k, 0:1, :])
    hb = jnp.concatenate(hs, axis=0).astype(BF16)

    def proj(lo, hi):
        return jnp.dot(hb, wm_ref[:, lo:hi], preferred_element_type=F32)

    def rows(z, k):
        return z[k * TILE:(k + 1) * TILE]

    a = proj(0, 2 * cw)
    u = a[:, :cw] * jax.nn.sigmoid(a[:, cw:])
    o = 2 * cw
    mq = proj(o, o + mw).astype(BF16)
    mk = (proj(o + mw, o + 2 * mw) * (MLSTM_DH ** -0.5)).astype(BF16)
    mv = proj(o + 2 * mw, o + 3 * mw).astype(BF16)
    mo = proj(o + 3 * mw, o + 4 * mw)
    o = o + 4 * mw
    dw = dq_ref.shape[2]
    c, sa, sb = cos_ref[...], sa_ref[...], sb_ref[...]
    pq = proj(o, o + dw)
    pk = proj(o + dw, o + 2 * dw)
    dvt = lax.dot_general(wdvt_ref[...], hb, NT_DIMS, preferred_element_type=F32).astype(BF16)
    o = o + 2 * dw
    ng = gcol_ref.shape[2]
    g = proj(o, o + LANES) + gb_ref[...]
    cidx = lax.broadcasted_iota(jnp.int32, g.shape, 1)
    g = jnp.where((cidx // MLSTM_HEADS) % 2 == 1, _log_sigmoid(g), g)
    for k in range(bb):
        u_ref[k] = rows(u, k)
        mqkv_ref[k, :, 0:mw] = rows(mq, k)
        mqkv_ref[k, :, mw:2 * mw] = rows(mk, k)
        mqkv_ref[k, :, 2 * mw:3 * mw] = rows(mv, k)
        mo_ref[k] = rows(mo, k)
        dq_ref[k] = (_rope(rows(pq, k), c, sa, sb) * (DIFF_DH ** -0.5 * math.log2(math.e))).astype(BF16)
        dk_ref[k] = _rope(rows(pk, k), c, sa, sb).astype(BF16)
        dvt_ref[k] = dvt[:, k * TILE:(k + 1) * TILE]
        gk = rows(g, k)
        gcol_ref[k] = gk[:, :ng]
        grow_ref[k] = gk.T[:ng, :]


def _batch_block(b):
    return 2 if b % 2 == 0 else 1


def _inproj_call(x, peer, modp, mod, n1g, wm, wdvt, gb, ng, cos, sa, sb):
    b, t, d = x.shape
    nt = t // TILE
    bb = _batch_block(b)
    cw = d // 4
    mw = d // 4
    dw = d // 2
    has_peer = peer is not None
    tok = lambda bi, i: (bi, i, 0)
    modspec = pl.BlockSpec((bb, None, 6, d), lambda bi, i: (bi, jnp.minimum(i, 1), 0, 0))
    full2 = lambda arr: pl.BlockSpec(arr.shape, lambda bi, i: (0, 0))
    in_specs = [pl.BlockSpec((bb, TILE, d), tok)]
    args = [x]
    if has_peer:
        in_specs += [pl.BlockSpec((bb, TILE, d), tok), modspec]
        args += [peer, modp]
    in_specs += [modspec, full2(n1g), full2(wm), full2(wdvt), full2(gb)]
    args += [mod, n1g, wm, wdvt, gb]
    in_specs += [pl.BlockSpec((TILE, LANES), lambda bi, i: (i, 0))] * 3
    args += [cos, sa, sb]
    out_shape = [jax.ShapeDtypeStruct((b, t, d), F32),
                 jax.ShapeDtypeStruct((b, t, cw), F32),
                 jax.ShapeDtypeStruct((b, t, 3 * mw), BF16),
                 jax.ShapeDtypeStruct((b, t, mw), F32),
                 jax.ShapeDtypeStruct((b, t, ng), F32),
                 jax.ShapeDtypeStruct((b, ng, t), F32),
                 jax.ShapeDtypeStruct((b, t, dw), BF16),
                 jax.ShapeDtypeStruct((b, t, dw), BF16),
                 jax.ShapeDtypeStruct((b, dw, t), BF16)]
    out_specs = [pl.BlockSpec((bb, TILE, d), tok),
                 pl.BlockSpec((bb, TILE, cw), tok),
                 pl.BlockSpec((bb, TILE, 3 * mw), tok),
                 pl.BlockSpec((bb, TILE, mw), tok),
                 pl.BlockSpec((bb, TILE, ng), tok),
                 pl.BlockSpec((bb, ng, TILE), lambda bi, i: (bi, 0, i)),
                 pl.BlockSpec((bb, TILE, dw), tok),
                 pl.BlockSpec((bb, TILE, dw), tok),
                 pl.BlockSpec((bb, dw, TILE), lambda bi, i: (bi, 0, i))]
    return pl.pallas_call(
        functools.partial(_inproj_kernel, has_peer=has_peer, cw=cw, mw=mw, bb=bb),
        grid=(b // bb, nt), in_specs=in_specs, out_specs=out_specs, out_shape=out_shape,
        compiler_params=_cparams(("parallel", "parallel")),
        name="in_proj",
    )(*args)


def _conv_kernel(up_ref, uc_ref, un_ref, w_ref, b_ref, lg_ref, lb_ref, o_ref, ext_ref):
    i = pl.program_id(1)
    nt = pl.num_programs(1)
    lm = jnp.where(i >= 2, 1.0, 0.0)
    rm = jnp.where(jnp.logical_and(i >= 1, i < nt - 1), 1.0, 0.0)
    hl = CONV_HALO
    ext_ref[0:hl, :] = up_ref[TILE - hl:TILE, :] * lm
    ext_ref[hl:hl + TILE, :] = uc_ref[...]
    ext_ref[hl + TILE:2 * hl + TILE, :] = un_ref[0:hl, :] * rm
    off = hl - CONV_K // 2
    acc = jnp.zeros(uc_ref.shape, F32)
    for k in range(CONV_K):
        acc = acc + w_ref[k:k + 1, :] * ext_ref[off + k:off + k + TILE, :]
    y = acc + b_ref[...]
    mu = jnp.mean(y, axis=-1, keepdims=True)
    yc = y - mu
    var = jnp.mean(yc * yc, axis=-1, keepdims=True)
    z = yc * lax.rsqrt(var + EPS) * lg_ref[...] + lb_ref[...]
    o_ref[...] = (z * jax.nn.sigmoid(z)).astype(o_ref.dtype)


def _conv_call(u, w, bias, lg, lb):
    b, t, cw = u.shape
    nt = t // TILE
    full2 = lambda arr: pl.BlockSpec(arr.shape, lambda bi, i: (0, 0))
    return pl.pallas_call(
        _conv_kernel,
        grid=(b, nt),
        in_specs=[pl.BlockSpec((None, TILE, cw), lambda bi, i: (bi, jnp.maximum(i - 1, 0), 0)),
                  pl.BlockSpec((None, TILE, cw), lambda bi, i: (bi, i, 0)),
                  pl.BlockSpec((None, TILE, cw), lambda bi, i: (bi, jnp.minimum(i + 1, nt - 1), 0)),
                  full2(w), full2(bias), full2(lg), full2(lb)],
        out_specs=pl.BlockSpec((None, TILE, cw), lambda bi, i: (bi, i, 0)),
        out_shape=jax.ShapeDtypeStruct((b, t, cw), BF16),
        scratch_shapes=[pltpu.VMEM((TILE + 2 * CONV_HALO, cw), F32)],
        compiler_params=_cparams(("parallel", "parallel")),
        name="conv_module",
    )(u, u, u, w, bias, lg, lb)


def _mlstm_chunk(qkv_ref, gc_ref, gr_ref, h_ref, c_ref, n_ref, m_ref, fwd):
    tc = qkv_ref.shape[0]
    mw = qkv_ref.shape[1] // 3
    nh = MLSTM_HEADS
    row = lax.broadcasted_iota(jnp.int32, (tc, tc), 0)
    col = lax.broadcasted_iota(jnp.int32, (tc, tc), 1)
    tri = row >= col if fwd else row <= col
    trif = tri.astype(F32)
    gc = gc_ref[...]
    gr = gr_ref[...]
    bcol = jnp.dot(trif, gc, precision=HIGHEST, preferred_element_type=F32)
    brow = lax.dot_general(gr, trif, NT_DIMS, precision=HIGHEST, preferred_element_type=F32)
    bl = jnp.sum(gr, axis=1, keepdims=True)

    q = qkv_ref[:, 0:mw]
    k = qkv_ref[:, mw:2 * mw]
    v = qkv_ref[:, 2 * mw:3 * mw]
    lane_head = lax.broadcasted_iota(jnp.int32, (tc, mw), 1) // MLSTM_DH
    cb = c_ref[...]
    inter_c = lax.dot_general(q, cb.astype(BF16), NT_DIMS, preferred_element_type=F32)
    qf = q.astype(F32)
    qn = qf * n_ref[...]

    out = jnp.zeros((tc, mw), F32)
    wkfull = jnp.zeros((tc, mw), F32)
    rhead = lax.broadcasted_iota(jnp.int32, (mw, 1), 0) // MLSTM_DH
    chead = lax.broadcasted_iota(jnp.int32, (1, mw), 1) // MLSTM_DH
    decay_col = jnp.zeros((mw, 1), F32)
    decay_row = jnp.zeros((1, mw), F32)
    for h in range(nh):
        mh = m_ref[h:h + 1, 0:1]
        bc = bcol[:, nh + h:nh + h + 1]
        br = brow[nh + h:nh + h + 1, :]
        ir = gr[h:h + 1, :]
        dlog = jnp.where(tri, bc - br + ir, -jnp.inf)
        inter = bc + mh
        mt = jnp.maximum(inter, jnp.max(dlog, axis=1, keepdims=True))
        dwt = jnp.exp(dlog - mt)
        iw = jnp.exp(inter - mt)
        hm = lane_head == h
        qh = jnp.where(hm, qf, 0.0).astype(BF16)
        s = lax.dot_general(qh, k, NT_DIMS, preferred_element_type=F32) * dwt
        sv = jnp.dot(s.astype(BF16), v, preferred_element_type=F32)
        qn_h = jnp.sum(jnp.where(hm, qn, 0.0), axis=1, keepdims=True)
        den = jnp.sum(s, axis=1, keepdims=True) + iw * qn_h
        denom = jnp.maximum(jnp.abs(den), jnp.exp(-mt))
        out = jnp.where(hm, (sv + iw * inter_c) / denom, out)
        blh = bl[nh + h:nh + h + 1, :]
        wlog_r = blh - br + ir
        mn = jnp.maximum(blh + mh, jnp.max(wlog_r, axis=1, keepdims=True))
        decay = jnp.exp(blh + mh - mn)
        wk_c = jnp.exp(blh - bc + gc[:, h:h + 1] - mn)
        wkfull = jnp.where(hm, wk_c, wkfull)
        decay_col = jnp.where(rhead == h, decay, decay_col)
        decay_row = jnp.where(chead == h, decay, decay_row)
        m_ref[h:h + 1, :] = jnp.broadcast_to(mn, (1, m_ref.shape[1]))
    h_ref[...] = out

    vw = v.astype(F32) * wkfull
    upd = jnp.dot(vw.T.astype(BF16), k, preferred_element_type=F32)
    c_ref[...] = decay_col * cb + jnp.where(rhead == chead, upd, 0.0)
    n_ref[...] = decay_row * n_ref[...] + jnp.sum(k.astype(F32) * wkfull, axis=0, keepdims=True)


def _mlstm_kernel(qkvf_ref, gcf_ref, grf_ref, qkvb_ref, gcb_ref, grb_ref, hf_ref, hb_ref,
                  c_ref, n_ref, m_ref):
    @pl.when(pl.program_id(1) == 0)
    def _():
        c_ref[...] = jnp.zeros_like(c_ref)
        n_ref[...] = jnp.zeros_like(n_ref)
        m_ref[...] = jnp.zeros_like(m_ref)

    _mlstm_chunk(qkvf_ref, gcf_ref, grf_ref, hf_ref, c_ref.at[0], n_ref.at[0], m_ref.at[0], True)
    _mlstm_chunk(qkvb_ref, gcb_ref, grb_ref, hb_ref, c_ref.at[1], n_ref.at[1], m_ref.at[1], False)


def _mlstm_call(mqkv, gcol, grow):
    b, t, w3 = mqkv.shape
    mw = w3 // 3
    nt = t // TILE
    ng = gcol.shape[-1]

    def rev(j):
        return jnp.where(j == 0, 0, nt - j)

    hshape = jax.ShapeDtypeStruct((b, t, mw), F32)
    return pl.pallas_call(
        _mlstm_kernel,
        grid=(b, nt),
        in_specs=[pl.BlockSpec((None, TILE, w3), lambda bi, j: (bi, j, 0)),
                  pl.BlockSpec((None, None, TILE, ng), lambda bi, j: (bi, 0, j, 0)),
                  pl.BlockSpec((None, None, ng, TILE), lambda bi, j: (bi, 0, 0, j)),
                  pl.BlockSpec((None, TILE, w3), lambda bi, j: (bi, rev(j), 0)),
                  pl.BlockSpec((None, None, TILE, ng), lambda bi, j: (bi, 1, rev(j), 0)),
                  pl.BlockSpec((None, None, ng, TILE), lambda bi, j: (bi, 1, 0, rev(j)))],
        out_specs=[pl.BlockSpec((None, TILE, mw), lambda bi, j: (bi, j, 0)),
                   pl.BlockSpec((None, TILE, mw), lambda bi, j: (bi, rev(j), 0))],
        out_shape=[hshape, hshape],
        scratch_shapes=[pltpu.VMEM((2, mw, mw), F32), pltpu.VMEM((2, 1, mw), F32),
                        pltpu.VMEM((2, SUBLANES, LANES), F32)],
        compiler_params=_cparams(("arbitrary", "arbitrary")),
        name="mlstm_scan",
    )(mqkv, gcol, grow, mqkv, gcol, grow)


def _attn_kernel(*refs, lam_init, kc, nq):
    dl_ref, ng_ref = refs[:2]
    q_refs = refs[2:2 + nq]
    k_ref, vt_ref, o_ref, s_ref, m_ref, l_ref, acc_ref = refs[2 + nq:]
    n_keys = k_ref.shape[0]
    q = jnp.concatenate([r[...] for r in q_refs], axis=0) if nq > 1 else q_refs[0][...]
    lane = lax.broadcasted_iota(jnp.int32, q.shape, 1)
    zero = jnp.zeros_like(q)
    qz = (jnp.where(lane < DIFF_DH, q, zero), jnp.where(lane >= DIFF_DH, q, zero))
    m_ref[...] = jnp.full(m_ref.shape, -jnp.inf, F32)
    l_ref[...] = jnp.zeros_like(l_ref)
    acc_ref[...] = jnp.zeros_like(acc_ref)

    def scores(start, slot):
        kb = k_ref[pl.ds(start, kc), :]
        for c in range(2):
            s_ref[slot, c] = lax.dot_general(kb, qz[c], NT_DIMS, preferred_element_type=F32)

    def softmax_pv(start, slot):
        vtb = vt_ref[:, pl.ds(start, kc)]
        for c in range(2):
            s = s_ref[slot, c]
            m_old = m_ref[c]
            m_new = jnp.maximum(m_old, jnp.max(s, axis=0, keepdims=True))
            alpha = jnp.exp2(m_old - m_new)
            p = jnp.exp2(s - m_new)
            l_ref[c] = alpha * l_ref[c] + jnp.sum(p, axis=0, keepdims=True)
            acc_ref[c] = alpha * acc_ref[c] + jnp.dot(vtb, p.astype(BF16), preferred_element_type=F32)
            m_ref[c] = m_new

    n = n_keys // kc
    pairs = (n - 1) // 2
    scores(0, 0)
    if pairs:
        def body(jj, carry):
            base = pl.multiple_of(jj * (2 * kc), TILE)
            scores(base + kc, 1)
            softmax_pv(base, 0)
            scores(base + 2 * kc, 0)
            softmax_pv(base + kc, 1)
            return carry

        lax.fori_loop(0, pairs, body, 0)
    done = 2 * pairs
    if n - done == 2:
        scores((done + 1) * kc, 1)
    softmax_pv(done * kc, 0)
    if n - done == 2:
        softmax_pv((done + 1) * kc, 1)

    dl = dl_ref[...]
    lam = (jnp.exp(jnp.sum(dl[0:1] * dl[1:2], axis=1, keepdims=True))
           - jnp.exp(jnp.sum(dl[2:3] * dl[3:4], axis=1, keepdims=True)) + lam_init)
    ot = acc_ref[0] / l_ref[0] - lam * (acc_ref[1] / l_ref[1])
    ms = jnp.mean(ot * ot, axis=0, keepdims=True)
    y = ot * lax.rsqrt(ms + EPS) * (ng_ref[...] * (1.0 - lam_init))
    o_ref[...] = y.T.astype(o_ref.dtype)


def _attn_key_chunk(t):
    return max(k for k in range(TILE, ATT_KC + 1, TILE) if t % k == 0)


def _attn_call(dq, dk, dvt, dlam, dng, lam_init, *, ctx):
    b, t, w = dq.shape
    nh = w // DIFF_DV
    if ctx:
        n_keys, nq, steps, q0 = TILE, 1, 1, 0
    else:
        n_keys, nq, q0 = t, ATT_TQ // TILE, 1
        steps = (t - TILE) // ATT_TQ
    tq = nq * TILE
    kc = _attn_key_chunk(n_keys)
    kern = functools.partial(_attn_kernel, lam_init=lam_init, kc=kc, nq=nq)
    q_specs = [pl.BlockSpec((None, TILE, DIFF_DV), functools.partial(
        lambda bi, h, i, j: (bi, q0 + nq * i + j, h), j=j)) for j in range(nq)]
    return pl.pallas_call(
        kern,
        grid=(b, nh, steps),
        in_specs=[pl.BlockSpec(dlam.shape, lambda bi, h, i: (0, 0)),
                  pl.BlockSpec(dng.shape, lambda bi, h, i: (0, 0))] + q_specs + [
                  pl.BlockSpec((None, n_keys, DIFF_DV), lambda bi, h, i: (bi, 0, h)),
                  pl.BlockSpec((None, DIFF_DV, n_keys), lambda bi, h, i: (bi, h, 0))],
        out_specs=pl.BlockSpec((None, tq, DIFF_DV), lambda bi, h, i: (bi, i, h)),
        out_shape=jax.ShapeDtypeStruct((b, steps * tq, w), BF16),
        scratch_shapes=[pltpu.VMEM((2, 2, kc, tq), F32),
                        pltpu.VMEM((2, 1, tq), F32), pltpu.VMEM((2, 1, tq), F32),
                        pltpu.VMEM((2, DIFF_DV, tq), F32)],
        compiler_params=_cparams(("parallel", "parallel", "arbitrary")),
        name="diff_attn_ctx" if ctx else "diff_attn",
    )(dlam, dng, *([dq] * nq), dk, dvt)


def _outproj_kernel(x_ref, conv_ref, hf_ref, hb_ref, mo_ref, dyc_ref, dyl_ref, mod_ref, mng_ref, n2g_ref,
                    wo_ref, xo_ref, h2_ref, h2t_ref):
    cw = conv_ref.shape[1]
    mw = hf_ref.shape[1]
    hh = hf_ref[...] + hb_ref[...]
    r = lax.broadcasted_iota(jnp.int32, (mw, mw), 0) // MLSTM_DH
    c = lax.broadcasted_iota(jnp.int32, (mw, mw), 1) // MLSTM_DH
    gm = jnp.where(r == c, 1.0 / MLSTM_DH, 0.0)
    ms = jnp.dot(hh * hh, gm, precision=HIGHEST, preferred_element_type=F32)
    ym = hh * lax.rsqrt(ms + EPS) * mng_ref[...] * jax.nn.sigmoid(mo_ref[...])
    dy = jnp.where(pl.program_id(1) == 0, dyc_ref[...], dyl_ref[...])
    o = (jnp.dot(conv_ref[...], wo_ref[0:cw, :], preferred_element_type=F32)
         + jnp.dot(ym.astype(BF16), wo_ref[cw:cw + mw, :], preferred_element_type=F32)
         + jnp.dot(dy, wo_ref[cw + mw:, :], preferred_element_type=F32))
    x = x_ref[...] + mod_ref[2:3, :] * o
    xo_ref[...] = x
    h2 = _rms(x) * n2g_ref[...] * (1.0 + mod_ref[4:5, :]) + mod_ref[3:4, :]
    h2_ref[...] = h2.astype(BF16)
    h2t_ref[...] = h2.T.astype(BF16)


def _outproj_call(x, conv, hf, hb, mo, dyc, dyl, mod, mng, n2g, wo):
    b, t, d = x.shape
    nt = t // TILE
    cw, mw, dw = conv.shape[-1], mo.shape[-1], dyl.shape[-1]
    tok = lambda bi, i: (bi, i, 0)
    full2 = lambda arr: pl.BlockSpec(arr.shape, lambda bi, i: (0, 0))
    return pl.pallas_call(
        _outproj_kernel,
        grid=(b, nt),
        in_specs=[pl.BlockSpec((None, TILE, d), tok),
                  pl.BlockSpec((None, TILE, cw), tok),
                  pl.BlockSpec((None, TILE, mw), tok),
                  pl.BlockSpec((None, TILE, mw), tok),
                  pl.BlockSpec((None, TILE, mw), tok),
                  pl.BlockSpec((None, TILE, dw), lambda bi, i: (bi, 0, 0)),
                  pl.BlockSpec((None, TILE, dw), lambda bi, i: (bi, jnp.maximum(i - 1, 0), 0)),
                  pl.BlockSpec((None, None, 6, d), lambda bi, i: (bi, jnp.minimum(i, 1), 0, 0)),
                  full2(mng), full2(n2g), full2(wo)],
        out_specs=[pl.BlockSpec((None, TILE, d), tok), pl.BlockSpec((None, TILE, d), tok),
                   pl.BlockSpec((d, TILE), lambda bi, i: (0, bi * nt + i))],
        out_shape=[jax.ShapeDtypeStruct((b, t, d), F32), jax.ShapeDtypeStruct((b, t, d), BF16),
                   jax.ShapeDtypeStruct((d, b * t), BF16)],
        compiler_params=_cparams(("parallel", "parallel")),
        name="out_proj",
    )(x, conv, hf, hb, mo, dyc, dyl, mod, mng, n2g, wo)


def _top_k_rows(s, k):
    idx = lax.broadcasted_iota(jnp.int32, s.shape, 0).astype(F32)
    vals, ids = [], []
    cur = s
    for _ in range(k):
        m = jnp.max(cur, axis=0, keepdims=True)
        am = jnp.min(jnp.where(cur == m, idx, float(s.shape[0])), axis=0, keepdims=True)
        vals.append(m)
        ids.append(am)
        cur = jnp.where(idx == am, -jnp.inf, cur)
    return vals, ids


def _select_exact(s0, s1):
    kk = PEER_TOPK
    tt = s0.shape[1]
    iota_k = lax.broadcasted_iota(jnp.int32, (kk, tt), 0).astype(F32)
    iota_n = lax.broadcasted_iota(jnp.int32, (N_KEYS, tt), 0).astype(F32)
    v0, i0 = _top_k_rows(s0, kk)
    v1, i1 = _top_k_rows(s1, kk)
    st0 = jnp.concatenate(v0, axis=0)
    n = jnp.zeros((kk, tt), F32)
    hv = st0 + v1[0]
    mx = v0[0] + v1[0]
    zsum = jnp.zeros((1, tt), F32)
    for _ in range(kk):
        m = jnp.max(hv, axis=0, keepdims=True)
        a_star = jnp.min(jnp.where(hv == m, iota_k, float(kk)), axis=0, keepdims=True)
        sel = iota_k == a_star
        zsum = zsum + jnp.exp(m - mx)
        n = n + jnp.where(sel, 1.0, 0.0)
        nxt = jnp.full((kk, tt), -jnp.inf, F32)
        for b in range(1, kk):
            nxt = jnp.where(n == float(b), v1[b], nxt)
        hv = jnp.where(sel, st0 + nxt, hv)
    ni = jnp.zeros((N_KEYS, tt), F32)
    r1 = jnp.full((N_KEYS, tt), float(kk), F32)
    for a in range(kk):
        ni = jnp.where(iota_n == i0[a], n[a:a + 1, :], ni)
        r1 = jnp.where(iota_n == i1[a], float(a), r1)
    return jnp.exp(s0 - v0[0]) * (0.5 / zsum), ni, r1, jnp.exp(s1 - v1[0])


def _top_k_ranks(s, k):
    cur = s
    rank = jnp.full(s.shape, float(k), F32)
    vals = []
    for j in range(k):
        m = jnp.max(cur, axis=0, keepdims=True)
        eq = cur == m
        vals.append(m)
        rank = jnp.where(eq, float(j), rank)
        cur = jnp.where(eq, -jnp.inf, cur)
    cnt = jnp.sum(jnp.where(rank < float(k), 1.0, 0.0), axis=0, keepdims=True)
    return vals, rank, cnt


def _select_fast(s0, s1):
    kk = PEER_TOPK
    tt = s0.shape[1]
    v0, rank0, cnt0 = _top_k_ranks(s0, kk)
    v1, rank1, cnt1 = _top_k_ranks(s1, kk)
    half = SUBLANES
    st1 = jnp.concatenate(v1, axis=0)
    st1_h = st1[0:half]
    riota = lax.broadcasted_iota(jnp.int32, (half, tt), 0)
    blocks = [v0[0] + st1]
    for a in range(1, half):
        blocks.append(jnp.where(riota < kk // (a + 1), v0[a] + st1_h, -jnp.inf))
    blocks.append(jnp.concatenate(v0[half:], axis=0) + v1[0])
    p = jnp.concatenate(blocks, axis=0)
    cur = p
    for _ in range(kk):
        m = jnp.max(cur, axis=0, keepdims=True)
        cur = jnp.where(cur == m, -jnp.inf, cur)
    picked = jnp.where(cur != p, 1.0, 0.0)
    mx = v0[0] + v1[0]
    zsum = jnp.sum(picked * jnp.exp(p - mx), axis=0, keepdims=True)
    cntm = jnp.sum(picked, axis=0, keepdims=True)
    n_rows = [jnp.sum(picked[0:kk], axis=0, keepdims=True)]
    for a in range(1, half):
        lo = kk + (a - 1) * half
        n_rows.append(jnp.sum(picked[lo:lo + half], axis=0, keepdims=True))
    last = kk + (half - 1) * half
    n_rows += [picked[last + r:last + r + 1] for r in range(kk - half)]
    ni = jnp.zeros((N_KEYS, tt), F32)
    for a in range(kk):
        ni = jnp.where(rank0 == float(a), n_rows[a], ni)
    want = float(kk)
    bad = jnp.where((cnt0 != want) | (cnt1 != want) | (cntm != want), 1.0, 0.0)
    return jnp.exp(s0 - v0[0]) * (0.5 / zsum), ni, rank1, jnp.exp(s1 - v1[0]), bad


def _peer_sel_kernel(h2_ref, wq_ref, kh_ref, a_ref, ni_ref, r1_ref, bv_ref):
    q = jnp.dot(h2_ref[...], wq_ref[...], preferred_element_type=F32).astype(BF16)
    hw = q.shape[1] // PEER_HEADS

    def head_scores(h):
        st_ = lax.dot_general(kh_ref[h], q[:, h * hw:(h + 1) * hw], NT_DIMS,
                              preferred_element_type=F32)
        return st_[0:N_KEYS], st_[N_KEYS:2 * N_KEYS]

    def store(h, a, ni, r1, bv):
        a_ref[h] = a
        ni_ref[h] = ni
        r1_ref[h] = r1.astype(BF16)
        bv_ref[h] = bv.astype(BF16)

    for h in range(PEER_HEADS):
        s0, s1 = head_scores(h)
        a, ni, r1, bv, bad = _select_fast(s0, s1)
        store(h, a, ni, r1, bv)

        @pl.when(jnp.max(bad) > 0.0)
        def _(h=h, s0=s0, s1=s1):
            store(h, *_select_exact(s0, s1))


def _peer_sel_call(h2, wq, kh):
    ntok, d = h2.shape
    nt = ntok // PEER_SEL_TILE
    shp = jax.ShapeDtypeStruct((PEER_HEADS, N_KEYS, ntok), F32)
    shp_b = jax.ShapeDtypeStruct((PEER_HEADS, N_KEYS, ntok), BF16)
    ospec = pl.BlockSpec((PEER_HEADS, N_KEYS, PEER_SEL_TILE), lambda i: (0, 0, i))
    return pl.pallas_call(
        _peer_sel_kernel,
        grid=(nt,),
        in_specs=[pl.BlockSpec((PEER_SEL_TILE, d), lambda i: (i, 0)),
                  pl.BlockSpec(wq.shape, lambda i: (0, 0)),
                  pl.BlockSpec(kh.shape, lambda i: (0, 0, 0))],
        out_specs=[ospec] * 4,
        out_shape=[shp, shp, shp_b, shp_b],
        compiler_params=_cparams(("parallel",)),
        name="peer_select",
    )(h2, wq, kh)


def _gelu_x2(x):
    return x + x * lax.erf(x * (2.0 ** -0.5))


def _peer_dense_kernel(h2t_ref, a_ref, ni_ref, r1_ref, bv_ref, u_ref, vt_ref, o_ref, acc_ref):
    c = pl.program_id(1)

    @pl.when(c == 0)
    def _():
        acc_ref[...] = jnp.zeros_like(acc_ref)

    tm = h2t_ref.shape[1]
    nib = u_ref.shape[0] // N_KEYS
    rep = N_KEYS // BF16_SUBLANES

    def rows(ref, h, ib):
        r = jnp.broadcast_to(ref[h, ib:ib + 1, :], (BF16_SUBLANES, tm)).astype(BF16)
        return jnp.concatenate([r] * rep, axis=0)

    st_ = jnp.dot(u_ref[...], h2t_ref[...], preferred_element_type=F32)
    blocks = []
    for ib in range(nib):
        g = None
        for h in range(PEER_HEADS):
            bvh = bv_ref[h]
            keep = r1_ref[h] < rows(ni_ref, h, ib)
            term = rows(a_ref, h, ib) * jnp.where(keep, bvh, jnp.zeros_like(bvh))
            g = term if g is None else g + term
        blocks.append(g * _gelu_x2(st_[ib * N_KEYS:(ib + 1) * N_KEYS].astype(BF16)))
    wt = jnp.concatenate(blocks, axis=0)
    acc_ref[...] += jnp.dot(vt_ref[...], wt, preferred_element_type=F32)

    @pl.when(c == pl.num_programs(1) - 1)
    def _():
        o_ref[...] = acc_ref[...].T


def _peer_dense_call(h2t, a, ni, r1, bv, u, vt):
    d, ntok = h2t.shape
    ne = u.shape[0]
    tm, ec = PEER_TM, PEER_EC
    ib = ec // N_KEYS
    rowspec = pl.BlockSpec((PEER_HEADS, ib, tm), lambda t, c: (0, c, t))
    colspec = pl.BlockSpec((PEER_HEADS, N_KEYS, tm), lambda t, c: (0, 0, t))
    return pl.pallas_call(
        _peer_dense_kernel,
        grid=(ntok // tm, ne // ec),
        in_specs=[pl.BlockSpec((d, tm), lambda t, c: (0, t)),
                  rowspec, rowspec, colspec, colspec,
                  pl.BlockSpec((ec, d), lambda t, c: (c, 0)),
                  pl.BlockSpec((d, ec), lambda t, c: (0, c))],
        out_specs=pl.BlockSpec((tm, d), lambda t, c: (t, 0)),
        out_shape=jax.ShapeDtypeStruct((ntok, d), F32),
        scratch_shapes=[pltpu.VMEM((d, tm), F32)],
        compiler_params=_cparams(("parallel", "arbitrary")),
        name="peer_dense",
    )(h2t, a, ni, r1, bv, u, vt)


def _final_kernel(x_ref, peer_ref, mod_ref, g_ref, o_ref):
    x = x_ref[...] + mod_ref[5:6, :] * peer_ref[...]
    o_ref[...] = _rms(x) * g_ref[...]


def _final_call(x, peer, mod, fg, n_ctx_tiles):
    b, t, d = x.shape
    nl = t // TILE - n_ctx_tiles
    tok = lambda bi, i: (bi, i + n_ctx_tiles, 0)
    return pl.pallas_call(
        _final_kernel,
        grid=(b, nl),
        in_specs=[pl.BlockSpec((None, TILE, d), tok), pl.BlockSpec((None, TILE, d), tok),
                  pl.BlockSpec((None, None, 6, d), lambda bi, i: (bi, 1, 0, 0)),
                  pl.BlockSpec(fg.shape, lambda bi, i: (0, 0))],
        out_specs=pl.BlockSpec((None, TILE, d), lambda bi, i: (bi, i, 0)),
        out_shape=jax.ShapeDtypeStruct((b, nl * TILE, d), F32),
        compiler_params=_cparams(("parallel", "parallel")),
        name="final_norm",
    )(x, peer, mod, fg)


def _rope_tables(seq, n_ctx):
    rows = seq // GRID_W
    axis_rot = DIFF_DH // 2
    row = jnp.repeat(jnp.arange(rows), GRID_W).astype(F32)
    col = jnp.tile(jnp.arange(GRID_W), rows).astype(F32)
    inv = ROPE_BASE ** (-jnp.arange(0, axis_rot, 2, dtype=F32) / axis_rot)
    ang = jnp.concatenate([row[:, None] * inv, col[:, None] * inv], axis=-1)
    cos = jnp.repeat(jnp.cos(ang), 2, axis=-1)
    sin = jnp.repeat(jnp.sin(ang), 2, axis=-1)
    even = (jnp.arange(DIFF_DH) % 2 == 0)[None, :]
    sa = jnp.where(even, -sin, 0.0)
    sb = jnp.where(even, 0.0, sin)
    rep = LANES // DIFF_DH

    def full(tab, ctx_val):
        tab = jnp.tile(tab, (1, rep))
        return jnp.concatenate([jnp.full((n_ctx, LANES), ctx_val, F32), tab], axis=0)

    return full(cos, 1.0), full(sa, 0.0), full(sb, 0.0)


def kernel(x, c, ctx, c_ctx, ada_w, ada_b, norm1_g, norm2_g, w_in, conv_w, conv_b, conv_ln_g, conv_ln_b, mlstm_gate_b, mlstm_norm_g, diff_lambda, diff_norm_g, w_out, peer_wq, peer_keys, peer_u, peer_v, final_g):
    b, seq, d = x.shape
    n_ctx = ctx.shape[1]
    depth = ada_w.shape[0]
    cw, mw, dw = d // 4, d // 4, d // 2
    ng = 4 * MLSTM_HEADS
    assert n_ctx == TILE and seq % ATT_TQ == 0 and seq % GRID_W == 0
    assert w_in.shape[-1] == 2 * cw + 4 * mw + ng + 3 * dw
    assert peer_keys.shape[1:] == (PEER_HEADS, 2, N_KEYS, d // PEER_HEADS // 2)
    assert conv_w.shape[1] == CONV_K and (b * (seq + n_ctx)) % PEER_TM == 0

    rows = -(-(b + 1) // SUBLANES) * SUBLANES
    cvec = jnp.zeros((rows, d), F32).at[:b].set(c).at[b].set(c_ctx)
    mods = _ada_call(cvec, ada_w, ada_b)
    mod_lat = mods[:, :b].reshape(depth, b, 1, 6, d)
    mod_ctx = jnp.broadcast_to(mods[:, b].reshape(depth, 1, 1, 6, d), (depth, b, 1, 6, d))
    mod_all = jnp.concatenate([mod_ctx, mod_lat], axis=2)

    cos, sa, sb = _rope_tables(seq, n_ctx)
    g0 = 2 * cw + 4 * mw
    w_main = jnp.concatenate([w_in[:, :, :g0], w_in[:, :, g0 + ng:g0 + ng + 2 * dw], w_in[:, :, g0:g0 + ng],
                              jnp.zeros((depth, d, LANES - ng), F32)], axis=-1).astype(BF16)
    w_dvt = jnp.swapaxes(w_in[:, :, g0 + ng + 2 * dw:], 1, 2).astype(BF16)
    gbias = jnp.pad(mlstm_gate_b.reshape(depth, 1, ng), ((0, 0), (0, 0), (0, LANES - ng)))
    w_out_b = w_out.astype(BF16)
    wq_b = peer_wq.astype(BF16)
    hw = d // PEER_HEADS
    kz = jnp.zeros((depth, PEER_HEADS, N_KEYS, hw // 2), F32)
    kh = jnp.concatenate([jnp.concatenate([peer_keys[:, :, 0], kz], axis=-1),
                          jnp.concatenate([kz, peer_keys[:, :, 1]], axis=-1)], axis=2).astype(BF16)
    u_b = peer_u.astype(BF16)
    vt_b = jnp.swapaxes(peer_v, 1, 2).astype(BF16)
    mng = jnp.tile(mlstm_norm_g, (1, MLSTM_HEADS))

    xs = jnp.concatenate([ctx, x], axis=1)
    t = xs.shape[1]
    peer = None
    for l in range(depth):
        lam_init = 0.8 - 0.6 * math.exp(-0.3 * l)
        xs, u, mqkv, mo, gcol, grow, dq, dk, dvt = _inproj_call(
            xs, peer, mod_all[l - 1] if l else None, mod_all[l], norm1_g[l][None], w_main[l],
            w_dvt[l], gbias[l], ng, cos, sa, sb)
        conv = _conv_call(u, conv_w[l, :, 0, :], conv_b[l][None], conv_ln_g[l][None], conv_ln_b[l][None])
        gcol_d = gcol.reshape(b, t, 2, ng // 2).transpose(0, 2, 1, 3)
        grow_d = grow.reshape(b, 2, ng // 2, t)
        hf, hb = _mlstm_call(mqkv, gcol_d, grow_d)
        dyc = _attn_call(dq, dk, dvt, diff_lambda[l], diff_norm_g[l][:, None], lam_init, ctx=True)
        dyl = _attn_call(dq, dk, dvt, diff_lambda[l], diff_norm_g[l][:, None], lam_init, ctx=False)
        xs, h2, h2t = _outproj_call(xs, conv, hf, hb, mo, dyc, dyl, mod_all[l], mng[l][None], norm2_g[l][None],
                                    w_out_b[l])
        h2f = h2.reshape(b * t, d)
        a, ni, r1, bv = _peer_sel_call(h2f, wq_b[l], kh[l])
        peer = _peer_dense_call(h2t, a, ni, r1, bv, u_b[l], vt_b[l]).reshape(b, t, d)
    return _final_call(xs, peer, mod_all[depth - 1], final_g[None], n_ctx // TILE)
```

```python
import functools
import math

import jax
import jax.numpy as jnp
from jax import lax
from jax.experimental import pallas as pl
from jax.experimental.pallas import tpu as pltpu

F32 = jnp.float32
BF16 = jnp.bfloat16
HIGHEST = lax.Precision.HIGHEST

GRID_W = 64
EPS = 1e-6
CONV_K = 31
MLSTM_DH = 64
MLSTM_HEADS = 4
DIFF_DH = 64
DIFF_DV = 128
DIFF_HEADS = 4
ROPE_BASE = 10000.0
PEER_HEADS = 8
N_KEYS = 128
PEER_TOPK = 16

LANES = 128
SUBLANES = 8
TILE = 256
CONV_HALO = 16
ATT_KC = 1024
ATT_TQ = 512
PEER_SEL_TILE = 256
PEER_TM = 1024
BF16_SUBLANES = 16
PEER_EC = 1024
VMEM_LIMIT = 56 * 1024 * 1024

NT_DIMS = (((1,), (1,)), ((), ()))


def _cparams(sem):
    return pltpu.CompilerParams(dimension_semantics=sem, vmem_limit_bytes=VMEM_LIMIT)


def _rms(x, eps=EPS):
    return x * lax.rsqrt(jnp.mean(x * x, axis=-1, keepdims=True) + eps)


def _log_sigmoid(x):
    return jnp.minimum(x, 0.0) - jnp.log(1.0 + jnp.exp(-jnp.abs(x)))


def _ada_kernel(c_ref, w_ref, b_ref, o_ref):
    c = c_ref[...]
    s = (c * jax.nn.sigmoid(c)).astype(BF16)
    o_ref[...] = jnp.dot(s, w_ref[...].astype(BF16), preferred_element_type=F32) + b_ref[...]


def _ada_call(cvec, ada_w, ada_b):
    depth, d, n = ada_w.shape
    tn = 1536
    rows = cvec.shape[0]
    return pl.pallas_call(
        _ada_kernel,
        grid=(depth, n // tn),
        in_specs=[pl.BlockSpec((rows, d), lambda l, j: (0, 0)),
                  pl.BlockSpec((None, d, tn), lambda l, j: (l, 0, j)),
                  pl.BlockSpec((None, 1, tn), lambda l, j: (l, 0, j))],
        out_specs=pl.BlockSpec((None, rows, tn), lambda l, j: (l, 0, j)),
        out_shape=jax.ShapeDtypeStruct((depth, rows, n), F32),
        compiler_params=_cparams(("parallel", "parallel")),
        name="ada_mod",
    )(cvec, ada_w, ada_b.reshape(depth, 1, n))


def _rope(t, c, sa, sb):
    w = t.shape[1]
    rep = w // LANES
    c, sa, sb = (jnp.concatenate([z] * rep, axis=1) for z in (c, sa, sb))
    return t * c + pltpu.roll(t, w - 1, 1) * sa + pltpu.roll(t, 1, 1) * sb


def _inproj_kernel(*refs, has_peer, cw, mw, bb):
    if has_peer:
        x_ref, peer_ref, modp_ref = refs[:3]
        refs = refs[3:]
    (mod_ref, n1g_ref, wm_ref, wdvt_ref, gb_ref, cos_ref, sa_ref, sb_ref,
     xo_ref, u_ref, mqkv_ref, mo_ref, gcol_ref, grow_ref, dq_ref, dk_ref, dvt_ref) = refs[-17:]
    if not has_peer:
        x_ref = refs[0]
    hs = []
    for k in range(bb):
        x = x_ref[k]
        if has_peer:
            x = x + modp_ref[k, 5:6, :] * peer_ref[k]
        xo_ref[k] = x
        hs.append(_rms(x) * n1g_ref[...] * (1.0 + mod_ref[k, 1:2, :]) + mod_ref[k, 0:1, :])
    hb = jnp.concatenate(hs, axis=0).astype(BF16)

    def proj(lo, hi):
        return jnp.dot(hb, wm_ref[:, lo:hi], preferred_element_type=F32)

    def rows(z, k):
        return z[k * TILE:(k + 1) * TILE]

    a = proj(0, 2 * cw)
    u = a[:, :cw] * jax.nn.sigmoid(a[:, cw:])
    o = 2 * cw
    mq = proj(o, o + mw).astype(BF16)
    mk = (proj(o + mw, o + 2 * mw) * (MLSTM_DH ** -0.5)).astype(BF16)
    mv = proj(o + 2 * mw, o + 3 * mw).astype(BF16)
    mo = proj(o + 3 * mw, o + 4 * mw)
    o = o + 4 * mw
    dw = dq_ref.shape[2]
    c, sa, sb = cos_ref[...], sa_ref[...], sb_ref[...]
    pq = proj(o, o + dw)
    pk = proj(o + dw, o + 2 * dw)
    dvt = lax.dot_general(wdvt_ref[...], hb, NT_DIMS, preferred_element_type=F32).astype(BF16)
    o = o + 2 * dw
    ng = gcol_ref.shape[2]
    g = proj(o, o + LANES) + gb_ref[...]
    cidx = lax.broadcasted_iota(jnp.int32, g.shape, 1)
    g = jnp.where((cidx // MLSTM_HEADS) % 2 == 1, _log_sigmoid(g), g)
    for k in range(bb):
        u_ref[k] = rows(u, k)
        mqkv_ref[k, :, 0:mw] = rows(mq, k)
        mqkv_ref[k, :, mw:2 * mw] = rows(mk, k)
        mqkv_ref[k, :, 2 * mw:3 * mw] = rows(mv, k)
        mo_ref[k] = rows(mo, k)
        dq_ref[k] = (_rope(rows(pq, k), c, sa, sb) * (DIFF_DH ** -0.5 * math.log2(math.e))).astype(BF16)
        dk_ref[k] = _rope(rows(pk, k), c, sa, sb).astype(BF16)
        dvt_ref[k] = dvt[:, k * TILE:(k + 1) * TILE]
        gk = rows(g, k)
        gcol_ref[k] = gk[:, :ng]
        grow_ref[k] = gk.T[:ng, :]


def _batch_block(b):
    return 2 if b % 2 == 0 else 1


def _inproj_call(x, peer, modp, mod, n1g, wm, wdvt, gb, ng, cos, sa, sb):
    b, t, d = x.shape
    nt = t // TILE
    bb = _batch_block(b)
    cw = d // 4
    mw = d // 4
    dw = d // 2
    has_peer = peer is not None
    tok = lambda bi, i: (bi, i, 0)
    modspec = pl.BlockSpec((bb, None, 6, d), lambda bi, i: (bi, jnp.minimum(i, 1), 0, 0))
    full2 = lambda arr: pl.BlockSpec(arr.shape, lambda bi, i: (0, 0))
    in_specs = [pl.BlockSpec((bb, TILE, d), tok)]
    args = [x]
    if has_peer:
        in_specs += [pl.BlockSpec((bb, TILE, d), tok), modspec]
        args += [peer, modp]
    in_specs += [modspec, full2(n1g), full2(wm), full2(wdvt), full2(gb)]
    args += [mod, n1g, wm, wdvt, gb]
    in_specs += [pl.BlockSpec((TILE, LANES), lambda bi, i: (i, 0))] * 3
    args += [cos, sa, sb]
    out_shape = [jax.ShapeDtypeStruct((b, t, d), F32),
                 jax.ShapeDtypeStruct((b, t, cw), F32),
                 jax.ShapeDtypeStruct((b, t, 3 * mw), BF16),
                 jax.ShapeDtypeStruct((b, t, mw), F32),
                 jax.ShapeDtypeStruct((b, t, ng), F32),
                 jax.ShapeDtypeStruct((b, ng, t), F32),
                 jax.ShapeDtypeStruct((b, t, dw), BF16),
                 jax.ShapeDtypeStruct((b, t, dw), BF16),
                 jax.ShapeDtypeStruct((b, dw, t), BF16)]
    out_specs = [pl.BlockSpec((bb, TILE, d), tok),
                 pl.BlockSpec((bb, TILE, cw), tok),
                 pl.BlockSpec((bb, TILE, 3 * mw), tok),
                 pl.BlockSpec((bb, TILE, mw), tok),
                 pl.BlockSpec((bb, TILE, ng), tok),
                 pl.BlockSpec((bb, ng, TILE), lambda bi, i: (bi, 0, i)),
                 pl.BlockSpec((bb, TILE, dw), tok),
                 pl.BlockSpec((bb, TILE, dw), tok),
                 pl.BlockSpec((bb, dw, TILE), lambda bi, i: (bi, 0, i))]
    return pl.pallas_call(
        functools.partial(_inproj_kernel, has_peer=has_peer, cw=cw, mw=mw, bb=bb),
        grid=(b // bb, nt), in_specs=in_specs, out_specs=out_specs, out_shape=out_shape,
        compiler_params=_cparams(("parallel", "parallel")),
        name="in_proj",
    )(*args)


def _conv_kernel(up_ref, uc_ref, un_ref, w_ref, b_ref, lg_ref, lb_ref, o_ref, ext_ref):
    i = pl.program_id(1)
    nt = pl.num_programs(1)
    lm = jnp.where(i >= 2, 1.0, 0.0)
    rm = jnp.where(jnp.logical_and(i >= 1, i < nt - 1), 1.0, 0.0)
    hl = CONV_HALO
    ext_ref[0:hl, :] = up_ref[TILE - hl:TILE, :] * lm
    ext_ref[hl:hl + TILE, :] = uc_ref[...]
    ext_ref[hl + TILE:2 * hl + TILE, :] = un_ref[0:hl, :] * rm
    off = hl - CONV_K // 2
    acc = jnp.zeros(uc_ref.shape, F32)
    for k in range(CONV_K):
        acc = acc + w_ref[k:k + 1, :] * ext_ref[off + k:off + k + TILE, :]
    y = acc + b_ref[...]
    mu = jnp.mean(y, axis=-1, keepdims=True)
    yc = y - mu
    var = jnp.mean(yc * yc, axis=-1, keepdims=True)
    z = yc * lax.rsqrt(var + EPS) * lg_ref[...] + lb_ref[...]
    o_ref[...] = (z * jax.nn.sigmoid(z)).astype(o_ref.dtype)


def _conv_call(u, w, bias, lg, lb):
    b, t, cw = u.shape
    nt = t // TILE
    full2 = lambda arr: pl.BlockSpec(arr.shape, lambda bi, i: (0, 0))
    return pl.pallas_call(
        _conv_kernel,
        grid=(b, nt),
        in_specs=[pl.BlockSpec((None, TILE, cw), lambda bi, i: (bi, jnp.maximum(i - 1, 0), 0)),
                  pl.BlockSpec((None, TILE, cw), lambda bi, i: (bi, i, 0)),
                  pl.BlockSpec((None, TILE, cw), lambda bi, i: (bi, jnp.minimum(i + 1, nt - 1), 0)),
                  full2(w), full2(bias), full2(lg), full2(lb)],
        out_specs=pl.BlockSpec((None, TILE, cw), lambda bi, i: (bi, i, 0)),
        out_shape=jax.ShapeDtypeStruct((b, t, cw), BF16),
        scratch_shapes=[pltpu.VMEM((TILE + 2 * CONV_HALO, cw), F32)],
        compiler_params=_cparams(("parallel", "parallel")),
        name="conv_module",
    )(u, u, u, w, bias, lg, lb)


def _mlstm_chunk(qkv_ref, gc_ref, gr_ref, h_ref, c_ref, n_ref, m_ref, fwd):
    tc = qkv_ref.shape[0]
    mw = qkv_ref.shape[1] // 3
    nh = MLSTM_HEADS
    dh = MLSTM_DH
    row = lax.broadcasted_iota(jnp.int32, (tc, tc), 0)
    col = lax.broadcasted_iota(jnp.int32, (tc, tc), 1)
    tri = row >= col if fwd else row <= col
    vis = row <= col if fwd else row >= col
    trif = tri.astype(F32)
    gc = gc_ref[...]
    gr = gr_ref[...]
    bcol = jnp.dot(trif, gc, precision=HIGHEST, preferred_element_type=F32)
    brow = lax.dot_general(gr, trif, NT_DIMS, precision=HIGHEST, preferred_element_type=F32)
    bl = jnp.sum(gr, axis=1, keepdims=True)
    src = gc[:, 0:nh] - bcol[:, nh:2 * nh]

    q = qkv_ref[:, 0:mw]
    k = qkv_ref[:, mw:2 * mw]
    v = qkv_ref[:, 2 * mw:3 * mw]
    qf = q.astype(F32)
    vt = v.astype(F32).T
    vt_b = vt.astype(BF16)
    lane_head = lax.broadcasted_iota(jnp.int32, (tc, mw), 1) // dh
    rhead = lax.broadcasted_iota(jnp.int32, (mw, 1), 0) // dh
    chead = lax.broadcasted_iota(jnp.int32, (1, mw), 1) // dh
    cb = c_ref[...]
    n_old = n_ref[...]
    inter_c = lax.dot_general(cb.astype(BF16), q, NT_DIMS, preferred_element_type=F32)
    zrow = jnp.zeros((1, mw), F32)
    n4 = jnp.concatenate([jnp.where(chead == h, n_old, 0.0) for h in range(nh)] + [zrow] * (SUBLANES - nh),
                         axis=0)
    qn = lax.dot_general(n4, qf, NT_DIMS, precision=HIGHEST, preferred_element_type=F32)

    outs, vws, wks = [], [], []
    decay_col = jnp.zeros((mw, 1), F32)
    decay_row = jnp.zeros((1, mw), F32)
    for h in range(nh):
        hr = slice(h * dh, (h + 1) * dh)
        mh = m_ref[h:h + 1, 0:1]
        bt = brow[nh + h:nh + h + 1, :]
        dlog = jnp.where(vis, bt + src[:, h:h + 1], -jnp.inf)
        inter = bt + mh
        mt = jnp.maximum(inter, jnp.max(dlog, axis=0, keepdims=True))
        dwt = jnp.exp(dlog - mt)
        iw = jnp.exp(inter - mt)
        qh = jnp.where(lane_head == h, qf, 0.0).astype(BF16)
        s = lax.dot_general(k, qh, NT_DIMS, preferred_element_type=F32) * dwt
        num = jnp.dot(vt_b[hr], s.astype(BF16), preferred_element_type=F32)
        den = jnp.sum(s, axis=0, keepdims=True) + iw * qn[h:h + 1, :]
        denom = jnp.maximum(jnp.abs(den), jnp.exp(-mt))
        outs.append((num + iw * inter_c[hr]) / denom)
        blh = bl[nh + h:nh + h + 1, :]
        wlog = blh - bt + gr[h:h + 1, :]
        mn = jnp.maximum(blh + mh, jnp.max(wlog, axis=1, keepdims=True))
        decay = jnp.exp(blh + mh - mn)
        wk = jnp.exp(wlog - mn)
        wks.append(wk)
        vws.append(vt[hr] * wk)
        decay_col = jnp.where(rhead == h, decay, decay_col)
        decay_row = jnp.where(chead == h, decay, decay_row)
        m_ref[h:h + 1, :] = jnp.broadcast_to(mn, (1, m_ref.shape[1]))
    h_ref[...] = jnp.concatenate(outs, axis=0).T

    vw = jnp.concatenate(vws, axis=0).astype(BF16)
    upd = jnp.dot(vw, k, preferred_element_type=F32)
    c_ref[...] = decay_col * cb + jnp.where(rhead == chead, upd, 0.0)
    wk4 = jnp.concatenate(wks + [jnp.zeros((1, tc), F32)] * (SUBLANES - nh), axis=0)
    nk = jnp.dot(wk4, k.astype(F32), precision=HIGHEST, preferred_element_type=F32)
    n_add = zrow
    for h in range(nh):
        n_add = jnp.where(chead == h, nk[h:h + 1, :], n_add)
    n_ref[...] = decay_row * n_old + n_add


def _mlstm_kernel(qkvf_ref, gcf_ref, grf_ref, qkvb_ref, gcb_ref, grb_ref, hf_ref, hb_ref,
                  c_ref, n_ref, m_ref):
    @pl.when(pl.program_id(1) == 0)
    def _():
        c_ref[...] = jnp.zeros_like(c_ref)
        n_ref[...] = jnp.zeros_like(n_ref)
        m_ref[...] = jnp.zeros_like(m_ref)

    _mlstm_chunk(qkvf_ref, gcf_ref, grf_ref, hf_ref, c_ref.at[0], n_ref.at[0], m_ref.at[0], True)
    _mlstm_chunk(qkvb_ref, gcb_ref, grb_ref, hb_ref, c_ref.at[1], n_ref.at[1], m_ref.at[1], False)


def _mlstm_call(mqkv, gcol, grow):
    b, t, w3 = mqkv.shape
    mw = w3 // 3
    nt = t // TILE
    ng = gcol.shape[-1]

    def rev(j):
        return jnp.where(j == 0, 0, nt - j)

    hshape = jax.ShapeDtypeStruct((b, t, mw), F32)
    return pl.pallas_call(
        _mlstm_kernel,
        grid=(b, nt),
        in_specs=[pl.BlockSpec((None, TILE, w3), lambda bi, j: (bi, j, 0)),
                  pl.BlockSpec((None, None, TILE, ng), lambda bi, j: (bi, 0, j, 0)),
                  pl.BlockSpec((None, None, ng, TILE), lambda bi, j: (bi, 0, 0, j)),
                  pl.BlockSpec((None, TILE, w3), lambda bi, j: (bi, rev(j), 0)),
                  pl.BlockSpec((None, None, TILE, ng), lambda bi, j: (bi, 1, rev(j), 0)),
                  pl.BlockSpec((None, None, ng, TILE), lambda bi, j: (bi, 1, 0, rev(j)))],
        out_specs=[pl.BlockSpec((None, TILE, mw), lambda bi, j: (bi, j, 0)),
                   pl.BlockSpec((None, TILE, mw), lambda bi, j: (bi, rev(j), 0))],
        out_shape=[hshape, hshape],
        scratch_shapes=[pltpu.VMEM((2, mw, mw), F32), pltpu.VMEM((2, 1, mw), F32),
                        pltpu.VMEM((2, SUBLANES, LANES), F32)],
        compiler_params=_cparams(("arbitrary", "arbitrary")),
        name="mlstm_scan",
    )(mqkv, gcol, grow, mqkv, gcol, grow)


def _attn_kernel(*refs, lam_init, kc, nq):
    dl_ref, ng_ref = refs[:2]
    q_refs = refs[2:2 + nq]
    k_ref, vt_ref, o_ref, s_ref, m_ref, l_ref, acc_ref = refs[2 + nq:]
    n_keys = k_ref.shape[0]
    q = jnp.concatenate([r[...] for r in q_refs], axis=0) if nq > 1 else q_refs[0][...]
    lane = lax.broadcasted_iota(jnp.int32, q.shape, 1)
    zero = jnp.zeros_like(q)
    qz = (jnp.where(lane < DIFF_DH, q, zero), jnp.where(lane >= DIFF_DH, q, zero))
    m_ref[...] = jnp.full(m_ref.shape, -jnp.inf, F32)
    l_ref[...] = jnp.zeros_like(l_ref)
    acc_ref[...] = jnp.zeros_like(acc_ref)

    def scores(start, slot):
        kb = k_ref[pl.ds(start, kc), :]
        for c in range(2):
            s_ref[slot, c] = lax.dot_general(kb, qz[c], NT_DIMS, preferred_element_type=F32)

    def softmax_pv(start, slot):
        vtb = vt_ref[:, pl.ds(start, kc)]
        for c in range(2):
            s = s_ref[slot, c]
            m_old = m_ref[c]
            m_new = jnp.maximum(m_old, jnp.max(s, axis=0, keepdims=True))
            alpha = jnp.exp2(m_old - m_new)
            p = jnp.exp2(s - m_new)
            l_ref[c] = alpha * l_ref[c] + jnp.sum(p, axis=0, keepdims=True)
            acc_ref[c] = alpha * acc_ref[c] + jnp.dot(vtb, p.astype(BF16), preferred_element_type=F32)
            m_ref[c] = m_new

    n = n_keys // kc
    pairs = (n - 1) // 2
    scores(0, 0)
    if pairs:
        def body(jj, carry):
            base = pl.multiple_of(jj * (2 * kc), TILE)
            scores(base + kc, 1)
            softmax_pv(base, 0)
            scores(base + 2 * kc, 0)
            softmax_pv(base + kc, 1)
            return carry

        lax.fori_loop(0, pairs, body, 0)
    done = 2 * pairs
    if n - done == 2:
        scores((done + 1) * kc, 1)
    softmax_pv(done * kc, 0)
    if n - done == 2:
        softmax_pv((done + 1) * kc, 1)

    dl = dl_ref[...]
    lam = (jnp.exp(jnp.sum(dl[0:1] * dl[1:2], axis=1, keepdims=True))
           - jnp.exp(jnp.sum(dl[2:3] * dl[3:4], axis=1, keepdims=True)) + lam_init)
    ot = acc_ref[0] / l_ref[0] - lam * (acc_ref[1] / l_ref[1])
    ms = jnp.mean(ot * ot, axis=0, keepdims=True)
    y = ot * lax.rsqrt(ms + EPS) * (ng_ref[...] * (1.0 - lam_init))
    o_ref[...] = y.T.astype(o_ref.dtype)


def _attn_key_chunk(t):
    return max(k for k in range(TILE, ATT_KC + 1, TILE) if t % k == 0)


def _attn_call(dq, dk, dvt, dlam, dng, lam_init, *, ctx):
    b, t, w = dq.shape
    nh = w // DIFF_DV
    if ctx:
        n_keys, nq, steps, q0 = TILE, 1, 1, 0
    else:
        n_keys, nq, q0 = t, ATT_TQ // TILE, 1
        steps = (t - TILE) // ATT_TQ
    tq = nq * TILE
    kc = _attn_key_chunk(n_keys)
    kern = functools.partial(_attn_kernel, lam_init=lam_init, kc=kc, nq=nq)
    q_specs = [pl.BlockSpec((None, TILE, DIFF_DV), functools.partial(
        lambda bi, h, i, j: (bi, q0 + nq * i + j, h), j=j)) for j in range(nq)]
    return pl.pallas_call(
        kern,
        grid=(b, nh, steps),
        in_specs=[pl.BlockSpec(dlam.shape, lambda bi, h, i: (0, 0)),
                  pl.BlockSpec(dng.shape, lambda bi, h, i: (0, 0))] + q_specs + [
                  pl.BlockSpec((None, n_keys, DIFF_DV), lambda bi, h, i: (bi, 0, h)),
                  pl.BlockSpec((None, DIFF_DV, n_keys), lambda bi, h, i: (bi, h, 0))],
        out_specs=pl.BlockSpec((None, tq, DIFF_DV), lambda bi, h, i: (bi, i, h)),
        out_shape=jax.ShapeDtypeStruct((b, steps * tq, w), BF16),
        scratch_shapes=[pltpu.VMEM((2, 2, kc, tq), F32),
                        pltpu.VMEM((2, 1, tq), F32), pltpu.VMEM((2, 1, tq), F32),
                        pltpu.VMEM((2, DIFF_DV, tq), F32)],
        compiler_params=_cparams(("parallel", "parallel", "arbitrary")),
        name="diff_attn_ctx" if ctx else "diff_attn",
    )(dlam, dng, *([dq] * nq), dk, dvt)


def _outproj_kernel(x_ref, conv_ref, hf_ref, hb_ref, mo_ref, dyc_ref, dyl_ref, mod_ref, mng_ref, n2g_ref,
                    wo_ref, xo_ref, h2_ref, h2t_ref):
    cw = conv_ref.shape[1]
    mw = hf_ref.shape[1]
    hh = hf_ref[...] + hb_ref[...]
    r = lax.broadcasted_iota(jnp.int32, (mw, mw), 0) // MLSTM_DH
    c = lax.broadcasted_iota(jnp.int32, (mw, mw), 1) // MLSTM_DH
    gm = jnp.where(r == c, 1.0 / MLSTM_DH, 0.0)
    ms = jnp.dot(hh * hh, gm, precision=HIGHEST, preferred_element_type=F32)
    ym = hh * lax.rsqrt(ms + EPS) * mng_ref[...] * jax.nn.sigmoid(mo_ref[...])
    dy = jnp.where(pl.program_id(1) == 0, dyc_ref[...], dyl_ref[...])
    o = (jnp.dot(conv_ref[...], wo_ref[0:cw, :], preferred_element_type=F32)
         + jnp.dot(ym.astype(BF16), wo_ref[cw:cw + mw, :], preferred_element_type=F32)
         + jnp.dot(dy, wo_ref[cw + mw:, :], preferred_element_type=F32))
    x = x_ref[...] + mod_ref[2:3, :] * o
    xo_ref[...] = x
    h2 = _rms(x) * n2g_ref[...] * (1.0 + mod_ref[4:5, :]) + mod_ref[3:4, :]
    h2_ref[...] = h2.astype(BF16)
    h2t_ref[...] = h2.T.astype(BF16)


def _outproj_call(x, conv, hf, hb, mo, dyc, dyl, mod, mng, n2g, wo):
    b, t, d = x.shape
    nt = t // TILE
    cw, mw, dw = conv.shape[-1], mo.shape[-1], dyl.shape[-1]
    tok = lambda bi, i: (bi, i, 0)
    full2 = lambda arr: pl.BlockSpec(arr.shape, lambda bi, i: (0, 0))
    return pl.pallas_call(
        _outproj_kernel,
        grid=(b, nt),
        in_specs=[pl.BlockSpec((None, TILE, d), tok),
                  pl.BlockSpec((None, TILE, cw), tok),
                  pl.BlockSpec((None, TILE, mw), tok),
                  pl.BlockSpec((None, TILE, mw), tok),
                  pl.BlockSpec((None, TILE, mw), tok),
                  pl.BlockSpec((None, TILE, dw), lambda bi, i: (bi, 0, 0)),
                  pl.BlockSpec((None, TILE, dw), lambda bi, i: (bi, jnp.maximum(i - 1, 0), 0)),
                  pl.BlockSpec((None, None, 6, d), lambda bi, i: (bi, jnp.minimum(i, 1), 0, 0)),
                  full2(mng), full2(n2g), full2(wo)],
        out_specs=[pl.BlockSpec((None, TILE, d), tok), pl.BlockSpec((None, TILE, d), tok),
                   pl.BlockSpec((d, TILE), lambda bi, i: (0, bi * nt + i))],
        out_shape=[jax.ShapeDtypeStruct((b, t, d), F32), jax.ShapeDtypeStruct((b, t, d), BF16),
                   jax.ShapeDtypeStruct((d, b * t), BF16)],
        compiler_params=_cparams(("parallel", "parallel")),
        name="out_proj",
    )(x, conv, hf, hb, mo, dyc, dyl, mod, mng, n2g, wo)


def _top_k_rows(s, k):
    idx = lax.broadcasted_iota(jnp.int32, s.shape, 0).astype(F32)
    vals, ids = [], []
    cur = s
    for _ in range(k):
        m = jnp.max(cur, axis=0, keepdims=True)
        am = jnp.min(jnp.where(cur == m, idx, float(s.shape[0])), axis=0, keepdims=True)
        vals.append(m)
        ids.append(am)
        cur = jnp.where(idx == am, -jnp.inf, cur)
    return vals, ids


def _select_exact(s0, s1):
    kk = PEER_TOPK
    tt = s0.shape[1]
    iota_k = lax.broadcasted_iota(jnp.int32, (kk, tt), 0).astype(F32)
    iota_n = lax.broadcasted_iota(jnp.int32, (N_KEYS, tt), 0).astype(F32)
    v0, i0 = _top_k_rows(s0, kk)
    v1, i1 = _top_k_rows(s1, kk)
    st0 = jnp.concatenate(v0, axis=0)
    n = jnp.zeros((kk, tt), F32)
    hv = st0 + v1[0]
    mx = v0[0] + v1[0]
    zsum = jnp.zeros((1, tt), F32)
    for _ in range(kk):
        m = jnp.max(hv, axis=0, keepdims=True)
        a_star = jnp.min(jnp.where(hv == m, iota_k, float(kk)), axis=0, keepdims=True)
        sel = iota_k == a_star
        zsum = zsum + jnp.exp(m - mx)
        n = n + jnp.where(sel, 1.0, 0.0)
        nxt = jnp.full((kk, tt), -jnp.inf, F32)
        for b in range(1, kk):
            nxt = jnp.where(n == float(b), v1[b], nxt)
        hv = jnp.where(sel, st0 + nxt, hv)
    ni = jnp.zeros((N_KEYS, tt), F32)
    r1 = jnp.full((N_KEYS, tt), float(kk), F32)
    for a in range(kk):
        ni = jnp.where(iota_n == i0[a], n[a:a + 1, :], ni)
        r1 = jnp.where(iota_n == i1[a], float(a), r1)
    return jnp.exp(s0 - v0[0]) * (0.5 / zsum), ni, r1, jnp.exp(s1 - v1[0])


def _top_k_ranks(s, k):
    cur = s
    rank = jnp.full(s.shape, float(k), F32)
    vals = []
    for j in range(k):
        m = jnp.max(cur, axis=0, keepdims=True)
        eq = cur == m
        vals.append(m)
        rank = jnp.where(eq, float(j), rank)
        cur = jnp.where(eq, -jnp.inf, cur)
    cnt = jnp.sum(jnp.where(rank < float(k), 1.0, 0.0), axis=0, keepdims=True)
    return vals, rank, cnt


def _select_fast(s0, s1):
    kk = PEER_TOPK
    tt = s0.shape[1]
    v0, rank0, cnt0 = _top_k_ranks(s0, kk)
    v1, rank1, cnt1 = _top_k_ranks(s1, kk)
    half = SUBLANES
    st1 = jnp.concatenate(v1, axis=0)
    st1_h = st1[0:half]
    riota = lax.broadcasted_iota(jnp.int32, (half, tt), 0)
    blocks = [v0[0] + st1]
    for a in range(1, half):
        blocks.append(jnp.where(riota < kk // (a + 1), v0[a] + st1_h, -jnp.inf))
    blocks.append(jnp.concatenate(v0[half:], axis=0) + v1[0])
    p = jnp.concatenate(blocks, axis=0)
    cur = p
    for _ in range(kk):
        m = jnp.max(cur, axis=0, keepdims=True)
        cur = jnp.where(cur == m, -jnp.inf, cur)
    picked = jnp.where(cur != p, 1.0, 0.0)
    mx = v0[0] + v1[0]
    zsum = jnp.sum(picked * jnp.exp(p - mx), axis=0, keepdims=True)
    cntm = jnp.sum(picked, axis=0, keepdims=True)
    n_rows = [jnp.sum(picked[0:kk], axis=0, keepdims=True)]
    for a in range(1, half):
        lo = kk + (a - 1) * half
        n_rows.append(jnp.sum(picked[lo:lo + half], axis=0, keepdims=True))
    last = kk + (half - 1) * half
    n_rows += [picked[last + r:last + r + 1] for r in range(kk - half)]
    ni = jnp.zeros((N_KEYS, tt), F32)
    for a in range(kk):
        ni = jnp.where(rank0 == float(a), n_rows[a], ni)
    want = float(kk)
    bad = jnp.where((cnt0 != want) | (cnt1 != want) | (cntm != want), 1.0, 0.0)
    return jnp.exp(s0 - v0[0]) * (0.5 / zsum), ni, rank1, jnp.exp(s1 - v1[0]), bad


def _peer_sel_kernel(h2_ref, wq_ref, kh_ref, a_ref, ni_ref, r1_ref, bv_ref):
    q = jnp.dot(h2_ref[...], wq_ref[...], preferred_element_type=F32).astype(BF16)
    hw = q.shape[1] // PEER_HEADS

    def head_scores(h):
        st_ = lax.dot_general(kh_ref[h], q[:, h * hw:(h + 1) * hw], NT_DIMS,
                              preferred_element_type=F32)
        return st_[0:N_KEYS], st_[N_KEYS:2 * N_KEYS]

    def store(h, a, ni, r1, bv):
        a_ref[h] = a
        ni_ref[h] = ni
        r1_ref[h] = r1.astype(BF16)
        bv_ref[h] = bv.astype(BF16)

    for h in range(PEER_HEADS):
        s0, s1 = head_scores(h)
        a, ni, r1, bv, bad = _select_fast(s0, s1)
        store(h, a, ni, r1, bv)

        @pl.when(jnp.max(bad) > 0.0)
        def _(h=h, s0=s0, s1=s1):
            store(h, *_select_exact(s0, s1))


def _peer_sel_call(h2, wq, kh):
    ntok, d = h2.shape
    nt = ntok // PEER_SEL_TILE
    shp = jax.ShapeDtypeStruct((PEER_HEADS, N_KEYS, ntok), F32)
    shp_b = jax.ShapeDtypeStruct((PEER_HEADS, N_KEYS, ntok), BF16)
    ospec = pl.BlockSpec((PEER_HEADS, N_KEYS, PEER_SEL_TILE), lambda i: (0, 0, i))
    return pl.pallas_call(
        _peer_sel_kernel,
        grid=(nt,),
        in_specs=[pl.BlockSpec((PEER_SEL_TILE, d), lambda i: (i, 0)),
                  pl.BlockSpec(wq.shape, lambda i: (0, 0)),
                  pl.BlockSpec(kh.shape, lambda i: (0, 0, 0))],
        out_specs=[ospec] * 4,
        out_shape=[shp, shp, shp_b, shp_b],
        compiler_params=_cparams(("parallel",)),
        name="peer_select",
    )(h2, wq, kh)


def _gelu_x2(x):
    return x + x * lax.erf(x * (2.0 ** -0.5))


def _peer_dense_kernel(h2t_ref, a_ref, ni_ref, r1_ref, bv_ref, u_ref, vt_ref, o_ref, acc_ref):
    c = pl.program_id(1)

    @pl.when(c == 0)
    def _():
        acc_ref[...] = jnp.zeros_like(acc_ref)

    tm = h2t_ref.shape[1]
    nib = u_ref.shape[0] // N_KEYS
    rep = N_KEYS // BF16_SUBLANES

    def rows(ref, h, ib):
        r = jnp.broadcast_to(ref[h, ib:ib + 1, :], (BF16_SUBLANES, tm)).astype(BF16)
        return jnp.concatenate([r] * rep, axis=0)

    st_ = jnp.dot(u_ref[...], h2t_ref[...], preferred_element_type=F32)
    blocks = []
    for ib in range(nib):
        g = None
        for h in range(PEER_HEADS):
            bvh = bv_ref[h]
            keep = r1_ref[h] < rows(ni_ref, h, ib)
            term = rows(a_ref, h, ib) * jnp.where(keep, bvh, jnp.zeros_like(bvh))
            g = term if g is None else g + term
        blocks.append(g * _gelu_x2(st_[ib * N_KEYS:(ib + 1) * N_KEYS].astype(BF16)))
    wt = jnp.concatenate(blocks, axis=0)
    acc_ref[...] += jnp.dot(vt_ref[...], wt, preferred_element_type=F32)

    @pl.when(c == pl.num_programs(1) - 1)
    def _():
        o_ref[...] = acc_ref[...].T


def _peer_dense_call(h2t, a, ni, r1, bv, u, vt):
    d, ntok = h2t.shape
    ne = u.shape[0]
    tm, ec = PEER_TM, PEER_EC
    ib = ec // N_KEYS
    rowspec = pl.BlockSpec((PEER_HEADS, ib, tm), lambda t, c: (0, c, t))
    colspec = pl.BlockSpec((PEER_HEADS, N_KEYS, tm), lambda t, c: (0, 0, t))
    return pl.pallas_call(
        _peer_dense_kernel,
        grid=(ntok // tm, ne // ec),
        in_specs=[pl.BlockSpec((d, tm), lambda t, c: (0, t)),
                  rowspec, rowspec, colspec, colspec,
                  pl.BlockSpec((ec, d), lambda t, c: (c, 0)),
                  pl.BlockSpec((d, ec), lambda t, c: (0, c))],
        out_specs=pl.BlockSpec((tm, d), lambda t, c: (t, 0)),
        out_shape=jax.ShapeDtypeStruct((ntok, d), F32),
        scratch_shapes=[pltpu.VMEM((d, tm), F32)],
        compiler_params=_cparams(("parallel", "arbitrary")),
        name="peer_dense",
    )(h2t, a, ni, r1, bv, u, vt)


def _final_kernel(x_ref, peer_ref, mod_ref, g_ref, o_ref):
    x = x_ref[...] + mod_ref[5:6, :] * peer_ref[...]
    o_ref[...] = _rms(x) * g_ref[...]


def _final_call(x, peer, mod, fg, n_ctx_tiles):
    b, t, d = x.shape
    nl = t // TILE - n_ctx_tiles
    tok = lambda bi, i: (bi, i + n_ctx_tiles, 0)
    return pl.pallas_call(
        _final_kernel,
        grid=(b, nl),
        in_specs=[pl.BlockSpec((None, TILE, d), tok), pl.BlockSpec((None, TILE, d), tok),
                  pl.BlockSpec((None, None, 6, d), lambda bi, i: (bi, 1, 0, 0)),
                  pl.BlockSpec(fg.shape, lambda bi, i: (0, 0))],
        out_specs=pl.BlockSpec((None, TILE, d), lambda bi, i: (bi, i, 0)),
        out_shape=jax.ShapeDtypeStruct((b, nl * TILE, d), F32),
        compiler_params=_cparams(("parallel", "parallel")),
        name="final_norm",
    )(x, peer, mod, fg)


def _rope_tables(seq, n_ctx):
    rows = seq // GRID_W
    axis_rot = DIFF_DH // 2
    row = jnp.repeat(jnp.arange(rows), GRID_W).astype(F32)
    col = jnp.tile(jnp.arange(GRID_W), rows).astype(F32)
    inv = ROPE_BASE ** (-jnp.arange(0, axis_rot, 2, dtype=F32) / axis_rot)
    ang = jnp.concatenate([row[:, None] * inv, col[:, None] * inv], axis=-1)
    cos = jnp.repeat(jnp.cos(ang), 2, axis=-1)
    sin = jnp.repeat(jnp.sin(ang), 2, axis=-1)
    even = (jnp.arange(DIFF_DH) % 2 == 0)[None, :]
    sa = jnp.where(even, -sin, 0.0)
    sb = jnp.where(even, 0.0, sin)
    rep = LANES // DIFF_DH

    def full(tab, ctx_val):
        tab = jnp.tile(tab, (1, rep))
        return jnp.concatenate([jnp.full((n_ctx, LANES), ctx_val, F32), tab], axis=0)

    return full(cos, 1.0), full(sa, 0.0), full(sb, 0.0)


def kernel(x, c, ctx, c_ctx, ada_w, ada_b, norm1_g, norm2_g, w_in, conv_w, conv_b, conv_ln_g, conv_ln_b, mlstm_gate_b, mlstm_norm_g, diff_lambda, diff_norm_g, w_out, peer_wq, peer_keys, peer_u, peer_v, final_g):
    b, seq, d = x.shape
    n_ctx = ctx.shape[1]
    depth = ada_w.shape[0]
    cw, mw, dw = d // 4, d // 4, d // 2
    ng = 4 * MLSTM_HEADS
    assert n_ctx == TILE and seq % ATT_TQ == 0 and seq % GRID_W == 0
    assert w_in.shape[-1] == 2 * cw + 4 * mw + ng + 3 * dw
    assert peer_keys.shape[1:] == (PEER_HEADS, 2, N_KEYS, d // PEER_HEADS // 2)
    assert conv_w.shape[1] == CONV_K and (b * (seq + n_ctx)) % PEER_TM == 0

    rows = -(-(b + 1) // SUBLANES) * SUBLANES
    cvec = jnp.zeros((rows, d), F32).at[:b].set(c).at[b].set(c_ctx)
    mods = _ada_call(cvec, ada_w, ada_b)
    mod_lat = mods[:, :b].reshape(depth, b, 1, 6, d)
    mod_ctx = jnp.broadcast_to(mods[:, b].reshape(depth, 1, 1, 6, d), (depth, b, 1, 6, d))
    mod_all = jnp.concatenate([mod_ctx, mod_lat], axis=2)

    cos, sa, sb = _rope_tables(seq, n_ctx)
    g0 = 2 * cw + 4 * mw
    w_main = jnp.concatenate([w_in[:, :, :g0], w_in[:, :, g0 + ng:g0 + ng + 2 * dw], w_in[:, :, g0:g0 + ng],
                              jnp.zeros((depth, d, LANES - ng), F32)], axis=-1).astype(BF16)
    w_dvt = jnp.swapaxes(w_in[:, :, g0 + ng + 2 * dw:], 1, 2).astype(BF16)
    gbias = jnp.pad(mlstm_gate_b.reshape(depth, 1, ng), ((0, 0), (0, 0), (0, LANES - ng)))
    w_out_b = w_out.astype(BF16)
    wq_b = peer_wq.astype(BF16)
    hw = d // PEER_HEADS
    kz = jnp.zeros((depth, PEER_HEADS, N_KEYS, hw // 2), F32)
    kh = jnp.concatenate([jnp.concatenate([peer_keys[:, :, 0], kz], axis=-1),
                          jnp.concatenate([kz, peer_keys[:, :, 1]], axis=-1)], axis=2).astype(BF16)
    u_b = peer_u.astype(BF16)
    vt_b = jnp.swapaxes(peer_v, 1, 2).astype(BF16)
    mng = jnp.tile(mlstm_norm_g, (1, MLSTM_HEADS))

    xs = jnp.concatenate([ctx, x], axis=1)
    t = xs.shape[1]
    peer = None
    for l in range(depth):
        lam_init = 0.8 - 0.6 * math.exp(-0.3 * l)
        xs, u, mqkv, mo, gcol, grow, dq, dk, dvt = _inproj_call(
            xs, peer, mod_all[l - 1] if l else None, mod_all[l], norm1_g[l][None], w_main[l],
            w_dvt[l], gbias[l], ng, cos, sa, sb)
        conv = _conv_call(u, conv_w[l, :, 0, :], conv_b[l][None], conv_ln_g[l][None], conv_ln_b[l][None])
        gcol_d = gcol.reshape(b, t, 2, ng // 2).transpose(0, 2, 1, 3)
        grow_d = grow.reshape(b, 2, ng // 2, t)
        hf, hb = _mlstm_call(mqkv, gcol_d, grow_d)
        dyc = _attn_call(dq, dk, dvt, diff_lambda[l], diff_norm_g[l][:, None], lam_init, ctx=True)
        dyl = _attn_call(dq, dk, dvt, diff_lambda[l], diff_norm_g[l][:, None], lam_init, ctx=False)
        xs, h2, h2t = _outproj_call(xs, conv, hf, hb, mo, dyc, dyl, mod_all[l], mng[l][None], norm2_g[l][None],
                                    w_out_b[l])
        h2f = h2.reshape(b * t, d)
        a, ni, r1, bv = _peer_sel_call(h2f, wq_b[l], kh[l])
        peer = _peer_dense_call(h2t, a, ni, r1, bv, u_b[l], vt_b[l]).reshape(b, t, d)
    return _final_call(xs, peer, mod_all[depth - 1], final_g[None], n_ctx // TILE)
```

```python
import functools
import math

import jax
import jax.numpy as jnp
from jax import lax
from jax.experimental import pallas as pl
from jax.experimental.pallas import tpu as pltpu

F32 = jnp.float32
BF16 = jnp.bfloat16
HIGHEST = lax.Precision.HIGHEST

GRID_W = 64
EPS = 1e-6
CONV_K = 31
MLSTM_DH = 64
MLSTM_HEADS = 4
DIFF_DH = 64
DIFF_DV = 128
DIFF_HEADS = 4
ROPE_BASE = 10000.0
PEER_HEADS = 8
N_KEYS = 128
PEER_TOPK = 16

LANES = 128
SUBLANES = 8
TILE = 256
CONV_HALO = 16
ATT_KC = 1024
ATT_TQ = 512
PEER_SEL_TILE = 256
PEER_TM = 1024
BF16_SUBLANES = 16
PEER_EC = 1024
VMEM_LIMIT = 56 * 1024 * 1024

NT_DIMS = (((1,), (1,)), ((), ()))


def _cparams(sem):
    return pltpu.CompilerParams(dimension_semantics=sem, vmem_limit_bytes=VMEM_LIMIT)


def _rms(x, eps=EPS):
    return x * lax.rsqrt(jnp.mean(x * x, axis=-1, keepdims=True) + eps)


def _log_sigmoid(x):
    return jnp.minimum(x, 0.0) - jnp.log(1.0 + jnp.exp(-jnp.abs(x)))


def _ada_kernel(c_ref, w_ref, b_ref, o_ref):
    c = c_ref[...]
    s = (c * jax.nn.sigmoid(c)).astype(BF16)
    o_ref[...] = jnp.dot(s, w_ref[...].astype(BF16), preferred_element_type=F32) + b_ref[...]


def _ada_call(cvec, ada_w, ada_b):
    depth, d, n = ada_w.shape
    tn = 1536
    rows = cvec.shape[0]
    return pl.pallas_call(
        _ada_kernel,
        grid=(depth, n // tn),
        in_specs=[pl.BlockSpec((rows, d), lambda l, j: (0, 0)),
                  pl.BlockSpec((None, d, tn), lambda l, j: (l, 0, j)),
                  pl.BlockSpec((None, 1, tn), lambda l, j: (l, 0, j))],
        out_specs=pl.BlockSpec((None, rows, tn), lambda l, j: (l, 0, j)),
        out_shape=jax.ShapeDtypeStruct((depth, rows, n), F32),
        compiler_params=_cparams(("parallel", "parallel")),
        name="ada_mod",
    )(cvec, ada_w, ada_b.reshape(depth, 1, n))


def _rope(t, c, sa, sb):
    w = t.shape[1]
    rep = w // LANES
    c, sa, sb = (jnp.concatenate([z] * rep, axis=1) for z in (c, sa, sb))
    return t * c + pltpu.roll(t, w - 1, 1) * sa + pltpu.roll(t, 1, 1) * sb


def _inproj_kernel(*refs, has_peer, cw, mw, bb):
    if has_peer:
        x_ref, peer_ref, modp_ref = refs[:3]
        refs = refs[3:]
    (mod_ref, n1g_ref, wm_ref, wdvt_ref, gb_ref, cos_ref, sa_ref, sb_ref,
     xo_ref, u_ref, mqkv_ref, mo_ref, gcol_ref, grow_ref, dq_ref, dk_ref, dvt_ref) = refs[-17:]
    if not has_peer:
        x_ref = refs[0]
    hs = []
    for k in range(bb):
        x = x_ref[k]
        if has_peer:
            x = x + modp_ref[k, 5:6, :] * peer_ref[k]
        xo_ref[k] = x
        hs.append(_rms(x) * n1g_ref[...] * (1.0 + mod_ref[k, 1:2, :]) + mod_ref[k, 0:1, :])
    hb = jnp.concatenate(hs, axis=0).astype(BF16)

    def proj(lo, hi):
        return jnp.dot(hb, wm_ref[:, lo:hi], preferred_element_type=F32)

    def rows(z, k):
        return z[k * TILE:(k + 1) * TILE]

    a = proj(0, 2 * cw)
    u = a[:, :cw] * jax.nn.sigmoid(a[:, cw:])
    o = 2 * cw
    mq = proj(o, o + mw).astype(BF16)
    mk = (proj(o + mw, o + 2 * mw) * (MLSTM_DH ** -0.5)).astype(BF16)
    mv = proj(o + 2 * mw, o + 3 * mw).astype(BF16)
    mo = proj(o + 3 * mw, o + 4 * mw)
    o = o + 4 * mw
    dw = dq_ref.shape[2]
    c, sa, sb = cos_ref[...], sa_ref[...], sb_ref[...]
    pq = proj(o, o + dw)
    pk = proj(o + dw, o + 2 * dw)
    dvt = lax.dot_general(wdvt_ref[...], hb, NT_DIMS, preferred_element_type=F32).astype(BF16)
    o = o + 2 * dw
    ng = gcol_ref.shape[2]
    g = proj(o, o + LANES) + gb_ref[...]
    cidx = lax.broadcasted_iota(jnp.int32, g.shape, 1)
    g = jnp.where((cidx // MLSTM_HEADS) % 2 == 1, _log_sigmoid(g), g)
    for k in range(bb):
        u_ref[k] = rows(u, k)
        mqkv_ref[k, :, 0:mw] = rows(mq, k)
        mqkv_ref[k, :, mw:2 * mw] = rows(mk, k)
        mqkv_ref[k, :, 2 * mw:3 * mw] = rows(mv, k)
        mo_ref[k] = rows(mo, k)
        dq_ref[k] = (_rope(rows(pq, k), c, sa, sb) * (DIFF_DH ** -0.5 * math.log2(math.e))).astype(BF16)
        dk_ref[k] = _rope(rows(pk, k), c, sa, sb).astype(BF16)
        dvt_ref[k] = dvt[:, k * TILE:(k + 1) * TILE]
        gk = rows(g, k)
        gcol_ref[k] = gk[:, :ng]
        grow_ref[k] = gk.T[:ng, :]


def _batch_block(b):
    return 2 if b % 2 == 0 else 1


def _inproj_call(x, peer, modp, mod, n1g, wm, wdvt, gb, ng, cos, sa, sb):
    b, t, d = x.shape
    nt = t // TILE
    bb = _batch_block(b)
    cw = d // 4
    mw = d // 4
    dw = d // 2
    has_peer = peer is not None
    tok = lambda bi, i: (bi, i, 0)
    modspec = pl.BlockSpec((bb, None, 6, d), lambda bi, i: (bi, jnp.minimum(i, 1), 0, 0))
    full2 = lambda arr: pl.BlockSpec(arr.shape, lambda bi, i: (0, 0))
    in_specs = [pl.BlockSpec((bb, TILE, d), tok)]
    args = [x]
    if has_peer:
        in_specs += [pl.BlockSpec((bb, TILE, d), tok), modspec]
        args += [peer, modp]
    in_specs += [modspec, full2(n1g), full2(wm), full2(wdvt), full2(gb)]
    args += [mod, n1g, wm, wdvt, gb]
    in_specs += [pl.BlockSpec((TILE, LANES), lambda bi, i: (i, 0))] * 3
    args += [cos, sa, sb]
    out_shape = [jax.ShapeDtypeStruct((b, t, d), F32),
                 jax.ShapeDtypeStruct((b, t, cw), F32),
                 jax.ShapeDtypeStruct((b, t, 3 * mw), BF16),
                 jax.ShapeDtypeStruct((b, t, mw), F32),
                 jax.ShapeDtypeStruct((b, t, ng), F32),
                 jax.ShapeDtypeStruct((b, ng, t), F32),
                 jax.ShapeDtypeStruct((b, t, dw), BF16),
                 jax.ShapeDtypeStruct((b, t, dw), BF16),
                 jax.ShapeDtypeStruct((b, dw, t), BF16)]
    out_specs = [pl.BlockSpec((bb, TILE, d), tok),
                 pl.BlockSpec((bb, TILE, cw), tok),
                 pl.BlockSpec((bb, TILE, 3 * mw), tok),
                 pl.BlockSpec((bb, TILE, mw), tok),
                 pl.BlockSpec((bb, TILE, ng), tok),
                 pl.BlockSpec((bb, ng, TILE), lambda bi, i: (bi, 0, i)),
                 pl.BlockSpec((bb, TILE, dw), tok),
                 pl.BlockSpec((bb, TILE, dw), tok),
                 pl.BlockSpec((bb, dw, TILE), lambda bi, i: (bi, 0, i))]
    return pl.pallas_call(
        functools.partial(_inproj_kernel, has_peer=has_peer, cw=cw, mw=mw, bb=bb),
        grid=(b // bb, nt), in_specs=in_specs, out_specs=out_specs, out_shape=out_shape,
        compiler_params=_cparams(("parallel", "parallel")),
        name="in_proj",
    )(*args)


def _conv_kernel(up_ref, uc_ref, un_ref, w_ref, b_ref, lg_ref, lb_ref, o_ref, ext_ref, sh_ref):
    i = pl.program_id(1)
    nt = pl.num_programs(1)
    lm = jnp.where(i >= 2, 1.0, 0.0)
    rm = jnp.where(jnp.logical_and(i >= 1, i < nt - 1), 1.0, 0.0)
    hl = CONV_HALO
    ext_ref[0:hl, :] = up_ref[TILE - hl:TILE, :] * lm
    ext_ref[hl:hl + TILE, :] = uc_ref[...]
    ext_ref[hl + TILE:2 * hl + TILE, :] = un_ref[0:hl, :] * rm
    off = hl - CONV_K // 2
    span = sh_ref.shape[1]
    acc = jnp.zeros(uc_ref.shape, F32)
    for r in range(SUBLANES):
        taps = [k for k in range(CONV_K) if (off + k) % SUBLANES == r]
        if not taps:
            continue
        sh_ref[r] = ext_ref[r:r + span, :]
        for k in taps:
            q = (off + k) // SUBLANES * SUBLANES
            acc = acc + w_ref[k:k + 1, :] * sh_ref[r, q:q + TILE, :]
    y = acc + b_ref[...]
    mu = jnp.mean(y, axis=-1, keepdims=True)
    yc = y - mu
    var = jnp.mean(yc * yc, axis=-1, keepdims=True)
    z = yc * lax.rsqrt(var + EPS) * lg_ref[...] + lb_ref[...]
    o_ref[...] = (z * jax.nn.sigmoid(z)).astype(o_ref.dtype)


def _conv_call(u, w, bias, lg, lb):
    b, t, cw = u.shape
    nt = t // TILE
    full2 = lambda arr: pl.BlockSpec(arr.shape, lambda bi, i: (0, 0))
    return pl.pallas_call(
        _conv_kernel,
        grid=(b, nt),
        in_specs=[pl.BlockSpec((None, TILE, cw), lambda bi, i: (bi, jnp.maximum(i - 1, 0), 0)),
                  pl.BlockSpec((None, TILE, cw), lambda bi, i: (bi, i, 0)),
                  pl.BlockSpec((None, TILE, cw), lambda bi, i: (bi, jnp.minimum(i + 1, nt - 1), 0)),
                  full2(w), full2(bias), full2(lg), full2(lb)],
        out_specs=pl.BlockSpec((None, TILE, cw), lambda bi, i: (bi, i, 0)),
        out_shape=jax.ShapeDtypeStruct((b, t, cw), BF16),
        scratch_shapes=[pltpu.VMEM((TILE + 2 * CONV_HALO, cw), F32),
                        pltpu.VMEM((SUBLANES, TILE + 2 * CONV_HALO - SUBLANES, cw), F32)],
        compiler_params=_cparams(("parallel", "parallel")),
        name="conv_module",
    )(u, u, u, w, bias, lg, lb)


def _mlstm_chunk(qkv_ref, gc_ref, gr_ref, h_ref, c_ref, n_ref, m_ref, fwd):
    tc = qkv_ref.shape[0]
    mw = qkv_ref.shape[1] // 3
    nh = MLSTM_HEADS
    dh = MLSTM_DH
    row = lax.broadcasted_iota(jnp.int32, (tc, tc), 0)
    col = lax.broadcasted_iota(jnp.int32, (tc, tc), 1)
    tri = row >= col if fwd else row <= col
    vis = row <= col if fwd else row >= col
    trif = tri.astype(F32)
    gc = gc_ref[...]
    gr = gr_ref[...]
    bcol = jnp.dot(trif, gc, precision=HIGHEST, preferred_element_type=F32)
    brow = lax.dot_general(gr, trif, NT_DIMS, precision=HIGHEST, preferred_element_type=F32)
    bl = jnp.sum(gr, axis=1, keepdims=True)
    src = gc[:, 0:nh] - bcol[:, nh:2 * nh]

    q = qkv_ref[:, 0:mw]
    k = qkv_ref[:, mw:2 * mw]
    v = qkv_ref[:, 2 * mw:3 * mw]
    qf = q.astype(F32)
    vt = v.astype(F32).T
    vt_b = vt.astype(BF16)
    lane_head = lax.broadcasted_iota(jnp.int32, (tc, mw), 1) // dh
    rhead = lax.broadcasted_iota(jnp.int32, (mw, 1), 0) // dh
    chead = lax.broadcasted_iota(jnp.int32, (1, mw), 1) // dh
    cb = c_ref[...]
    n_old = n_ref[...]
    inter_c = lax.dot_general(cb.astype(BF16), q, NT_DIMS, preferred_element_type=F32)
    zrow = jnp.zeros((1, mw), F32)
    n4 = jnp.concatenate([jnp.where(chead == h, n_old, 0.0) for h in range(nh)] + [zrow] * (SUBLANES - nh),
                         axis=0)
    qn = lax.dot_general(n4, qf, NT_DIMS, precision=HIGHEST, preferred_element_type=F32)

    outs, vws, wks = [], [], []
    decay_col = jnp.zeros((mw, 1), F32)
    decay_row = jnp.zeros((1, mw), F32)
    for h in range(nh):
        hr = slice(h * dh, (h + 1) * dh)
        mh = m_ref[h:h + 1, 0:1]
        bt = brow[nh + h:nh + h + 1, :]
        dlog = jnp.where(vis, bt + src[:, h:h + 1], -jnp.inf)
        inter = bt + mh
        mt = jnp.maximum(inter, jnp.max(dlog, axis=0, keepdims=True))
        dwt = jnp.exp(dlog - mt)
        iw = jnp.exp(inter - mt)
        qh = jnp.where(lane_head == h, qf, 0.0).astype(BF16)
        s = lax.dot_general(k, qh, NT_DIMS, preferred_element_type=F32) * dwt
        num = jnp.dot(vt_b[hr], s.astype(BF16), preferred_element_type=F32)
        den = jnp.sum(s, axis=0, keepdims=True) + iw * qn[h:h + 1, :]
        denom = jnp.maximum(jnp.abs(den), jnp.exp(-mt))
        outs.append((num + iw * inter_c[hr]) / denom)
        blh = bl[nh + h:nh + h + 1, :]
        wlog = blh - bt + gr[h:h + 1, :]
        mn = jnp.maximum(blh + mh, jnp.max(wlog, axis=1, keepdims=True))
        decay = jnp.exp(blh + mh - mn)
        wk = jnp.exp(wlog - mn)
        wks.append(wk)
        vws.append(vt[hr] * wk)
        decay_col = jnp.where(rhead == h, decay, decay_col)
        decay_row = jnp.where(chead == h, decay, decay_row)
        m_ref[h:h + 1, :] = jnp.broadcast_to(mn, (1, m_ref.shape[1]))
    h_ref[...] = jnp.concatenate(outs, axis=0).T

    vw = jnp.concatenate(vws, axis=0).astype(BF16)
    upd = jnp.dot(vw, k, preferred_element_type=F32)
    c_ref[...] = decay_col * cb + jnp.where(rhead == chead, upd, 0.0)
    wk4 = jnp.concatenate(wks + [jnp.zeros((1, tc), F32)] * (SUBLANES - nh), axis=0)
    nk = jnp.dot(wk4, k.astype(F32), precision=HIGHEST, preferred_element_type=F32)
    n_add = zrow
    for h in range(nh):
        n_add = jnp.where(chead == h, nk[h:h + 1, :], n_add)
    n_ref[...] = decay_row * n_old + n_add


def _mlstm_kernel(qkvf_ref, gcf_ref, grf_ref, qkvb_ref, gcb_ref, grb_ref, hf_ref, hb_ref,
                  c_ref, n_ref, m_ref):
    @pl.when(pl.program_id(1) == 0)
    def _():
        c_ref[...] = jnp.zeros_like(c_ref)
        n_ref[...] = jnp.zeros_like(n_ref)
        m_ref[...] = jnp.zeros_like(m_ref)

    _mlstm_chunk(qkvf_ref, gcf_ref, grf_ref, hf_ref, c_ref.at[0], n_ref.at[0], m_ref.at[0], True)
    _mlstm_chunk(qkvb_ref, gcb_ref, grb_ref, hb_ref, c_ref.at[1], n_ref.at[1], m_ref.at[1], False)


def _mlstm_call(mqkv, gcol, grow):
    b, t, w3 = mqkv.shape
    mw = w3 // 3
    nt = t // TILE
    ng = gcol.shape[-1]

    def rev(j):
        return jnp.where(j == 0, 0, nt - j)

    hshape = jax.ShapeDtypeStruct((b, t, mw), F32)
    return pl.pallas_call(
        _mlstm_kernel,
        grid=(b, nt),
        in_specs=[pl.BlockSpec((None, TILE, w3), lambda bi, j: (bi, j, 0)),
                  pl.BlockSpec((None, None, TILE, ng), lambda bi, j: (bi, 0, j, 0)),
                  pl.BlockSpec((None, None, ng, TILE), lambda bi, j: (bi, 0, 0, j)),
                  pl.BlockSpec((None, TILE, w3), lambda bi, j: (bi, rev(j), 0)),
                  pl.BlockSpec((None, None, TILE, ng), lambda bi, j: (bi, 1, rev(j), 0)),
                  pl.BlockSpec((None, None, ng, TILE), lambda bi, j: (bi, 1, 0, rev(j)))],
        out_specs=[pl.BlockSpec((None, TILE, mw), lambda bi, j: (bi, j, 0)),
                   pl.BlockSpec((None, TILE, mw), lambda bi, j: (bi, rev(j), 0))],
        out_shape=[hshape, hshape],
        scratch_shapes=[pltpu.VMEM((2, mw, mw), F32), pltpu.VMEM((2, 1, mw), F32),
                        pltpu.VMEM((2, SUBLANES, LANES), F32)],
        compiler_params=_cparams(("arbitrary", "arbitrary")),
        name="mlstm_scan",
    )(mqkv, gcol, grow, mqkv, gcol, grow)


def _attn_kernel(*refs, lam_init, kc, nq):
    dl_ref, ng_ref = refs[:2]
    q_refs = refs[2:2 + nq]
    k_ref, vt_ref, o_ref, s_ref, m_ref, l_ref, acc_ref = refs[2 + nq:]
    n_keys = k_ref.shape[0]
    q = jnp.concatenate([r[...] for r in q_refs], axis=0) if nq > 1 else q_refs[0][...]
    lane = lax.broadcasted_iota(jnp.int32, q.shape, 1)
    zero = jnp.zeros_like(q)
    qz = (jnp.where(lane < DIFF_DH, q, zero), jnp.where(lane >= DIFF_DH, q, zero))
    m_ref[...] = jnp.full(m_ref.shape, -jnp.inf, F32)
    l_ref[...] = jnp.zeros_like(l_ref)
    acc_ref[...] = jnp.zeros_like(acc_ref)

    def scores(start, slot):
        kb = k_ref[pl.ds(start, kc), :]
        for c in range(2):
            s_ref[slot, c] = lax.dot_general(kb, qz[c], NT_DIMS, preferred_element_type=F32)

    def softmax_pv(start, slot):
        vtb = vt_ref[:, pl.ds(start, kc)]
        for c in range(2):
            s = s_ref[slot, c]
            m_old = m_ref[c]
            m_new = jnp.maximum(m_old, jnp.max(s, axis=0, keepdims=True))
            alpha = jnp.exp2(m_old - m_new)
            p = jnp.exp2(s - m_new)
            l_ref[c] = alpha * l_ref[c] + jnp.sum(p, axis=0, keepdims=True)
            acc_ref[c] = alpha * acc_ref[c] + jnp.dot(vtb, p.astype(BF16), preferred_element_type=F32)
            m_ref[c] = m_new

    n = n_keys // kc
    pairs = (n - 1) // 2
    scores(0, 0)
    if pairs:
        def body(jj, carry):
            base = pl.multiple_of(jj * (2 * kc), TILE)
            scores(base + kc, 1)
            softmax_pv(base, 0)
            scores(base + 2 * kc, 0)
            softmax_pv(base + kc, 1)
            return carry

        lax.fori_loop(0, pairs, body, 0)
    done = 2 * pairs
    if n - done == 2:
        scores((done + 1) * kc, 1)
    softmax_pv(done * kc, 0)
    if n - done == 2:
        softmax_pv((done + 1) * kc, 1)

    dl = dl_ref[...]
    lam = (jnp.exp(jnp.sum(dl[0:1] * dl[1:2], axis=1, keepdims=True))
           - jnp.exp(jnp.sum(dl[2:3] * dl[3:4], axis=1, keepdims=True)) + lam_init)
    ot = acc_ref[0] / l_ref[0] - lam * (acc_ref[1] / l_ref[1])
    ms = jnp.mean(ot * ot, axis=0, keepdims=True)
    y = ot * lax.rsqrt(ms + EPS) * (ng_ref[...] * (1.0 - lam_init))
    o_ref[...] = y.T.astype(o_ref.dtype)


def _attn_key_chunk(t):
    return max(k for k in range(TILE, ATT_KC + 1, TILE) if t % k == 0)


def _attn_call(dq, dk, dvt, dlam, dng, lam_init, *, ctx):
    b, t, w = dq.shape
    nh = w // DIFF_DV
    if ctx:
        n_keys, nq, steps, q0 = TILE, 1, 1, 0
    else:
        n_keys, nq, q0 = t, ATT_TQ // TILE, 1
        steps = (t - TILE) // ATT_TQ
    tq = nq * TILE
    kc = _attn_key_chunk(n_keys)
    kern = functools.partial(_attn_kernel, lam_init=lam_init, kc=kc, nq=nq)
    q_specs = [pl.BlockSpec((None, TILE, DIFF_DV), functools.partial(
        lambda bi, h, i, j: (bi, q0 + nq * i + j, h), j=j)) for j in range(nq)]
    return pl.pallas_call(
        kern,
        grid=(b, nh, steps),
        in_specs=[pl.BlockSpec(dlam.shape, lambda bi, h, i: (0, 0)),
                  pl.BlockSpec(dng.shape, lambda bi, h, i: (0, 0))] + q_specs + [
                  pl.BlockSpec((None, n_keys, DIFF_DV), lambda bi, h, i: (bi, 0, h)),
                  pl.BlockSpec((None, DIFF_DV, n_keys), lambda bi, h, i: (bi, h, 0))],
        out_specs=pl.BlockSpec((None, tq, DIFF_DV), lambda bi, h, i: (bi, i, h)),
        out_shape=jax.ShapeDtypeStruct((b, steps * tq, w), BF16),
        scratch_shapes=[pltpu.VMEM((2, 2, kc, tq), F32),
                        pltpu.VMEM((2, 1, tq), F32), pltpu.VMEM((2, 1, tq), F32),
                        pltpu.VMEM((2, DIFF_DV, tq), F32)],
        compiler_params=_cparams(("parallel", "parallel", "arbitrary")),
        name="diff_attn_ctx" if ctx else "diff_attn",
    )(dlam, dng, *([dq] * nq), dk, dvt)


def _outproj_kernel(x_ref, conv_ref, hf_ref, hb_ref, mo_ref, dyc_ref, dyl_ref, mod_ref, mng_ref, n2g_ref,
                    wo_ref, xo_ref, h2_ref, h2t_ref):
    cw = conv_ref.shape[1]
    mw = hf_ref.shape[1]
    hh = hf_ref[...] + hb_ref[...]
    r = lax.broadcasted_iota(jnp.int32, (mw, mw), 0) // MLSTM_DH
    c = lax.broadcasted_iota(jnp.int32, (mw, mw), 1) // MLSTM_DH
    gm = jnp.where(r == c, 1.0 / MLSTM_DH, 0.0)
    ms = jnp.dot(hh * hh, gm, precision=HIGHEST, preferred_element_type=F32)
    ym = hh * lax.rsqrt(ms + EPS) * mng_ref[...] * jax.nn.sigmoid(mo_ref[...])
    dy = jnp.where(pl.program_id(1) == 0, dyc_ref[...], dyl_ref[...])
    o = (jnp.dot(conv_ref[...], wo_ref[0:cw, :], preferred_element_type=F32)
         + jnp.dot(ym.astype(BF16), wo_ref[cw:cw + mw, :], preferred_element_type=F32)
         + jnp.dot(dy, wo_ref[cw + mw:, :], preferred_element_type=F32))
    x = x_ref[...] + mod_ref[2:3, :] * o
    xo_ref[...] = x
    h2 = _rms(x) * n2g_ref[...] * (1.0 + mod_ref[4:5, :]) + mod_ref[3:4, :]
    h2_ref[...] = h2.astype(BF16)
    h2t_ref[...] = h2.T.astype(BF16)


def _outproj_call(x, conv, hf, hb, mo, dyc, dyl, mod, mng, n2g, wo):
    b, t, d = x.shape
    nt = t // TILE
    cw, mw, dw = conv.shape[-1], mo.shape[-1], dyl.shape[-1]
    tok = lambda bi, i: (bi, i, 0)
    full2 = lambda arr: pl.BlockSpec(arr.shape, lambda bi, i: (0, 0))
    return pl.pallas_call(
        _outproj_kernel,
        grid=(b, nt),
        in_specs=[pl.BlockSpec((None, TILE, d), tok),
                  pl.BlockSpec((None, TILE, cw), tok),
                  pl.BlockSpec((None, TILE, mw), tok),
                  pl.BlockSpec((None, TILE, mw), tok),
                  pl.BlockSpec((None, TILE, mw), tok),
                  pl.BlockSpec((None, TILE, dw), lambda bi, i: (bi, 0, 0)),
                  pl.BlockSpec((None, TILE, dw), lambda bi, i: (bi, jnp.maximum(i - 1, 0), 0)),
                  pl.BlockSpec((None, None, 6, d), lambda bi, i: (bi, jnp.minimum(i, 1), 0, 0)),
                  full2(mng), full2(n2g), full2(wo)],
        out_specs=[pl.BlockSpec((None, TILE, d), tok), pl.BlockSpec((None, TILE, d), tok),
                   pl.BlockSpec((d, TILE), lambda bi, i: (0, bi * nt + i))],
        out_shape=[jax.ShapeDtypeStruct((b, t, d), F32), jax.ShapeDtypeStruct((b, t, d), BF16),
                   jax.ShapeDtypeStruct((d, b * t), BF16)],
        compiler_params=_cparams(("parallel", "parallel")),
        name="out_proj",
    )(x, conv, hf, hb, mo, dyc, dyl, mod, mng, n2g, wo)


def _top_k_rows(s, k):
    idx = lax.broadcasted_iota(jnp.int32, s.shape, 0).astype(F32)
    vals, ids = [], []
    cur = s
    for _ in range(k):
        m = jnp.max(cur, axis=0, keepdims=True)
        am = jnp.min(jnp.where(cur == m, idx, float(s.shape[0])), axis=0, keepdims=True)
        vals.append(m)
        ids.append(am)
        cur = jnp.where(idx == am, -jnp.inf, cur)
    return vals, ids


def _select_exact(s0, s1):
    kk = PEER_TOPK
    tt = s0.shape[1]
    iota_k = lax.broadcasted_iota(jnp.int32, (kk, tt), 0).astype(F32)
    iota_n = lax.broadcasted_iota(jnp.int32, (N_KEYS, tt), 0).astype(F32)
    v0, i0 = _top_k_rows(s0, kk)
    v1, i1 = _top_k_rows(s1, kk)
    st0 = jnp.concatenate(v0, axis=0)
    n = jnp.zeros((kk, tt), F32)
    hv = st0 + v1[0]
    mx = v0[0] + v1[0]
    zsum = jnp.zeros((1, tt), F32)
    for _ in range(kk):
        m = jnp.max(hv, axis=0, keepdims=True)
        a_star = jnp.min(jnp.where(hv == m, iota_k, float(kk)), axis=0, keepdims=True)
        sel = iota_k == a_star
        zsum = zsum + jnp.exp(m - mx)
        n = n + jnp.where(sel, 1.0, 0.0)
        nxt = jnp.full((kk, tt), -jnp.inf, F32)
        for b in range(1, kk):
            nxt = jnp.where(n == float(b), v1[b], nxt)
        hv = jnp.where(sel, st0 + nxt, hv)
    ni = jnp.zeros((N_KEYS, tt), F32)
    r1 = jnp.full((N_KEYS, tt), float(kk), F32)
    for a in range(kk):
        ni = jnp.where(iota_n == i0[a], n[a:a + 1, :], ni)
        r1 = jnp.where(iota_n == i1[a], float(a), r1)
    return jnp.exp(s0 - v0[0]) * (0.5 / zsum), ni, r1, jnp.exp(s1 - v1[0])


def _top_k_ranks(s, k):
    cur = s
    rank = jnp.full(s.shape, float(k), F32)
    vals = []
    for j in range(k):
        m = jnp.max(cur, axis=0, keepdims=True)
        eq = cur == m
        vals.append(m)
        rank = jnp.where(eq, float(j), rank)
        cur = jnp.where(eq, -jnp.inf, cur)
    cnt = jnp.sum(jnp.where(rank < float(k), 1.0, 0.0), axis=0, keepdims=True)
    return vals, rank, cnt


def _select_fast(s0, s1):
    kk = PEER_TOPK
    tt = s0.shape[1]
    v0, rank0, cnt0 = _top_k_ranks(s0, kk)
    v1, rank1, cnt1 = _top_k_ranks(s1, kk)
    half = SUBLANES
    st1 = jnp.concatenate(v1, axis=0)
    st1_h = st1[0:half]
    riota = lax.broadcasted_iota(jnp.int32, (half, tt), 0)
    blocks = [v0[0] + st1]
    for a in range(1, half):
        blocks.append(jnp.where(riota < kk // (a + 1), v0[a] + st1_h, -jnp.inf))
    blocks.append(jnp.concatenate(v0[half:], axis=0) + v1[0])
    p = jnp.concatenate(blocks, axis=0)
    cur = p
    for _ in range(kk):
        m = jnp.max(cur, axis=0, keepdims=True)
        cur = jnp.where(cur == m, -jnp.inf, cur)
    picked = jnp.where(cur != p, 1.0, 0.0)
    mx = v0[0] + v1[0]
    zsum = jnp.sum(picked * jnp.exp(p - mx), axis=0, keepdims=True)
    cntm = jnp.sum(picked, axis=0, keepdims=True)
    n_rows = [jnp.sum(picked[0:kk], axis=0, keepdims=True)]
    for a in range(1, half):
        lo = kk + (a - 1) * half
        n_rows.append(jnp.sum(picked[lo:lo + half], axis=0, keepdims=True))
    last = kk + (half - 1) * half
    n_rows += [picked[last + r:last + r + 1] for r in range(kk - half)]
    ni = jnp.zeros((N_KEYS, tt), F32)
    for a in range(kk):
        ni = jnp.where(rank0 == float(a), n_rows[a], ni)
    want = float(kk)
    bad = jnp.where((cnt0 != want) | (cnt1 != want) | (cntm != want), 1.0, 0.0)
    return jnp.exp(s0 - v0[0]) * (0.5 / zsum), ni, rank1, jnp.exp(s1 - v1[0]), bad


def _peer_sel_kernel(h2_ref, wq_ref, kh_ref, a_ref, ni_ref, r1_ref, bv_ref):
    q = jnp.dot(h2_ref[...], wq_ref[...], preferred_element_type=F32).astype(BF16)
    hw = q.shape[1] // PEER_HEADS

    def head_scores(h):
        st_ = lax.dot_general(kh_ref[h], q[:, h * hw:(h + 1) * hw], NT_DIMS,
                              preferred_element_type=F32)
        return st_[0:N_KEYS], st_[N_KEYS:2 * N_KEYS]

    def store(h, a, ni, r1, bv):
        a_ref[h] = a
        ni_ref[h] = ni
        r1_ref[h] = r1.astype(BF16)
        bv_ref[h] = bv.astype(BF16)

    for h in range(PEER_HEADS):
        s0, s1 = head_scores(h)
        a, ni, r1, bv, bad = _select_fast(s0, s1)
        store(h, a, ni, r1, bv)

        @pl.when(jnp.max(bad) > 0.0)
        def _(h=h, s0=s0, s1=s1):
            store(h, *_select_exact(s0, s1))


def _peer_sel_call(h2, wq, kh):
    ntok, d = h2.shape
    nt = ntok // PEER_SEL_TILE
    shp = jax.ShapeDtypeStruct((PEER_HEADS, N_KEYS, ntok), F32)
    shp_b = jax.ShapeDtypeStruct((PEER_HEADS, N_KEYS, ntok), BF16)
    ospec = pl.BlockSpec((PEER_HEADS, N_KEYS, PEER_SEL_TILE), lambda i: (0, 0, i))
    return pl.pallas_call(
        _peer_sel_kernel,
        grid=(nt,),
        in_specs=[pl.BlockSpec((PEER_SEL_TILE, d), lambda i: (i, 0)),
                  pl.BlockSpec(wq.shape, lambda i: (0, 0)),
                  pl.BlockSpec(kh.shape, lambda i: (0, 0, 0))],
        out_specs=[ospec] * 4,
        out_shape=[shp, shp, shp_b, shp_b],
        compiler_params=_cparams(("parallel",)),
        name="peer_select",
    )(h2, wq, kh)


def _gelu_x2(x):
    return x + x * lax.erf(x * (2.0 ** -0.5))


def _peer_dense_kernel(h2t_ref, a_ref, ni_ref, r1_ref, bv_ref, u_ref, vt_ref, o_ref, acc_ref):
    c = pl.program_id(1)

    @pl.when(c == 0)
    def _():
        acc_ref[...] = jnp.zeros_like(acc_ref)

    tm = h2t_ref.shape[1]
    nib = u_ref.shape[0] // N_KEYS
    rep = N_KEYS // BF16_SUBLANES

    def rows(ref, h, ib):
        r = jnp.broadcast_to(ref[h, ib:ib + 1, :], (BF16_SUBLANES, tm)).astype(BF16)
        return jnp.concatenate([r] * rep, axis=0)

    st_ = jnp.dot(u_ref[...], h2t_ref[...], preferred_element_type=F32)
    blocks = []
    for ib in range(nib):
        g = None
        for h in range(PEER_HEADS):
            bvh = bv_ref[h]
            keep = r1_ref[h] < rows(ni_ref, h, ib)
            term = rows(a_ref, h, ib) * jnp.where(keep, bvh, jnp.zeros_like(bvh))
            g = term if g is None else g + term
        blocks.append(g * _gelu_x2(st_[ib * N_KEYS:(ib + 1) * N_KEYS].astype(BF16)))
    wt = jnp.concatenate(blocks, axis=0)
    acc_ref[...] += jnp.dot(vt_ref[...], wt, preferred_element_type=F32)

    @pl.when(c == pl.num_programs(1) - 1)
    def _():
        o_ref[...] = acc_ref[...].T


def _peer_dense_call(h2t, a, ni, r1, bv, u, vt):
    d, ntok = h2t.shape
    ne = u.shape[0]
    tm, ec = PEER_TM, PEER_EC
    ib = ec // N_KEYS
    rowspec = pl.BlockSpec((PEER_HEADS, ib, tm), lambda t, c: (0, c, t))
    colspec = pl.BlockSpec((PEER_HEADS, N_KEYS, tm), lambda t, c: (0, 0, t))
    return pl.pallas_call(
        _peer_dense_kernel,
        grid=(ntok // tm, ne // ec),
        in_specs=[pl.BlockSpec((d, tm), lambda t, c: (0, t)),
                  rowspec, rowspec, colspec, colspec,
                  pl.BlockSpec((ec, d), lambda t, c: (c, 0)),
                  pl.BlockSpec((d, ec), lambda t, c: (0, c))],
        out_specs=pl.BlockSpec((tm, d), lambda t, c: (t, 0)),
        out_shape=jax.ShapeDtypeStruct((ntok, d), F32),
        scratch_shapes=[pltpu.VMEM((d, tm), F32)],
        compiler_params=_cparams(("parallel", "arbitrary")),
        name="peer_dense",
    )(h2t, a, ni, r1, bv, u, vt)


def _final_kernel(x_ref, peer_ref, mod_ref, g_ref, o_ref):
    x = x_ref[...] + mod_ref[5:6, :] * peer_ref[...]
    o_ref[...] = _rms(x) * g_ref[...]


def _final_call(x, peer, mod, fg, n_ctx_tiles):
    b, t, d = x.shape
    nl = t // TILE - n_ctx_tiles
    tok = lambda bi, i: (bi, i + n_ctx_tiles, 0)
    return pl.pallas_call(
        _final_kernel,
        grid=(b, nl),
        in_specs=[pl.BlockSpec((None, TILE, d), tok), pl.BlockSpec((None, TILE, d), tok),
                  pl.BlockSpec((None, None, 6, d), lambda bi, i: (bi, 1, 0, 0)),
                  pl.BlockSpec(fg.shape, lambda bi, i: (0, 0))],
        out_specs=pl.BlockSpec((None, TILE, d), lambda bi, i: (bi, i, 0)),
        out_shape=jax.ShapeDtypeStruct((b, nl * TILE, d), F32),
        compiler_params=_cparams(("parallel", "parallel")),
        name="final_norm",
    )(x, peer, mod, fg)


def _rope_tables(seq, n_ctx):
    rows = seq // GRID_W
    axis_rot = DIFF_DH // 2
    row = jnp.repeat(jnp.arange(rows), GRID_W).astype(F32)
    col = jnp.tile(jnp.arange(GRID_W), rows).astype(F32)
    inv = ROPE_BASE ** (-jnp.arange(0, axis_rot, 2, dtype=F32) / axis_rot)
    ang = jnp.concatenate([row[:, None] * inv, col[:, None] * inv], axis=-1)
    cos = jnp.repeat(jnp.cos(ang), 2, axis=-1)
    sin = jnp.repeat(jnp.sin(ang), 2, axis=-1)
    even = (jnp.arange(DIFF_DH) % 2 == 0)[None, :]
    sa = jnp.where(even, -sin, 0.0)
    sb = jnp.where(even, 0.0, sin)
    rep = LANES // DIFF_DH

    def full(tab, ctx_val):
        tab = jnp.tile(tab, (1, rep))
        return jnp.concatenate([jnp.full((n_ctx, LANES), ctx_val, F32), tab], axis=0)

    return full(cos, 1.0), full(sa, 0.0), full(sb, 0.0)


def kernel(x, c, ctx, c_ctx, ada_w, ada_b, norm1_g, norm2_g, w_in, conv_w, conv_b, conv_ln_g, conv_ln_b, mlstm_gate_b, mlstm_norm_g, diff_lambda, diff_norm_g, w_out, peer_wq, peer_keys, peer_u, peer_v, final_g):
    b, seq, d = x.shape
    n_ctx = ctx.shape[1]
    depth = ada_w.shape[0]
    cw, mw, dw = d // 4, d // 4, d // 2
    ng = 4 * MLSTM_HEADS
    assert n_ctx == TILE and seq % ATT_TQ == 0 and seq % GRID_W == 0
    assert w_in.shape[-1] == 2 * cw + 4 * mw + ng + 3 * dw
    assert peer_keys.shape[1:] == (PEER_HEADS, 2, N_KEYS, d // PEER_HEADS // 2)
    assert conv_w.shape[1] == CONV_K and (b * (seq + n_ctx)) % PEER_TM == 0

    rows = -(-(b + 1) // SUBLANES) * SUBLANES
    cvec = jnp.zeros((rows, d), F32).at[:b].set(c).at[b].set(c_ctx)
    mods = _ada_call(cvec, ada_w, ada_b)
    mod_lat = mods[:, :b].reshape(depth, b, 1, 6, d)
    mod_ctx = jnp.broadcast_to(mods[:, b].reshape(depth, 1, 1, 6, d), (depth, b, 1, 6, d))
    mod_all = jnp.concatenate([mod_ctx, mod_lat], axis=2)

    cos, sa, sb = _rope_tables(seq, n_ctx)
    g0 = 2 * cw + 4 * mw
    w_main = jnp.concatenate([w_in[:, :, :g0], w_in[:, :, g0 + ng:g0 + ng + 2 * dw], w_in[:, :, g0:g0 + ng],
                              jnp.zeros((depth, d, LANES - ng), F32)], axis=-1).astype(BF16)
    w_dvt = jnp.swapaxes(w_in[:, :, g0 + ng + 2 * dw:], 1, 2).astype(BF16)
    gbias = jnp.pad(mlstm_gate_b.reshape(depth, 1, ng), ((0, 0), (0, 0), (0, LANES - ng)))
    w_out_b = w_out.astype(BF16)
    wq_b = peer_wq.astype(BF16)
    hw = d // PEER_HEADS
    kz = jnp.zeros((depth, PEER_HEADS, N_KEYS, hw // 2), F32)
    kh = jnp.concatenate([jnp.concatenate([peer_keys[:, :, 0], kz], axis=-1),
                          jnp.concatenate([kz, peer_keys[:, :, 1]], axis=-1)], axis=2).astype(BF16)
    u_b = peer_u.astype(BF16)
    vt_b = jnp.swapaxes(peer_v, 1, 2).astype(BF16)
    mng = jnp.tile(mlstm_norm_g, (1, MLSTM_HEADS))

    xs = jnp.concatenate([ctx, x], axis=1)
    t = xs.shape[1]
    peer = None
    for l in range(depth):
        lam_init = 0.8 - 0.6 * math.exp(-0.3 * l)
        xs, u, mqkv, mo, gcol, grow, dq, dk, dvt = _inproj_call(
            xs, peer, mod_all[l - 1] if l else None, mod_all[l], norm1_g[l][None], w_main[l],
            w_dvt[l], gbias[l], ng, cos, sa, sb)
        conv = _conv_call(u, conv_w[l, :, 0, :], conv_b[l][None], conv_ln_g[l][None], conv_ln_b[l][None])
        gcol_d = gcol.reshape(b, t, 2, ng // 2).transpose(0, 2, 1, 3)
        grow_d = grow.reshape(b, 2, ng // 2, t)
        hf, hb = _mlstm_call(mqkv, gcol_d, grow_d)
        dyc = _attn_call(dq, dk, dvt, diff_lambda[l], diff_norm_g[l][:, None], lam_init, ctx=True)
        dyl = _attn_call(dq, dk, dvt, diff_lambda[l], diff_norm_g[l][:, None], lam_init, ctx=False)
        xs, h2, h2t = _outproj_call(xs, conv, hf, hb, mo, dyc, dyl, mod_all[l], mng[l][None], norm2_g[l][None],
                                    w_out_b[l])
        h2f = h2.reshape(b * t, d)
        a, ni, r1, bv = _peer_sel_call(h2f, wq_b[l], kh[l])
        peer = _peer_dense_call(h2t, a, ni, r1, bv, u_b[l], vt_b[l]).reshape(b, t, d)
    return _final_call(xs, peer, mod_all[depth - 1], final_g[None], n_ctx // TILE)
```

```python
import functools
import math

import jax
import jax.numpy as jnp
from jax import lax
from jax.experimental import pallas as pl
from jax.experimental.pallas import tpu as pltpu

F32 = jnp.float32
BF16 = jnp.bfloat16
HIGHEST = lax.Precision.HIGHEST

GRID_W = 64
EPS = 1e-6
CONV_K = 31
MLSTM_DH = 64
MLSTM_HEADS = 4
DIFF_DH = 64
DIFF_DV = 128
DIFF_HEADS = 4
ROPE_BASE = 10000.0
PEER_HEADS = 8
N_KEYS = 128
PEER_TOPK = 16

LANES = 128
SUBLANES = 8
TILE = 256
CONV_HALO = 16
ATT_KC = 1024
ATT_TQ = 512
PEER_SEL_TILE = 256
PEER_TM = 1024
BF16_SUBLANES = 16
PEER_EC = 2048
VMEM_LIMIT = 56 * 1024 * 1024

NT_DIMS = (((1,), (1,)), ((), ()))


def _cparams(sem):
    return pltpu.CompilerParams(dimension_semantics=sem, vmem_limit_bytes=VMEM_LIMIT)


def _rms(x, eps=EPS):
    return x * lax.rsqrt(jnp.mean(x * x, axis=-1, keepdims=True) + eps)


def _log_sigmoid(x):
    return jnp.minimum(x, 0.0) - jnp.log(1.0 + jnp.exp(-jnp.abs(x)))


def _ada_kernel(c_ref, w_ref, b_ref, o_ref):
    c = c_ref[...]
    s = (c * jax.nn.sigmoid(c)).astype(BF16)
    o_ref[...] = jnp.dot(s, w_ref[...].astype(BF16), preferred_element_type=F32) + b_ref[...]


def _ada_call(cvec, ada_w, ada_b):
    depth, d, n = ada_w.shape
    tn = 1536
    rows = cvec.shape[0]
    return pl.pallas_call(
        _ada_kernel,
        grid=(depth, n // tn),
        in_specs=[pl.BlockSpec((rows, d), lambda l, j: (0, 0)),
                  pl.BlockSpec((None, d, tn), lambda l, j: (l, 0, j)),
                  pl.BlockSpec((None, 1, tn), lambda l, j: (l, 0, j))],
        out_specs=pl.BlockSpec((None, rows, tn), lambda l, j: (l, 0, j)),
        out_shape=jax.ShapeDtypeStruct((depth, rows, n), F32),
        compiler_params=_cparams(("parallel", "parallel")),
        name="ada_mod",
    )(cvec, ada_w, ada_b.reshape(depth, 1, n))


def _rope(t, c, sa, sb):
    w = t.shape[1]
    rep = w // LANES
    c, sa, sb = (jnp.concatenate([z] * rep, axis=1) for z in (c, sa, sb))
    return t * c + pltpu.roll(t, w - 1, 1) * sa + pltpu.roll(t, 1, 1) * sb


def _inproj_kernel(*refs, has_peer, cw, mw, bb):
    if has_peer:
        x_ref, peer_ref, modp_ref = refs[:3]
        refs = refs[3:]
    (mod_ref, n1g_ref, wm_ref, wdvt_ref, gb_ref, cos_ref, sa_ref, sb_ref,
     xo_ref, u_ref, mqkv_ref, mo_ref, gcol_ref, grow_ref, dq_ref, dk_ref, dvt_ref) = refs[-17:]
    if not has_peer:
        x_ref = refs[0]
    hs = []
    for k in range(bb):
        x = x_ref[k]
        if has_peer:
            x = x + modp_ref[k, 5:6, :] * peer_ref[k]
        xo_ref[k] = x
        hs.append(_rms(x) * n1g_ref[...] * (1.0 + mod_ref[k, 1:2, :]) + mod_ref[k, 0:1, :])
    hb = jnp.concatenate(hs, axis=0).astype(BF16)

    def proj(lo, hi):
        return jnp.dot(hb, wm_ref[:, lo:hi], preferred_element_type=F32)

    def rows(z, k):
        return z[k * TILE:(k + 1) * TILE]

    a = proj(0, 2 * cw)
    u = a[:, :cw] * jax.nn.sigmoid(a[:, cw:])
    o = 2 * cw
    mq = proj(o, o + mw).astype(BF16)
    mk = (proj(o + mw, o + 2 * mw) * (MLSTM_DH ** -0.5)).astype(BF16)
    mv = proj(o + 2 * mw, o + 3 * mw).astype(BF16)
    mo = proj(o + 3 * mw, o + 4 * mw)
    o = o + 4 * mw
    dw = dq_ref.shape[2]
    c, sa, sb = cos_ref[...], sa_ref[...], sb_ref[...]
    pq = proj(o, o + dw)
    pk = proj(o + dw, o + 2 * dw)
    dvt = lax.dot_general(wdvt_ref[...], hb, NT_DIMS, preferred_element_type=F32).astype(BF16)
    o = o + 2 * dw
    ng = gcol_ref.shape[2]
    g = proj(o, o + LANES) + gb_ref[...]
    cidx = lax.broadcasted_iota(jnp.int32, g.shape, 1)
    g = jnp.where((cidx // MLSTM_HEADS) % 2 == 1, _log_sigmoid(g), g)
    for k in range(bb):
        u_ref[k] = rows(u, k)
        mqkv_ref[k, :, 0:mw] = rows(mq, k)
        mqkv_ref[k, :, mw:2 * mw] = rows(mk, k)
        mqkv_ref[k, :, 2 * mw:3 * mw] = rows(mv, k)
        mo_ref[k] = rows(mo, k)
        dq_ref[k] = (_rope(rows(pq, k), c, sa, sb) * (DIFF_DH ** -0.5 * math.log2(math.e))).astype(BF16)
        dk_ref[k] = _rope(rows(pk, k), c, sa, sb).astype(BF16)
        dvt_ref[k] = dvt[:, k * TILE:(k + 1) * TILE]
        gk = rows(g, k)
        gcol_ref[k] = gk[:, :ng]
        grow_ref[k] = gk.T[:ng, :]


def _batch_block(b):
    return 2 if b % 2 == 0 else 1


def _inproj_call(x, peer, modp, mod, n1g, wm, wdvt, gb, ng, cos, sa, sb):
    b, t, d = x.shape
    nt = t // TILE
    bb = _batch_block(b)
    cw = d // 4
    mw = d // 4
    dw = d // 2
    has_peer = peer is not None
    tok = lambda bi, i: (bi, i, 0)
    modspec = pl.BlockSpec((bb, None, 6, d), lambda bi, i: (bi, jnp.minimum(i, 1), 0, 0))
    full2 = lambda arr: pl.BlockSpec(arr.shape, lambda bi, i: (0, 0))
    in_specs = [pl.BlockSpec((bb, TILE, d), tok)]
    args = [x]
    if has_peer:
        in_specs += [pl.BlockSpec((bb, TILE, d), tok), modspec]
        args += [peer, modp]
    in_specs += [modspec, full2(n1g), full2(wm), full2(wdvt), full2(gb)]
    args += [mod, n1g, wm, wdvt, gb]
    in_specs += [pl.BlockSpec((TILE, LANES), lambda bi, i: (i, 0))] * 3
    args += [cos, sa, sb]
    out_shape = [jax.ShapeDtypeStruct((b, t, d), F32),
                 jax.ShapeDtypeStruct((b, t, cw), F32),
                 jax.ShapeDtypeStruct((b, t, 3 * mw), BF16),
                 jax.ShapeDtypeStruct((b, t, mw), F32),
                 jax.ShapeDtypeStruct((b, t, ng), F32),
                 jax.ShapeDtypeStruct((b, ng, t), F32),
                 jax.ShapeDtypeStruct((b, t, dw), BF16),
                 jax.ShapeDtypeStruct((b, t, dw), BF16),
                 jax.ShapeDtypeStruct((b, dw, t), BF16)]
    out_specs = [pl.BlockSpec((bb, TILE, d), tok),
                 pl.BlockSpec((bb, TILE, cw), tok),
                 pl.BlockSpec((bb, TILE, 3 * mw), tok),
                 pl.BlockSpec((bb, TILE, mw), tok),
                 pl.BlockSpec((bb, TILE, ng), tok),
                 pl.BlockSpec((bb, ng, TILE), lambda bi, i: (bi, 0, i)),
                 pl.BlockSpec((bb, TILE, dw), tok),
                 pl.BlockSpec((bb, TILE, dw), tok),
                 pl.BlockSpec((bb, dw, TILE), lambda bi, i: (bi, 0, i))]
    return pl.pallas_call(
        functools.partial(_inproj_kernel, has_peer=has_peer, cw=cw, mw=mw, bb=bb),
        grid=(b // bb, nt), in_specs=in_specs, out_specs=out_specs, out_shape=out_shape,
        compiler_params=_cparams(("parallel", "parallel")),
        name="in_proj",
    )(*args)


def _conv_kernel(up_ref, uc_ref, un_ref, w_ref, b_ref, lg_ref, lb_ref, o_ref, ext_ref, sh_ref):
    i = pl.program_id(1)
    nt = pl.num_programs(1)
    lm = jnp.where(i >= 2, 1.0, 0.0)
    rm = jnp.where(jnp.logical_and(i >= 1, i < nt - 1), 1.0, 0.0)
    hl = CONV_HALO
    ext_ref[0:hl, :] = up_ref[TILE - hl:TILE, :] * lm
    ext_ref[hl:hl + TILE, :] = uc_ref[...]
    ext_ref[hl + TILE:2 * hl + TILE, :] = un_ref[0:hl, :] * rm
    off = hl - CONV_K // 2
    span = sh_ref.shape[1]
    acc = jnp.zeros(uc_ref.shape, F32)
    for r in range(SUBLANES):
        taps = [k for k in range(CONV_K) if (off + k) % SUBLANES == r]
        if not taps:
            continue
        sh_ref[r] = ext_ref[r:r + span, :]
        for k in taps:
            q = (off + k) // SUBLANES * SUBLANES
            acc = acc + w_ref[k:k + 1, :] * sh_ref[r, q:q + TILE, :]
    y = acc + b_ref[...]
    mu = jnp.mean(y, axis=-1, keepdims=True)
    yc = y - mu
    var = jnp.mean(yc * yc, axis=-1, keepdims=True)
    z = yc * lax.rsqrt(var + EPS) * lg_ref[...] + lb_ref[...]
    o_ref[...] = (z * jax.nn.sigmoid(z)).astype(o_ref.dtype)


def _conv_call(u, w, bias, lg, lb):
    b, t, cw = u.shape
    nt = t // TILE
    full2 = lambda arr: pl.BlockSpec(arr.shape, lambda bi, i: (0, 0))
    return pl.pallas_call(
        _conv_kernel,
        grid=(b, nt),
        in_specs=[pl.BlockSpec((None, TILE, cw), lambda bi, i: (bi, jnp.maximum(i - 1, 0), 0)),
                  pl.BlockSpec((None, TILE, cw), lambda bi, i: (bi, i, 0)),
                  pl.BlockSpec((None, TILE, cw), lambda bi, i: (bi, jnp.minimum(i + 1, nt - 1), 0)),
                  full2(w), full2(bias), full2(lg), full2(lb)],
        out_specs=pl.BlockSpec((None, TILE, cw), lambda bi, i: (bi, i, 0)),
        out_shape=jax.ShapeDtypeStruct((b, t, cw), BF16),
        scratch_shapes=[pltpu.VMEM((TILE + 2 * CONV_HALO, cw), F32),
                        pltpu.VMEM((SUBLANES, TILE + 2 * CONV_HALO - SUBLANES, cw), F32)],
        compiler_params=_cparams(("parallel", "parallel")),
        name="conv_module",
    )(u, u, u, w, bias, lg, lb)


def _mlstm_chunk(qkv_ref, gc_ref, gr_ref, h_ref, c_ref, n_ref, m_ref, fwd):
    tc = qkv_ref.shape[0]
    mw = qkv_ref.shape[1] // 3
    nh = MLSTM_HEADS
    dh = MLSTM_DH
    row = lax.broadcasted_iota(jnp.int32, (tc, tc), 0)
    col = lax.broadcasted_iota(jnp.int32, (tc, tc), 1)
    tri = row >= col if fwd else row <= col
    vis = row <= col if fwd else row >= col
    trif = tri.astype(F32)
    gc = gc_ref[...]
    gr = gr_ref[...]
    bcol = jnp.dot(trif, gc, precision=HIGHEST, preferred_element_type=F32)
    brow = lax.dot_general(gr, trif, NT_DIMS, precision=HIGHEST, preferred_element_type=F32)
    bl = jnp.sum(gr, axis=1, keepdims=True)
    src = gc[:, 0:nh] - bcol[:, nh:2 * nh]

    q = qkv_ref[:, 0:mw]
    k = qkv_ref[:, mw:2 * mw]
    v = qkv_ref[:, 2 * mw:3 * mw]
    qf = q.astype(F32)
    vt = v.astype(F32).T
    vt_b = vt.astype(BF16)
    lane_head = lax.broadcasted_iota(jnp.int32, (tc, mw), 1) // dh
    rhead = lax.broadcasted_iota(jnp.int32, (mw, 1), 0) // dh
    chead = lax.broadcasted_iota(jnp.int32, (1, mw), 1) // dh
    cb = c_ref[...]
    n_old = n_ref[...]
    inter_c = lax.dot_general(cb.astype(BF16), q, NT_DIMS, preferred_element_type=F32)
    zrow = jnp.zeros((1, mw), F32)
    n4 = jnp.concatenate([jnp.where(chead == h, n_old, 0.0) for h in range(nh)] + [zrow] * (SUBLANES - nh),
                         axis=0)
    qn = lax.dot_general(n4, qf, NT_DIMS, precision=HIGHEST, preferred_element_type=F32)

    outs, vws, wks = [], [], []
    decay_col = jnp.zeros((mw, 1), F32)
    decay_row = jnp.zeros((1, mw), F32)
    for h in range(nh):
        hr = slice(h * dh, (h + 1) * dh)
        mh = m_ref[h:h + 1, 0:1]
        bt = brow[nh + h:nh + h + 1, :]
        dlog = jnp.where(vis, bt + src[:, h:h + 1], -jnp.inf)
        inter = bt + mh
        mt = jnp.maximum(inter, jnp.max(dlog, axis=0, keepdims=True))
        dwt = jnp.exp(dlog - mt)
        iw = jnp.exp(inter - mt)
        qh = jnp.where(lane_head == h, qf, 0.0).astype(BF16)
        s = lax.dot_general(k, qh, NT_DIMS, preferred_element_type=F32) * dwt
        num = jnp.dot(vt_b[hr], s.astype(BF16), preferred_element_type=F32)
        den = jnp.sum(s, axis=0, keepdims=True) + iw * qn[h:h + 1, :]
        denom = jnp.maximum(jnp.abs(den), jnp.exp(-mt))
        outs.append((num + iw * inter_c[hr]) / denom)
        blh = bl[nh + h:nh + h + 1, :]
        wlog = blh - bt + gr[h:h + 1, :]
        mn = jnp.maximum(blh + mh, jnp.max(wlog, axis=1, keepdims=True))
        decay = jnp.exp(blh + mh - mn)
        wk = jnp.exp(wlog - mn)
        wks.append(wk)
        vws.append(vt[hr] * wk)
        decay_col = jnp.where(rhead == h, decay, decay_col)
        decay_row = jnp.where(chead == h, decay, decay_row)
        m_ref[h:h + 1, :] = jnp.broadcast_to(mn, (1, m_ref.shape[1]))
    h_ref[...] = jnp.concatenate(outs, axis=0).T

    vw = jnp.concatenate(vws, axis=0).astype(BF16)
    upd = jnp.dot(vw, k, preferred_element_type=F32)
    c_ref[...] = decay_col * cb + jnp.where(rhead == chead, upd, 0.0)
    wk4 = jnp.concatenate(wks + [jnp.zeros((1, tc), F32)] * (SUBLANES - nh), axis=0)
    nk = jnp.dot(wk4, k.astype(F32), precision=HIGHEST, preferred_element_type=F32)
    n_add = zrow
    for h in range(nh):
        n_add = jnp.where(chead == h, nk[h:h + 1, :], n_add)
    n_ref[...] = decay_row * n_old + n_add


def _mlstm_kernel(qkvf_ref, gcf_ref, grf_ref, qkvb_ref, gcb_ref, grb_ref, hf_ref, hb_ref,
                  c_ref, n_ref, m_ref):
    @pl.when(pl.program_id(1) == 0)
    def _():
        c_ref[...] = jnp.zeros_like(c_ref)
        n_ref[...] = jnp.zeros_like(n_ref)
        m_ref[...] = jnp.zeros_like(m_ref)

    _mlstm_chunk(qkvf_ref, gcf_ref, grf_ref, hf_ref, c_ref.at[0], n_ref.at[0], m_ref.at[0], True)
    _mlstm_chunk(qkvb_ref, gcb_ref, grb_ref, hb_ref, c_ref.at[1], n_ref.at[1], m_ref.at[1], False)


def _mlstm_call(mqkv, gcol, grow):
    b, t, w3 = mqkv.shape
    mw = w3 // 3
    nt = t // TILE
    ng = gcol.shape[-1]

    def rev(j):
        return jnp.where(j == 0, 0, nt - j)

    hshape = jax.ShapeDtypeStruct((b, t, mw), F32)
    return pl.pallas_call(
        _mlstm_kernel,
        grid=(b, nt),
        in_specs=[pl.BlockSpec((None, TILE, w3), lambda bi, j: (bi, j, 0)),
                  pl.BlockSpec((None, None, TILE, ng), lambda bi, j: (bi, 0, j, 0)),
                  pl.BlockSpec((None, None, ng, TILE), lambda bi, j: (bi, 0, 0, j)),
                  pl.BlockSpec((None, TILE, w3), lambda bi, j: (bi, rev(j), 0)),
                  pl.BlockSpec((None, None, TILE, ng), lambda bi, j: (bi, 1, rev(j), 0)),
                  pl.BlockSpec((None, None, ng, TILE), lambda bi, j: (bi, 1, 0, rev(j)))],
        out_specs=[pl.BlockSpec((None, TILE, mw), lambda bi, j: (bi, j, 0)),
                   pl.BlockSpec((None, TILE, mw), lambda bi, j: (bi, rev(j), 0))],
        out_shape=[hshape, hshape],
        scratch_shapes=[pltpu.VMEM((2, mw, mw), F32), pltpu.VMEM((2, 1, mw), F32),
                        pltpu.VMEM((2, SUBLANES, LANES), F32)],
        compiler_params=_cparams(("arbitrary", "arbitrary")),
        name="mlstm_scan",
    )(mqkv, gcol, grow, mqkv, gcol, grow)


def _attn_kernel(*refs, lam_init, kc, nq):
    dl_ref, ng_ref = refs[:2]
    q_refs = refs[2:2 + nq]
    k_ref, vt_ref, o_ref, s_ref, m_ref, l_ref, acc_ref = refs[2 + nq:]
    n_keys = k_ref.shape[0]
    q = jnp.concatenate([r[...] for r in q_refs], axis=0) if nq > 1 else q_refs[0][...]
    lane = lax.broadcasted_iota(jnp.int32, q.shape, 1)
    zero = jnp.zeros_like(q)
    qz = (jnp.where(lane < DIFF_DH, q, zero), jnp.where(lane >= DIFF_DH, q, zero))
    m_ref[...] = jnp.full(m_ref.shape, -jnp.inf, F32)
    l_ref[...] = jnp.zeros_like(l_ref)
    acc_ref[...] = jnp.zeros_like(acc_ref)

    def scores(start, slot):
        kb = k_ref[pl.ds(start, kc), :]
        for c in range(2):
            s_ref[slot, c] = lax.dot_general(kb, qz[c], NT_DIMS, preferred_element_type=F32)

    def softmax_pv(start, slot):
        vtb = vt_ref[:, pl.ds(start, kc)]
        for c in range(2):
            s = s_ref[slot, c]
            m_old = m_ref[c]
            m_new = jnp.maximum(m_old, jnp.max(s, axis=0, keepdims=True))
            alpha = jnp.exp2(m_old - m_new)
            p = jnp.exp2(s - m_new)
            l_ref[c] = alpha * l_ref[c] + jnp.sum(p, axis=0, keepdims=True)
            acc_ref[c] = alpha * acc_ref[c] + jnp.dot(vtb, p.astype(BF16), preferred_element_type=F32)
            m_ref[c] = m_new

    n = n_keys // kc
    pairs = (n - 1) // 2
    scores(0, 0)
    if pairs:
        def body(jj, carry):
            base = pl.multiple_of(jj * (2 * kc), TILE)
            scores(base + kc, 1)
            softmax_pv(base, 0)
            scores(base + 2 * kc, 0)
            softmax_pv(base + kc, 1)
            return carry

        lax.fori_loop(0, pairs, body, 0)
    done = 2 * pairs
    if n - done == 2:
        scores((done + 1) * kc, 1)
    softmax_pv(done * kc, 0)
    if n - done == 2:
        softmax_pv((done + 1) * kc, 1)

    dl = dl_ref[...]
    lam = (jnp.exp(jnp.sum(dl[0:1] * dl[1:2], axis=1, keepdims=True))
           - jnp.exp(jnp.sum(dl[2:3] * dl[3:4], axis=1, keepdims=True)) + lam_init)
    ot = acc_ref[0] / l_ref[0] - lam * (acc_ref[1] / l_ref[1])
    ms = jnp.mean(ot * ot, axis=0, keepdims=True)
    y = ot * lax.rsqrt(ms + EPS) * (ng_ref[...] * (1.0 - lam_init))
    o_ref[...] = y.T.astype(o_ref.dtype)


def _attn_key_chunk(t):
    return max(k for k in range(TILE, ATT_KC + 1, TILE) if t % k == 0)


def _attn_call(dq, dk, dvt, dlam, dng, lam_init, *, ctx):
    b, t, w = dq.shape
    nh = w // DIFF_DV
    if ctx:
        n_keys, nq, steps, q0 = TILE, 1, 1, 0
    else:
        n_keys, nq, q0 = t, ATT_TQ // TILE, 1
        steps = (t - TILE) // ATT_TQ
    tq = nq * TILE
    kc = _attn_key_chunk(n_keys)
    kern = functools.partial(_attn_kernel, lam_init=lam_init, kc=kc, nq=nq)
    q_specs = [pl.BlockSpec((None, TILE, DIFF_DV), functools.partial(
        lambda bi, h, i, j: (bi, q0 + nq * i + j, h), j=j)) for j in range(nq)]
    return pl.pallas_call(
        kern,
        grid=(b, nh, steps),
        in_specs=[pl.BlockSpec(dlam.shape, lambda bi, h, i: (0, 0)),
                  pl.BlockSpec(dng.shape, lambda bi, h, i: (0, 0))] + q_specs + [
                  pl.BlockSpec((None, n_keys, DIFF_DV), lambda bi, h, i: (bi, 0, h)),
                  pl.BlockSpec((None, DIFF_DV, n_keys), lambda bi, h, i: (bi, h, 0))],
        out_specs=pl.BlockSpec((None, tq, DIFF_DV), lambda bi, h, i: (bi, i, h)),
        out_shape=jax.ShapeDtypeStruct((b, steps * tq, w), BF16),
        scratch_shapes=[pltpu.VMEM((2, 2, kc, tq), F32),
                        pltpu.VMEM((2, 1, tq), F32), pltpu.VMEM((2, 1, tq), F32),
                        pltpu.VMEM((2, DIFF_DV, tq), F32)],
        compiler_params=_cparams(("parallel", "parallel", "arbitrary")),
        name="diff_attn_ctx" if ctx else "diff_attn",
    )(dlam, dng, *([dq] * nq), dk, dvt)


def _outproj_kernel(x_ref, conv_ref, hf_ref, hb_ref, mo_ref, dyc_ref, dyl_ref, mod_ref, mng_ref, n2g_ref,
                    wo_ref, xo_ref, h2_ref, h2t_ref):
    cw = conv_ref.shape[1]
    mw = hf_ref.shape[1]
    hh = hf_ref[...] + hb_ref[...]
    r = lax.broadcasted_iota(jnp.int32, (mw, mw), 0) // MLSTM_DH
    c = lax.broadcasted_iota(jnp.int32, (mw, mw), 1) // MLSTM_DH
    gm = jnp.where(r == c, 1.0 / MLSTM_DH, 0.0)
    ms = jnp.dot(hh * hh, gm, precision=HIGHEST, preferred_element_type=F32)
    ym = hh * lax.rsqrt(ms + EPS) * mng_ref[...] * jax.nn.sigmoid(mo_ref[...])
    dy = jnp.where(pl.program_id(1) == 0, dyc_ref[...], dyl_ref[...])
    o = (jnp.dot(conv_ref[...], wo_ref[0:cw, :], preferred_element_type=F32)
         + jnp.dot(ym.astype(BF16), wo_ref[cw:cw + mw, :], preferred_element_type=F32)
         + jnp.dot(dy, wo_ref[cw + mw:, :], preferred_element_type=F32))
    x = x_ref[...] + mod_ref[2:3, :] * o
    xo_ref[...] = x
    h2 = _rms(x) * n2g_ref[...] * (1.0 + mod_ref[4:5, :]) + mod_ref[3:4, :]
    h2_ref[...] = h2.astype(BF16)
    h2t_ref[...] = h2.T.astype(BF16)


def _outproj_call(x, conv, hf, hb, mo, dyc, dyl, mod, mng, n2g, wo):
    b, t, d = x.shape
    nt = t // TILE
    cw, mw, dw = conv.shape[-1], mo.shape[-1], dyl.shape[-1]
    tok = lambda bi, i: (bi, i, 0)
    full2 = lambda arr: pl.BlockSpec(arr.shape, lambda bi, i: (0, 0))
    return pl.pallas_call(
        _outproj_kernel,
        grid=(b, nt),
        in_specs=[pl.BlockSpec((None, TILE, d), tok),
                  pl.BlockSpec((None, TILE, cw), tok),
                  pl.BlockSpec((None, TILE, mw), tok),
                  pl.BlockSpec((None, TILE, mw), tok),
                  pl.BlockSpec((None, TILE, mw), tok),
                  pl.BlockSpec((None, TILE, dw), lambda bi, i: (bi, 0, 0)),
                  pl.BlockSpec((None, TILE, dw), lambda bi, i: (bi, jnp.maximum(i - 1, 0), 0)),
                  pl.BlockSpec((None, None, 6, d), lambda bi, i: (bi, jnp.minimum(i, 1), 0, 0)),
                  full2(mng), full2(n2g), full2(wo)],
        out_specs=[pl.BlockSpec((None, TILE, d), tok), pl.BlockSpec((None, TILE, d), tok),
                   pl.BlockSpec((d, TILE), lambda bi, i: (0, bi * nt + i))],
        out_shape=[jax.ShapeDtypeStruct((b, t, d), F32), jax.ShapeDtypeStruct((b, t, d), BF16),
                   jax.ShapeDtypeStruct((d, b * t), BF16)],
        compiler_params=_cparams(("parallel", "parallel")),
        name="out_proj",
    )(x, conv, hf, hb, mo, dyc, dyl, mod, mng, n2g, wo)


def _top_k_rows(s, k):
    idx = lax.broadcasted_iota(jnp.int32, s.shape, 0).astype(F32)
    vals, ids = [], []
    cur = s
    for _ in range(k):
        m = jnp.max(cur, axis=0, keepdims=True)
        am = jnp.min(jnp.where(cur == m, idx, float(s.shape[0])), axis=0, keepdims=True)
        vals.append(m)
        ids.append(am)
        cur = jnp.where(idx == am, -jnp.inf, cur)
    return vals, ids


def _select_exact(s0, s1):
    kk = PEER_TOPK
    tt = s0.shape[1]
    iota_k = lax.broadcasted_iota(jnp.int32, (kk, tt), 0).astype(F32)
    iota_n = lax.broadcasted_iota(jnp.int32, (N_KEYS, tt), 0).astype(F32)
    v0, i0 = _top_k_rows(s0, kk)
    v1, i1 = _top_k_rows(s1, kk)
    st0 = jnp.concatenate(v0, axis=0)
    n = jnp.zeros((kk, tt), F32)
    hv = st0 + v1[0]
    mx = v0[0] + v1[0]
    zsum = jnp.zeros((1, tt), F32)
    for _ in range(kk):
        m = jnp.max(hv, axis=0, keepdims=True)
        a_star = jnp.min(jnp.where(hv == m, iota_k, float(kk)), axis=0, keepdims=True)
        sel = iota_k == a_star
        zsum = zsum + jnp.exp(m - mx)
        n = n + jnp.where(sel, 1.0, 0.0)
        nxt = jnp.full((kk, tt), -jnp.inf, F32)
        for b in range(1, kk):
            nxt = jnp.where(n == float(b), v1[b], nxt)
        hv = jnp.where(sel, st0 + nxt, hv)
    ni = jnp.zeros((N_KEYS, tt), F32)
    r1 = jnp.full((N_KEYS, tt), float(kk), F32)
    for a in range(kk):
        ni = jnp.where(iota_n == i0[a], n[a:a + 1, :], ni)
        r1 = jnp.where(iota_n == i1[a], float(a), r1)
    return jnp.exp(s0 - v0[0]) * (0.5 / zsum), ni, r1, jnp.exp(s1 - v1[0])


def _top_k_ranks(s, k):
    cur = s
    rank = jnp.full(s.shape, float(k), F32)
    vals = []
    for j in range(k):
        m = jnp.max(cur, axis=0, keepdims=True)
        eq = cur == m
        vals.append(m)
        rank = jnp.where(eq, float(j), rank)
        cur = jnp.where(eq, -jnp.inf, cur)
    cnt = jnp.sum(jnp.where(rank < float(k), 1.0, 0.0), axis=0, keepdims=True)
    return vals, rank, cnt


def _select_fast(s0, s1):
    kk = PEER_TOPK
    tt = s0.shape[1]
    v0, rank0, cnt0 = _top_k_ranks(s0, kk)
    v1, rank1, cnt1 = _top_k_ranks(s1, kk)
    half = SUBLANES
    st1 = jnp.concatenate(v1, axis=0)
    st1_h = st1[0:half]
    riota = lax.broadcasted_iota(jnp.int32, (half, tt), 0)
    blocks = [v0[0] + st1]
    for a in range(1, half):
        blocks.append(jnp.where(riota < kk // (a + 1), v0[a] + st1_h, -jnp.inf))
    blocks.append(jnp.concatenate(v0[half:], axis=0) + v1[0])
    p = jnp.concatenate(blocks, axis=0)
    cur = p
    for _ in range(kk):
        m = jnp.max(cur, axis=0, keepdims=True)
        cur = jnp.where(cur == m, -jnp.inf, cur)
    picked = jnp.where(cur != p, 1.0, 0.0)
    mx = v0[0] + v1[0]
    zsum = jnp.sum(picked * jnp.exp(p - mx), axis=0, keepdims=True)
    cntm = jnp.sum(picked, axis=0, keepdims=True)
    n_rows = [jnp.sum(picked[0:kk], axis=0, keepdims=True)]
    for a in range(1, half):
        lo = kk + (a - 1) * half
        n_rows.append(jnp.sum(picked[lo:lo + half], axis=0, keepdims=True))
    last = kk + (half - 1) * half
    n_rows += [picked[last + r:last + r + 1] for r in range(kk - half)]
    ni = jnp.zeros((N_KEYS, tt), F32)
    for a in range(kk):
        ni = jnp.where(rank0 == float(a), n_rows[a], ni)
    want = float(kk)
    bad = jnp.where((cnt0 != want) | (cnt1 != want) | (cntm != want), 1.0, 0.0)
    return jnp.exp(s0 - v0[0]) * (0.5 / zsum), ni, rank1, jnp.exp(s1 - v1[0]), bad


def _peer_sel_kernel(h2_ref, wq_ref, kh_ref, a_ref, ni_ref, r1_ref, bv_ref):
    q = jnp.dot(h2_ref[...], wq_ref[...], preferred_element_type=F32).astype(BF16)
    hw = q.shape[1] // PEER_HEADS

    def head_scores(h):
        st_ = lax.dot_general(kh_ref[h], q[:, h * hw:(h + 1) * hw], NT_DIMS,
                              preferred_element_type=F32)
        return st_[0:N_KEYS], st_[N_KEYS:2 * N_KEYS]

    def store(h, a, ni, r1, bv):
        a_ref[h] = a
        ni_ref[h] = ni
        r1_ref[h] = r1.astype(BF16)
        bv_ref[h] = bv.astype(BF16)

    for h in range(PEER_HEADS):
        s0, s1 = head_scores(h)
        a, ni, r1, bv, bad = _select_fast(s0, s1)
        store(h, a, ni, r1, bv)

        @pl.when(jnp.max(bad) > 0.0)
        def _(h=h, s0=s0, s1=s1):
            store(h, *_select_exact(s0, s1))


def _peer_sel_call(h2, wq, kh):
    ntok, d = h2.shape
    nt = ntok // PEER_SEL_TILE
    shp = jax.ShapeDtypeStruct((PEER_HEADS, N_KEYS, ntok), F32)
    shp_b = jax.ShapeDtypeStruct((PEER_HEADS, N_KEYS, ntok), BF16)
    ospec = pl.BlockSpec((PEER_HEADS, N_KEYS, PEER_SEL_TILE), lambda i: (0, 0, i))
    return pl.pallas_call(
        _peer_sel_kernel,
        grid=(nt,),
        in_specs=[pl.BlockSpec((PEER_SEL_TILE, d), lambda i: (i, 0)),
                  pl.BlockSpec(wq.shape, lambda i: (0, 0)),
                  pl.BlockSpec(kh.shape, lambda i: (0, 0, 0))],
        out_specs=[ospec] * 4,
        out_shape=[shp, shp, shp_b, shp_b],
        compiler_params=_cparams(("parallel",)),
        name="peer_select",
    )(h2, wq, kh)


def _gelu_x2(x):
    return x + x * lax.erf(x * (2.0 ** -0.5))


def _peer_dense_kernel(h2t_ref, a_ref, ni_ref, r1_ref, bv_ref, u_ref, vt_ref, o_ref, acc_ref):
    c = pl.program_id(1)

    @pl.when(c == 0)
    def _():
        acc_ref[...] = jnp.zeros_like(acc_ref)

    tm = h2t_ref.shape[1]
    nib = u_ref.shape[0] // N_KEYS
    rep = N_KEYS // BF16_SUBLANES

    def rows(ref, h, ib):
        r = jnp.broadcast_to(ref[h, ib:ib + 1, :], (BF16_SUBLANES, tm)).astype(BF16)
        return jnp.concatenate([r] * rep, axis=0)

    st_ = jnp.dot(u_ref[...], h2t_ref[...], preferred_element_type=F32)
    blocks = []
    for ib in range(nib):
        g = None
        for h in range(PEER_HEADS):
            bvh = bv_ref[h]
            keep = r1_ref[h] < rows(ni_ref, h, ib)
            term = rows(a_ref, h, ib) * jnp.where(keep, bvh, jnp.zeros_like(bvh))
            g = term if g is None else g + term
        blocks.append(g * _gelu_x2(st_[ib * N_KEYS:(ib + 1) * N_KEYS].astype(BF16)))
    wt = jnp.concatenate(blocks, axis=0)
    acc_ref[...] += jnp.dot(vt_ref[...], wt, preferred_element_type=F32)

    @pl.when(c == pl.num_programs(1) - 1)
    def _():
        o_ref[...] = acc_ref[...].T


def _peer_dense_call(h2t, a, ni, r1, bv, u, vt):
    d, ntok = h2t.shape
    ne = u.shape[0]
    tm, ec = PEER_TM, PEER_EC
    ib = ec // N_KEYS
    rowspec = pl.BlockSpec((PEER_HEADS, ib, tm), lambda t, c: (0, c, t))
    colspec = pl.BlockSpec((PEER_HEADS, N_KEYS, tm), lambda t, c: (0, 0, t))
    return pl.pallas_call(
        _peer_dense_kernel,
        grid=(ntok // tm, ne // ec),
        in_specs=[pl.BlockSpec((d, tm), lambda t, c: (0, t)),
                  rowspec, rowspec, colspec, colspec,
                  pl.BlockSpec((ec, d), lambda t, c: (c, 0)),
                  pl.BlockSpec((d, ec), lambda t, c: (0, c))],
        out_specs=pl.BlockSpec((tm, d), lambda t, c: (t, 0)),
        out_shape=jax.ShapeDtypeStruct((ntok, d), F32),
        scratch_shapes=[pltpu.VMEM((d, tm), F32)],
        compiler_params=_cparams(("parallel", "arbitrary")),
        name="peer_dense",
    )(h2t, a, ni, r1, bv, u, vt)


def _final_kernel(x_ref, peer_ref, mod_ref, g_ref, o_ref):
    x = x_ref[...] + mod_ref[5:6, :] * peer_ref[...]
    o_ref[...] = _rms(x) * g_ref[...]


def _final_call(x, peer, mod, fg, n_ctx_tiles):
    b, t, d = x.shape
    nl = t // TILE - n_ctx_tiles
    tok = lambda bi, i: (bi, i + n_ctx_tiles, 0)
    return pl.pallas_call(
        _final_kernel,
        grid=(b, nl),
        in_specs=[pl.BlockSpec((None, TILE, d), tok), pl.BlockSpec((None, TILE, d), tok),
                  pl.BlockSpec((None, None, 6, d), lambda bi, i: (bi, 1, 0, 0)),
                  pl.BlockSpec(fg.shape, lambda bi, i: (0, 0))],
        out_specs=pl.BlockSpec((None, TILE, d), lambda bi, i: (bi, i, 0)),
        out_shape=jax.ShapeDtypeStruct((b, nl * TILE, d), F32),
        compiler_params=_cparams(("parallel", "parallel")),
        name="final_norm",
    )(x, peer, mod, fg)


def _rope_tables(seq, n_ctx):
    rows = seq // GRID_W
    axis_rot = DIFF_DH // 2
    row = jnp.repeat(jnp.arange(rows), GRID_W).astype(F32)
    col = jnp.tile(jnp.arange(GRID_W), rows).astype(F32)
    inv = ROPE_BASE ** (-jnp.arange(0, axis_rot, 2, dtype=F32) / axis_rot)
    ang = jnp.concatenate([row[:, None] * inv, col[:, None] * inv], axis=-1)
    cos = jnp.repeat(jnp.cos(ang), 2, axis=-1)
    sin = jnp.repeat(jnp.sin(ang), 2, axis=-1)
    even = (jnp.arange(DIFF_DH) % 2 == 0)[None, :]
    sa = jnp.where(even, -sin, 0.0)
    sb = jnp.where(even, 0.0, sin)
    rep = LANES // DIFF_DH

    def full(tab, ctx_val):
        tab = jnp.tile(tab, (1, rep))
        return jnp.concatenate([jnp.full((n_ctx, LANES), ctx_val, F32), tab], axis=0)

    return full(cos, 1.0), full(sa, 0.0), full(sb, 0.0)


def kernel(x, c, ctx, c_ctx, ada_w, ada_b, norm1_g, norm2_g, w_in, conv_w, conv_b, conv_ln_g, conv_ln_b, mlstm_gate_b, mlstm_norm_g, diff_lambda, diff_norm_g, w_out, peer_wq, peer_keys, peer_u, peer_v, final_g):
    b, seq, d = x.shape
    n_ctx = ctx.shape[1]
    depth = ada_w.shape[0]
    cw, mw, dw = d // 4, d // 4, d // 2
    ng = 4 * MLSTM_HEADS
    assert n_ctx == TILE and seq % ATT_TQ == 0 and seq % GRID_W == 0
    assert w_in.shape[-1] == 2 * cw + 4 * mw + ng + 3 * dw
    assert peer_keys.shape[1:] == (PEER_HEADS, 2, N_KEYS, d // PEER_HEADS // 2)
    assert conv_w.shape[1] == CONV_K and (b * (seq + n_ctx)) % PEER_TM == 0

    rows = -(-(b + 1) // SUBLANES) * SUBLANES
    cvec = jnp.zeros((rows, d), F32).at[:b].set(c).at[b].set(c_ctx)
    mods = _ada_call(cvec, ada_w, ada_b)
    mod_lat = mods[:, :b].reshape(depth, b, 1, 6, d)
    mod_ctx = jnp.broadcast_to(mods[:, b].reshape(depth, 1, 1, 6, d), (depth, b, 1, 6, d))
    mod_all = jnp.concatenate([mod_ctx, mod_lat], axis=2)

    cos, sa, sb = _rope_tables(seq, n_ctx)
    g0 = 2 * cw + 4 * mw
    w_main = jnp.concatenate([w_in[:, :, :g0], w_in[:, :, g0 + ng:g0 + ng + 2 * dw], w_in[:, :, g0:g0 + ng],
                              jnp.zeros((depth, d, LANES - ng), F32)], axis=-1).astype(BF16)
    w_dvt = jnp.swapaxes(w_in[:, :, g0 + ng + 2 * dw:], 1, 2).astype(BF16)
    gbias = jnp.pad(mlstm_gate_b.reshape(depth, 1, ng), ((0, 0), (0, 0), (0, LANES - ng)))
    w_out_b = w_out.astype(BF16)
    wq_b = peer_wq.astype(BF16)
    hw = d // PEER_HEADS
    kz = jnp.zeros((depth, PEER_HEADS, N_KEYS, hw // 2), F32)
    kh = jnp.concatenate([jnp.concatenate([peer_keys[:, :, 0], kz], axis=-1),
                          jnp.concatenate([kz, peer_keys[:, :, 1]], axis=-1)], axis=2).astype(BF16)
    u_b = peer_u.astype(BF16)
    vt_b = jnp.swapaxes(peer_v, 1, 2).astype(BF16)
    mng = jnp.tile(mlstm_norm_g, (1, MLSTM_HEADS))

    xs = jnp.concatenate([ctx, x], axis=1)
    t = xs.shape[1]
    peer = None
    for l in range(depth):
        lam_init = 0.8 - 0.6 * math.exp(-0.3 * l)
        xs, u, mqkv, mo, gcol, grow, dq, dk, dvt = _inproj_call(
            xs, peer, mod_all[l - 1] if l else None, mod_all[l], norm1_g[l][None], w_main[l],
            w_dvt[l], gbias[l], ng, cos, sa, sb)
        conv = _conv_call(u, conv_w[l, :, 0, :], conv_b[l][None], conv_ln_g[l][None], conv_ln_b[l][None])
        gcol_d = gcol.reshape(b, t, 2, ng // 2).transpose(0, 2, 1, 3)
        grow_d = grow.reshape(b, 2, ng // 2, t)
        hf, hb = _mlstm_call(mqkv, gcol_d, grow_d)
        dyc = _attn_call(dq, dk, dvt, diff_lambda[l], diff_norm_g[l][:, None], lam_init, ctx=True)
        dyl = _attn_call(dq, dk, dvt, diff_lambda[l], diff_norm_g[l][:, None], lam_init, ctx=False)
        xs, h2, h2t = _outproj_call(xs, conv, hf, hb, mo, dyc, dyl, mod_all[l], mng[l][None], norm2_g[l][None],
                                    w_out_b[l])
        h2f = h2.reshape(b * t, d)
        a, ni, r1, bv = _peer_sel_call(h2f, wq_b[l], kh[l])
        peer = _peer_dense_call(h2t, a, ni, r1, bv, u_b[l], vt_b[l]).reshape(b, t, d)
    return _final_call(xs, peer, mod_all[depth - 1], final_g[None], n_ctx // TILE)
```

```python
import functools
import math

import jax
import jax.numpy as jnp
from jax import lax
from jax.experimental import pallas as pl
from jax.experimental.pallas import tpu as pltpu

F32 = jnp.float32
BF16 = jnp.bfloat16
HIGHEST = lax.Precision.HIGHEST

GRID_W = 64
EPS = 1e-6
CONV_K = 31
MLSTM_DH = 64
MLSTM_HEADS = 4
DIFF_DH = 64
DIFF_DV = 128
DIFF_HEADS = 4
ROPE_BASE = 10000.0
PEER_HEADS = 8
N_KEYS = 128
PEER_TOPK = 16

LANES = 128
SUBLANES = 8
TILE = 256
CONV_HALO = 16
ATT_KC = 1024
ATT_TQ = 1024
PEER_SEL_TILE = 256
PEER_TM = 1024
BF16_SUBLANES = 16
PEER_EC = 2048
VMEM_LIMIT = 56 * 1024 * 1024

NT_DIMS = (((1,), (1,)), ((), ()))


def _cparams(sem):
    return pltpu.CompilerParams(dimension_semantics=sem, vmem_limit_bytes=VMEM_LIMIT)


def _rms(x, eps=EPS):
    return x * lax.rsqrt(jnp.mean(x * x, axis=-1, keepdims=True) + eps)


def _log_sigmoid(x):
    return jnp.minimum(x, 0.0) - jnp.log(1.0 + jnp.exp(-jnp.abs(x)))


def _ada_kernel(c_ref, w_ref, b_ref, o_ref):
    c = c_ref[...]
    s = (c * jax.nn.sigmoid(c)).astype(BF16)
    o_ref[...] = jnp.dot(s, w_ref[...].astype(BF16), preferred_element_type=F32) + b_ref[...]


def _ada_call(cvec, ada_w, ada_b):
    depth, d, n = ada_w.shape
    tn = 1536
    rows = cvec.shape[0]
    return pl.pallas_call(
        _ada_kernel,
        grid=(depth, n // tn),
        in_specs=[pl.BlockSpec((rows, d), lambda l, j: (0, 0)),
                  pl.BlockSpec((None, d, tn), lambda l, j: (l, 0, j)),
                  pl.BlockSpec((None, 1, tn), lambda l, j: (l, 0, j))],
        out_specs=pl.BlockSpec((None, rows, tn), lambda l, j: (l, 0, j)),
        out_shape=jax.ShapeDtypeStruct((depth, rows, n), F32),
        compiler_params=_cparams(("parallel", "parallel")),
        name="ada_mod",
    )(cvec, ada_w, ada_b.reshape(depth, 1, n))


def _rope(t, c, sa, sb):
    w = t.shape[1]
    rep = w // LANES
    c, sa, sb = (jnp.concatenate([z] * rep, axis=1) for z in (c, sa, sb))
    return t * c + pltpu.roll(t, w - 1, 1) * sa + pltpu.roll(t, 1, 1) * sb


def _inproj_kernel(*refs, has_peer, cw, mw, bb):
    if has_peer:
        x_ref, peer_ref, modp_ref = refs[:3]
        refs = refs[3:]
    (mod_ref, n1g_ref, wm_ref, wdvt_ref, gb_ref, cos_ref, sa_ref, sb_ref,
     xo_ref, u_ref, mqkv_ref, mo_ref, gcol_ref, grow_ref, dq_ref, dk_ref, dvt_ref) = refs[-17:]
    if not has_peer:
        x_ref = refs[0]
    hs = []
    for k in range(bb):
        x = x_ref[k]
        if has_peer:
            x = x + modp_ref[k, 5:6, :] * peer_ref[k]
        xo_ref[k] = x
        hs.append(_rms(x) * n1g_ref[...] * (1.0 + mod_ref[k, 1:2, :]) + mod_ref[k, 0:1, :])
    hb = jnp.concatenate(hs, axis=0).astype(BF16)

    def proj(lo, hi):
        return jnp.dot(hb, wm_ref[:, lo:hi], preferred_element_type=F32)

    def rows(z, k):
        return z[k * TILE:(k + 1) * TILE]

    a = proj(0, 2 * cw)
    u = a[:, :cw] * jax.nn.sigmoid(a[:, cw:])
    o = 2 * cw
    mq = proj(o, o + mw).astype(BF16)
    mk = (proj(o + mw, o + 2 * mw) * (MLSTM_DH ** -0.5)).astype(BF16)
    mv = proj(o + 2 * mw, o + 3 * mw).astype(BF16)
    mo = proj(o + 3 * mw, o + 4 * mw)
    o = o + 4 * mw
    dw = dq_ref.shape[2]
    c, sa, sb = cos_ref[...], sa_ref[...], sb_ref[...]
    pq = proj(o, o + dw)
    pk = proj(o + dw, o + 2 * dw)
    dvt = lax.dot_general(wdvt_ref[...], hb, NT_DIMS, preferred_element_type=F32).astype(BF16)
    o = o + 2 * dw
    ng = gcol_ref.shape[2]
    g = proj(o, o + LANES) + gb_ref[...]
    cidx = lax.broadcasted_iota(jnp.int32, g.shape, 1)
    g = jnp.where((cidx // MLSTM_HEADS) % 2 == 1, _log_sigmoid(g), g)
    for k in range(bb):
        u_ref[k] = rows(u, k)
        mqkv_ref[k, :, 0:mw] = rows(mq, k)
        mqkv_ref[k, :, mw:2 * mw] = rows(mk, k)
        mqkv_ref[k, :, 2 * mw:3 * mw] = rows(mv, k)
        mo_ref[k] = rows(mo, k)
        dq_ref[k] = (_rope(rows(pq, k), c, sa, sb) * (DIFF_DH ** -0.5 * math.log2(math.e))).astype(BF16)
        dk_ref[k] = _rope(rows(pk, k), c, sa, sb).astype(BF16)
        dvt_ref[k] = dvt[:, k * TILE:(k + 1) * TILE]
        gk = rows(g, k)
        gcol_ref[k] = gk[:, :ng]
        grow_ref[k] = gk.T[:ng, :]


def _batch_block(b):
    return 2 if b % 2 == 0 else 1


def _inproj_call(x, peer, modp, mod, n1g, wm, wdvt, gb, ng, cos, sa, sb):
    b, t, d = x.shape
    nt = t // TILE
    bb = _batch_block(b)
    cw = d // 4
    mw = d // 4
    dw = d // 2
    has_peer = peer is not None
    tok = lambda bi, i: (bi, i, 0)
    modspec = pl.BlockSpec((bb, None, 6, d), lambda bi, i: (bi, jnp.minimum(i, 1), 0, 0))
    full2 = lambda arr: pl.BlockSpec(arr.shape, lambda bi, i: (0, 0))
    in_specs = [pl.BlockSpec((bb, TILE, d), tok)]
    args = [x]
    if has_peer:
        in_specs += [pl.BlockSpec((bb, TILE, d), tok), modspec]
        args += [peer, modp]
    in_specs += [modspec, full2(n1g), full2(wm), full2(wdvt), full2(gb)]
    args += [mod, n1g, wm, wdvt, gb]
    in_specs += [pl.BlockSpec((TILE, LANES), lambda bi, i: (i, 0))] * 3
    args += [cos, sa, sb]
    out_shape = [jax.ShapeDtypeStruct((b, t, d), F32),
                 jax.ShapeDtypeStruct((b, t, cw), F32),
                 jax.ShapeDtypeStruct((b, t, 3 * mw), BF16),
                 jax.ShapeDtypeStruct((b, t, mw), F32),
                 jax.ShapeDtypeStruct((b, t, ng), F32),
                 jax.ShapeDtypeStruct((b, ng, t), F32),
                 jax.ShapeDtypeStruct((b, t, dw), BF16),
                 jax.ShapeDtypeStruct((b, t, dw), BF16),
                 jax.ShapeDtypeStruct((b, dw, t), BF16)]
    out_specs = [pl.BlockSpec((bb, TILE, d), tok),
                 pl.BlockSpec((bb, TILE, cw), tok),
                 pl.BlockSpec((bb, TILE, 3 * mw), tok),
                 pl.BlockSpec((bb, TILE, mw), tok),
                 pl.BlockSpec((bb, TILE, ng), tok),
                 pl.BlockSpec((bb, ng, TILE), lambda bi, i: (bi, 0, i)),
                 pl.BlockSpec((bb, TILE, dw), tok),
                 pl.BlockSpec((bb, TILE, dw), tok),
                 pl.BlockSpec((bb, dw, TILE), lambda bi, i: (bi, 0, i))]
    return pl.pallas_call(
        functools.partial(_inproj_kernel, has_peer=has_peer, cw=cw, mw=mw, bb=bb),
        grid=(b // bb, nt), in_specs=in_specs, out_specs=out_specs, out_shape=out_shape,
        compiler_params=_cparams(("parallel", "parallel")),
        name="in_proj",
    )(*args)


def _conv_kernel(up_ref, uc_ref, un_ref, w_ref, b_ref, lg_ref, lb_ref, o_ref, ext_ref, sh_ref):
    i = pl.program_id(1)
    nt = pl.num_programs(1)
    lm = jnp.where(i >= 2, 1.0, 0.0)
    rm = jnp.where(jnp.logical_and(i >= 1, i < nt - 1), 1.0, 0.0)
    hl = CONV_HALO
    ext_ref[0:hl, :] = up_ref[TILE - hl:TILE, :] * lm
    ext_ref[hl:hl + TILE, :] = uc_ref[...]
    ext_ref[hl + TILE:2 * hl + TILE, :] = un_ref[0:hl, :] * rm
    off = hl - CONV_K // 2
    span = sh_ref.shape[1]
    acc = jnp.zeros(uc_ref.shape, F32)
    for r in range(SUBLANES):
        taps = [k for k in range(CONV_K) if (off + k) % SUBLANES == r]
        if not taps:
            continue
        sh_ref[r] = ext_ref[r:r + span, :]
        for k in taps:
            q = (off + k) // SUBLANES * SUBLANES
            acc = acc + w_ref[k:k + 1, :] * sh_ref[r, q:q + TILE, :]
    y = acc + b_ref[...]
    mu = jnp.mean(y, axis=-1, keepdims=True)
    yc = y - mu
    var = jnp.mean(yc * yc, axis=-1, keepdims=True)
    z = yc * lax.rsqrt(var + EPS) * lg_ref[...] + lb_ref[...]
    o_ref[...] = (z * jax.nn.sigmoid(z)).astype(o_ref.dtype)


def _conv_call(u, w, bias, lg, lb):
    b, t, cw = u.shape
    nt = t // TILE
    full2 = lambda arr: pl.BlockSpec(arr.shape, lambda bi, i: (0, 0))
    return pl.pallas_call(
        _conv_kernel,
        grid=(b, nt),
        in_specs=[pl.BlockSpec((None, TILE, cw), lambda bi, i: (bi, jnp.maximum(i - 1, 0), 0)),
                  pl.BlockSpec((None, TILE, cw), lambda bi, i: (bi, i, 0)),
                  pl.BlockSpec((None, TILE, cw), lambda bi, i: (bi, jnp.minimum(i + 1, nt - 1), 0)),
                  full2(w), full2(bias), full2(lg), full2(lb)],
        out_specs=pl.BlockSpec((None, TILE, cw), lambda bi, i: (bi, i, 0)),
        out_shape=jax.ShapeDtypeStruct((b, t, cw), BF16),
        scratch_shapes=[pltpu.VMEM((TILE + 2 * CONV_HALO, cw), F32),
                        pltpu.VMEM((SUBLANES, TILE + 2 * CONV_HALO - SUBLANES, cw), F32)],
        compiler_params=_cparams(("parallel", "parallel")),
        name="conv_module",
    )(u, u, u, w, bias, lg, lb)


def _mlstm_chunk(qkv_ref, gc_ref, gr_ref, h_ref, c_ref, n_ref, m_ref, fwd):
    tc = qkv_ref.shape[0]
    mw = qkv_ref.shape[1] // 3
    nh = MLSTM_HEADS
    dh = MLSTM_DH
    row = lax.broadcasted_iota(jnp.int32, (tc, tc), 0)
    col = lax.broadcasted_iota(jnp.int32, (tc, tc), 1)
    tri = row >= col if fwd else row <= col
    vis = row <= col if fwd else row >= col
    trif = tri.astype(F32)
    gc = gc_ref[...]
    gr = gr_ref[...]
    bcol = jnp.dot(trif, gc, precision=HIGHEST, preferred_element_type=F32)
    brow = lax.dot_general(gr, trif, NT_DIMS, precision=HIGHEST, preferred_element_type=F32)
    bl = jnp.sum(gr, axis=1, keepdims=True)
    src = gc[:, 0:nh] - bcol[:, nh:2 * nh]

    q = qkv_ref[:, 0:mw]
    k = qkv_ref[:, mw:2 * mw]
    v = qkv_ref[:, 2 * mw:3 * mw]
    qf = q.astype(F32)
    vt = v.astype(F32).T
    vt_b = vt.astype(BF16)
    lane_head = lax.broadcasted_iota(jnp.int32, (tc, mw), 1) // dh
    rhead = lax.broadcasted_iota(jnp.int32, (mw, 1), 0) // dh
    chead = lax.broadcasted_iota(jnp.int32, (1, mw), 1) // dh
    cb = c_ref[...]
    n_old = n_ref[...]
    inter_c = lax.dot_general(cb.astype(BF16), q, NT_DIMS, preferred_element_type=F32)
    zrow = jnp.zeros((1, mw), F32)
    n4 = jnp.concatenate([jnp.where(chead == h, n_old, 0.0) for h in range(nh)] + [zrow] * (SUBLANES - nh),
                         axis=0)
    qn = lax.dot_general(n4, qf, NT_DIMS, precision=HIGHEST, preferred_element_type=F32)

    outs, vws, wks = [], [], []
    decay_col = jnp.zeros((mw, 1), F32)
    decay_row = jnp.zeros((1, mw), F32)
    for h in range(nh):
        hr = slice(h * dh, (h + 1) * dh)
        mh = m_ref[h:h + 1, 0:1]
        bt = brow[nh + h:nh + h + 1, :]
        dlog = jnp.where(vis, bt + src[:, h:h + 1], -jnp.inf)
        inter = bt + mh
        mt = jnp.maximum(inter, jnp.max(dlog, axis=0, keepdims=True))
        dwt = jnp.exp(dlog - mt)
        iw = jnp.exp(inter - mt)
        qh = jnp.where(lane_head == h, qf, 0.0).astype(BF16)
        s = lax.dot_general(k, qh, NT_DIMS, preferred_element_type=F32) * dwt
        num = jnp.dot(vt_b[hr], s.astype(BF16), preferred_element_type=F32)
        den = jnp.sum(s, axis=0, keepdims=True) + iw * qn[h:h + 1, :]
        denom = jnp.maximum(jnp.abs(den), jnp.exp(-mt))
        outs.append((num + iw * inter_c[hr]) / denom)
        blh = bl[nh + h:nh + h + 1, :]
        wlog = blh - bt + gr[h:h + 1, :]
        mn = jnp.maximum(blh + mh, jnp.max(wlog, axis=1, keepdims=True))
        decay = jnp.exp(blh + mh - mn)
        wk = jnp.exp(wlog - mn)
        wks.append(wk)
        vws.append(vt[hr] * wk)
        decay_col = jnp.where(rhead == h, decay, decay_col)
        decay_row = jnp.where(chead == h, decay, decay_row)
        m_ref[h:h + 1, :] = jnp.broadcast_to(mn, (1, m_ref.shape[1]))
    h_ref[...] = jnp.concatenate(outs, axis=0).T

    vw = jnp.concatenate(vws, axis=0).astype(BF16)
    upd = jnp.dot(vw, k, preferred_element_type=F32)
    c_ref[...] = decay_col * cb + jnp.where(rhead == chead, upd, 0.0)
    wk4 = jnp.concatenate(wks + [jnp.zeros((1, tc), F32)] * (SUBLANES - nh), axis=0)
    nk = jnp.dot(wk4, k.astype(F32), precision=HIGHEST, preferred_element_type=F32)
    n_add = zrow
    for h in range(nh):
        n_add = jnp.where(chead == h, nk[h:h + 1, :], n_add)
    n_ref[...] = decay_row * n_old + n_add


def _mlstm_kernel(qkvf_ref, gcf_ref, grf_ref, qkvb_ref, gcb_ref, grb_ref, hf_ref, hb_ref,
                  c_ref, n_ref, m_ref):
    @pl.when(pl.program_id(1) == 0)
    def _():
        c_ref[...] = jnp.zeros_like(c_ref)
        n_ref[...] = jnp.zeros_like(n_ref)
        m_ref[...] = jnp.zeros_like(m_ref)

    _mlstm_chunk(qkvf_ref, gcf_ref, grf_ref, hf_ref, c_ref.at[0], n_ref.at[0], m_ref.at[0], True)
    _mlstm_chunk(qkvb_ref, gcb_ref, grb_ref, hb_ref, c_ref.at[1], n_ref.at[1], m_ref.at[1], False)


def _mlstm_call(mqkv, gcol, grow):
    b, t, w3 = mqkv.shape
    mw = w3 // 3
    nt = t // TILE
    ng = gcol.shape[-1]

    def rev(j):
        return jnp.where(j == 0, 0, nt - j)

    hshape = jax.ShapeDtypeStruct((b, t, mw), F32)
    return pl.pallas_call(
        _mlstm_kernel,
        grid=(b, nt),
        in_specs=[pl.BlockSpec((None, TILE, w3), lambda bi, j: (bi, j, 0)),
                  pl.BlockSpec((None, None, TILE, ng), lambda bi, j: (bi, 0, j, 0)),
                  pl.BlockSpec((None, None, ng, TILE), lambda bi, j: (bi, 0, 0, j)),
                  pl.BlockSpec((None, TILE, w3), lambda bi, j: (bi, rev(j), 0)),
                  pl.BlockSpec((None, None, TILE, ng), lambda bi, j: (bi, 1, rev(j), 0)),
                  pl.BlockSpec((None, None, ng, TILE), lambda bi, j: (bi, 1, 0, rev(j)))],
        out_specs=[pl.BlockSpec((None, TILE, mw), lambda bi, j: (bi, j, 0)),
                   pl.BlockSpec((None, TILE, mw), lambda bi, j: (bi, rev(j), 0))],
        out_shape=[hshape, hshape],
        scratch_shapes=[pltpu.VMEM((2, mw, mw), F32), pltpu.VMEM((2, 1, mw), F32),
                        pltpu.VMEM((2, SUBLANES, LANES), F32)],
        compiler_params=_cparams(("arbitrary", "arbitrary")),
        name="mlstm_scan",
    )(mqkv, gcol, grow, mqkv, gcol, grow)


def _attn_kernel(*refs, lam_init, kc, nq):
    dl_ref, ng_ref = refs[:2]
    q_refs = refs[2:2 + nq]
    k_ref, vt_ref, o_ref, s_ref, m_ref, l_ref, acc_ref = refs[2 + nq:]
    n_keys = k_ref.shape[0]
    q = jnp.concatenate([r[...] for r in q_refs], axis=0) if nq > 1 else q_refs[0][...]
    lane = lax.broadcasted_iota(jnp.int32, q.shape, 1)
    zero = jnp.zeros_like(q)
    qz = (jnp.where(lane < DIFF_DH, q, zero), jnp.where(lane >= DIFF_DH, q, zero))
    m_ref[...] = jnp.full(m_ref.shape, -jnp.inf, F32)
    l_ref[...] = jnp.zeros_like(l_ref)
    acc_ref[...] = jnp.zeros_like(acc_ref)

    def scores(start, slot):
        kb = k_ref[pl.ds(start, kc), :]
        for c in range(2):
            s_ref[slot, c] = lax.dot_general(kb, qz[c], NT_DIMS, preferred_element_type=F32)

    def softmax_pv(start, slot):
        vtb = vt_ref[:, pl.ds(start, kc)]
        for c in range(2):
            s = s_ref[slot, c]
            m_old = m_ref[c]
            m_new = jnp.maximum(m_old, jnp.max(s, axis=0, keepdims=True))
            alpha = jnp.exp2(m_old - m_new)
            p = jnp.exp2(s - m_new)
            l_ref[c] = alpha * l_ref[c] + jnp.sum(p, axis=0, keepdims=True)
            acc_ref[c] = alpha * acc_ref[c] + jnp.dot(vtb, p.astype(BF16), preferred_element_type=F32)
            m_ref[c] = m_new

    n = n_keys // kc
    pairs = (n - 1) // 2
    scores(0, 0)
    if pairs:
        def body(jj, carry):
            base = pl.multiple_of(jj * (2 * kc), TILE)
            scores(base + kc, 1)
            softmax_pv(base, 0)
            scores(base + 2 * kc, 0)
            softmax_pv(base + kc, 1)
            return carry

        lax.fori_loop(0, pairs, body, 0)
    done = 2 * pairs
    if n - done == 2:
        scores((done + 1) * kc, 1)
    softmax_pv(done * kc, 0)
    if n - done == 2:
        softmax_pv((done + 1) * kc, 1)

    dl = dl_ref[...]
    lam = (jnp.exp(jnp.sum(dl[0:1] * dl[1:2], axis=1, keepdims=True))
           - jnp.exp(jnp.sum(dl[2:3] * dl[3:4], axis=1, keepdims=True)) + lam_init)
    ot = acc_ref[0] / l_ref[0] - lam * (acc_ref[1] / l_ref[1])
    ms = jnp.mean(ot * ot, axis=0, keepdims=True)
    y = ot * lax.rsqrt(ms + EPS) * (ng_ref[...] * (1.0 - lam_init))
    o_ref[...] = y.T.astype(o_ref.dtype)


def _attn_key_chunk(t):
    return max(k for k in range(TILE, ATT_KC + 1, TILE) if t % k == 0)


def _attn_call(dq, dk, dvt, dlam, dng, lam_init, *, ctx):
    b, t, w = dq.shape
    nh = w // DIFF_DV
    if ctx:
        n_keys, nq, steps, q0 = TILE, 1, 1, 0
    else:
        n_keys, nq, q0 = t, ATT_TQ // TILE, 1
        steps = (t - TILE) // ATT_TQ
    tq = nq * TILE
    kc = _attn_key_chunk(n_keys)
    kern = functools.partial(_attn_kernel, lam_init=lam_init, kc=kc, nq=nq)
    q_specs = [pl.BlockSpec((None, TILE, DIFF_DV), functools.partial(
        lambda bi, h, i, j: (bi, q0 + nq * i + j, h), j=j)) for j in range(nq)]
    return pl.pallas_call(
        kern,
        grid=(b, nh, steps),
        in_specs=[pl.BlockSpec(dlam.shape, lambda bi, h, i: (0, 0)),
                  pl.BlockSpec(dng.shape, lambda bi, h, i: (0, 0))] + q_specs + [
                  pl.BlockSpec((None, n_keys, DIFF_DV), lambda bi, h, i: (bi, 0, h)),
                  pl.BlockSpec((None, DIFF_DV, n_keys), lambda bi, h, i: (bi, h, 0))],
        out_specs=pl.BlockSpec((None, tq, DIFF_DV), lambda bi, h, i: (bi, i, h)),
        out_shape=jax.ShapeDtypeStruct((b, steps * tq, w), BF16),
        scratch_shapes=[pltpu.VMEM((2, 2, kc, tq), F32),
                        pltpu.VMEM((2, 1, tq), F32), pltpu.VMEM((2, 1, tq), F32),
                        pltpu.VMEM((2, DIFF_DV, tq), F32)],
        compiler_params=_cparams(("parallel", "parallel", "arbitrary")),
        name="diff_attn_ctx" if ctx else "diff_attn",
    )(dlam, dng, *([dq] * nq), dk, dvt)


def _outproj_kernel(x_ref, conv_ref, hf_ref, hb_ref, mo_ref, dyc_ref, dyl_ref, mod_ref, mng_ref, n2g_ref,
                    wo_ref, xo_ref, h2_ref, h2t_ref):
    cw = conv_ref.shape[1]
    mw = hf_ref.shape[1]
    hh = hf_ref[...] + hb_ref[...]
    r = lax.broadcasted_iota(jnp.int32, (mw, mw), 0) // MLSTM_DH
    c = lax.broadcasted_iota(jnp.int32, (mw, mw), 1) // MLSTM_DH
    gm = jnp.where(r == c, 1.0 / MLSTM_DH, 0.0)
    ms = jnp.dot(hh * hh, gm, precision=HIGHEST, preferred_element_type=F32)
    ym = hh * lax.rsqrt(ms + EPS) * mng_ref[...] * jax.nn.sigmoid(mo_ref[...])
    dy = jnp.where(pl.program_id(1) == 0, dyc_ref[...], dyl_ref[...])
    o = (jnp.dot(conv_ref[...], wo_ref[0:cw, :], preferred_element_type=F32)
         + jnp.dot(ym.astype(BF16), wo_ref[cw:cw + mw, :], preferred_element_type=F32)
         + jnp.dot(dy, wo_ref[cw + mw:, :], preferred_element_type=F32))
    x = x_ref[...] + mod_ref[2:3, :] * o
    xo_ref[...] = x
    h2 = _rms(x) * n2g_ref[...] * (1.0 + mod_ref[4:5, :]) + mod_ref[3:4, :]
    h2_ref[...] = h2.astype(BF16)
    h2t_ref[...] = h2.T.astype(BF16)


def _outproj_call(x, conv, hf, hb, mo, dyc, dyl, mod, mng, n2g, wo):
    b, t, d = x.shape
    nt = t // TILE
    cw, mw, dw = conv.shape[-1], mo.shape[-1], dyl.shape[-1]
    tok = lambda bi, i: (bi, i, 0)
    full2 = lambda arr: pl.BlockSpec(arr.shape, lambda bi, i: (0, 0))
    return pl.pallas_call(
        _outproj_kernel,
        grid=(b, nt),
        in_specs=[pl.BlockSpec((None, TILE, d), tok),
                  pl.BlockSpec((None, TILE, cw), tok),
                  pl.BlockSpec((None, TILE, mw), tok),
                  pl.BlockSpec((None, TILE, mw), tok),
                  pl.BlockSpec((None, TILE, mw), tok),
                  pl.BlockSpec((None, TILE, dw), lambda bi, i: (bi, 0, 0)),
                  pl.BlockSpec((None, TILE, dw), lambda bi, i: (bi, jnp.maximum(i - 1, 0), 0)),
                  pl.BlockSpec((None, None, 6, d), lambda bi, i: (bi, jnp.minimum(i, 1), 0, 0)),
                  full2(mng), full2(n2g), full2(wo)],
        out_specs=[pl.BlockSpec((None, TILE, d), tok), pl.BlockSpec((None, TILE, d), tok),
                   pl.BlockSpec((d, TILE), lambda bi, i: (0, bi * nt + i))],
        out_shape=[jax.ShapeDtypeStruct((b, t, d), F32), jax.ShapeDtypeStruct((b, t, d), BF16),
                   jax.ShapeDtypeStruct((d, b * t), BF16)],
        compiler_params=_cparams(("parallel", "parallel")),
        name="out_proj",
    )(x, conv, hf, hb, mo, dyc, dyl, mod, mng, n2g, wo)


def _top_k_rows(s, k):
    idx = lax.broadcasted_iota(jnp.int32, s.shape, 0).astype(F32)
    vals, ids = [], []
    cur = s
    for _ in range(k):
        m = jnp.max(cur, axis=0, keepdims=True)
        am = jnp.min(jnp.where(cur == m, idx, float(s.shape[0])), axis=0, keepdims=True)
        vals.append(m)
        ids.append(am)
        cur = jnp.where(idx == am, -jnp.inf, cur)
    return vals, ids


def _select_exact(s0, s1):
    kk = PEER_TOPK
    tt = s0.shape[1]
    iota_k = lax.broadcasted_iota(jnp.int32, (kk, tt), 0).astype(F32)
    iota_n = lax.broadcasted_iota(jnp.int32, (N_KEYS, tt), 0).astype(F32)
    v0, i0 = _top_k_rows(s0, kk)
    v1, i1 = _top_k_rows(s1, kk)
    st0 = jnp.concatenate(v0, axis=0)
    n = jnp.zeros((kk, tt), F32)
    hv = st0 + v1[0]
    mx = v0[0] + v1[0]
    zsum = jnp.zeros((1, tt), F32)
    for _ in range(kk):
        m = jnp.max(hv, axis=0, keepdims=True)
        a_star = jnp.min(jnp.where(hv == m, iota_k, float(kk)), axis=0, keepdims=True)
        sel = iota_k == a_star
        zsum = zsum + jnp.exp(m - mx)
        n = n + jnp.where(sel, 1.0, 0.0)
        nxt = jnp.full((kk, tt), -jnp.inf, F32)
        for b in range(1, kk):
            nxt = jnp.where(n == float(b), v1[b], nxt)
        hv = jnp.where(sel, st0 + nxt, hv)
    ni = jnp.zeros((N_KEYS, tt), F32)
    r1 = jnp.full((N_KEYS, tt), float(kk), F32)
    for a in range(kk):
        ni = jnp.where(iota_n == i0[a], n[a:a + 1, :], ni)
        r1 = jnp.where(iota_n == i1[a], float(a), r1)
    return jnp.exp(s0 - v0[0]) * (0.5 / zsum), ni, r1, jnp.exp(s1 - v1[0])


def _top_k_ranks(s, k):
    cur = s
    rank = jnp.full(s.shape, float(k), F32)
    vals = []
    for j in range(k):
        m = jnp.max(cur, axis=0, keepdims=True)
        eq = cur == m
        vals.append(m)
        rank = jnp.where(eq, float(j), rank)
        cur = jnp.where(eq, -jnp.inf, cur)
    cnt = jnp.sum(jnp.where(rank < float(k), 1.0, 0.0), axis=0, keepdims=True)
    return vals, rank, cnt


def _select_fast(s0, s1):
    kk = PEER_TOPK
    tt = s0.shape[1]
    v0, rank0, cnt0 = _top_k_ranks(s0, kk)
    v1, rank1, cnt1 = _top_k_ranks(s1, kk)
    half = SUBLANES
    st1 = jnp.concatenate(v1, axis=0)
    st1_h = st1[0:half]
    riota = lax.broadcasted_iota(jnp.int32, (half, tt), 0)
    blocks = [v0[0] + st1]
    for a in range(1, half):
        blocks.append(jnp.where(riota < kk // (a + 1), v0[a] + st1_h, -jnp.inf))
    blocks.append(jnp.concatenate(v0[half:], axis=0) + v1[0])
    p = jnp.concatenate(blocks, axis=0)
    cur = p
    for _ in range(kk):
        m = jnp.max(cur, axis=0, keepdims=True)
        cur = jnp.where(cur == m, -jnp.inf, cur)
    picked = jnp.where(cur != p, 1.0, 0.0)
    mx = v0[0] + v1[0]
    zsum = jnp.sum(picked * jnp.exp(p - mx), axis=0, keepdims=True)
    cntm = jnp.sum(picked, axis=0, keepdims=True)
    n_rows = [jnp.sum(picked[0:kk], axis=0, keepdims=True)]
    for a in range(1, half):
        lo = kk + (a - 1) * half
        n_rows.append(jnp.sum(picked[lo:lo + half], axis=0, keepdims=True))
    last = kk + (half - 1) * half
    n_rows += [picked[last + r:last + r + 1] for r in range(kk - half)]
    ni = jnp.zeros((N_KEYS, tt), F32)
    for a in range(kk):
        ni = jnp.where(rank0 == float(a), n_rows[a], ni)
    want = float(kk)
    bad = jnp.where((cnt0 != want) | (cnt1 != want) | (cntm != want), 1.0, 0.0)
    return jnp.exp(s0 - v0[0]) * (0.5 / zsum), ni, rank1, jnp.exp(s1 - v1[0]), bad


def _peer_sel_kernel(h2_ref, wq_ref, kh_ref, a_ref, ni_ref, r1_ref, bv_ref):
    q = jnp.dot(h2_ref[...], wq_ref[...], preferred_element_type=F32).astype(BF16)
    hw = q.shape[1] // PEER_HEADS

    def head_scores(h):
        st_ = lax.dot_general(kh_ref[h], q[:, h * hw:(h + 1) * hw], NT_DIMS,
                              preferred_element_type=F32)
        return st_[0:N_KEYS], st_[N_KEYS:2 * N_KEYS]

    def store(h, a, ni, r1, bv):
        a_ref[h] = a
        ni_ref[h] = ni
        r1_ref[h] = r1.astype(BF16)
        bv_ref[h] = bv.astype(BF16)

    for h in range(PEER_HEADS):
        s0, s1 = head_scores(h)
        a, ni, r1, bv, bad = _select_fast(s0, s1)
        store(h, a, ni, r1, bv)

        @pl.when(jnp.max(bad) > 0.0)
        def _(h=h, s0=s0, s1=s1):
            store(h, *_select_exact(s0, s1))


def _peer_sel_call(h2, wq, kh):
    ntok, d = h2.shape
    nt = ntok // PEER_SEL_TILE
    shp = jax.ShapeDtypeStruct((PEER_HEADS, N_KEYS, ntok), F32)
    shp_b = jax.ShapeDtypeStruct((PEER_HEADS, N_KEYS, ntok), BF16)
    ospec = pl.BlockSpec((PEER_HEADS, N_KEYS, PEER_SEL_TILE), lambda i: (0, 0, i))
    return pl.pallas_call(
        _peer_sel_kernel,
        grid=(nt,),
        in_specs=[pl.BlockSpec((PEER_SEL_TILE, d), lambda i: (i, 0)),
                  pl.BlockSpec(wq.shape, lambda i: (0, 0)),
                  pl.BlockSpec(kh.shape, lambda i: (0, 0, 0))],
        out_specs=[ospec] * 4,
        out_shape=[shp, shp, shp_b, shp_b],
        compiler_params=_cparams(("parallel",)),
        name="peer_select",
    )(h2, wq, kh)


def _gelu_x2(x):
    return x + x * lax.erf(x * (2.0 ** -0.5))


def _peer_dense_kernel(h2t_ref, a_ref, ni_ref, r1_ref, bv_ref, u_ref, vt_ref, o_ref, acc_ref):
    c = pl.program_id(1)

    @pl.when(c == 0)
    def _():
        acc_ref[...] = jnp.zeros_like(acc_ref)

    tm = h2t_ref.shape[1]
    nib = u_ref.shape[0] // N_KEYS
    rep = N_KEYS // BF16_SUBLANES

    def rows(ref, h, ib):
        r = jnp.broadcast_to(ref[h, ib:ib + 1, :], (BF16_SUBLANES, tm)).astype(BF16)
        return jnp.concatenate([r] * rep, axis=0)

    st_ = jnp.dot(u_ref[...], h2t_ref[...], preferred_element_type=F32)
    blocks = []
    for ib in range(nib):
        g = None
        for h in range(PEER_HEADS):
            bvh = bv_ref[h]
            keep = r1_ref[h] < rows(ni_ref, h, ib)
            term = rows(a_ref, h, ib) * jnp.where(keep, bvh, jnp.zeros_like(bvh))
            g = term if g is None else g + term
        blocks.append(g * _gelu_x2(st_[ib * N_KEYS:(ib + 1) * N_KEYS].astype(BF16)))
    wt = jnp.concatenate(blocks, axis=0)
    acc_ref[...] += jnp.dot(vt_ref[...], wt, preferred_element_type=F32)

    @pl.when(c == pl.num_programs(1) - 1)
    def _():
        o_ref[...] = acc_ref[...].T


def _peer_dense_call(h2t, a, ni, r1, bv, u, vt):
    d, ntok = h2t.shape
    ne = u.shape[0]
    tm, ec = PEER_TM, PEER_EC
    ib = ec // N_KEYS
    rowspec = pl.BlockSpec((PEER_HEADS, ib, tm), lambda t, c: (0, c, t))
    colspec = pl.BlockSpec((PEER_HEADS, N_KEYS, tm), lambda t, c: (0, 0, t))
    return pl.pallas_call(
        _peer_dense_kernel,
        grid=(ntok // tm, ne // ec),
        in_specs=[pl.BlockSpec((d, tm), lambda t, c: (0, t)),
                  rowspec, rowspec, colspec, colspec,
                  pl.BlockSpec((ec, d), lambda t, c: (c, 0)),
                  pl.BlockSpec((d, ec), lambda t, c: (0, c))],
        out_specs=pl.BlockSpec((tm, d), lambda t, c: (t, 0)),
        out_shape=jax.ShapeDtypeStruct((ntok, d), F32),
        scratch_shapes=[pltpu.VMEM((d, tm), F32)],
        compiler_params=_cparams(("parallel", "arbitrary")),
        name="peer_dense",
    )(h2t, a, ni, r1, bv, u, vt)


def _final_kernel(x_ref, peer_ref, mod_ref, g_ref, o_ref):
    x = x_ref[...] + mod_ref[5:6, :] * peer_ref[...]
    o_ref[...] = _rms(x) * g_ref[...]


def _final_call(x, peer, mod, fg, n_ctx_tiles):
    b, t, d = x.shape
    nl = t // TILE - n_ctx_tiles
    tok = lambda bi, i: (bi, i + n_ctx_tiles, 0)
    return pl.pallas_call(
        _final_kernel,
        grid=(b, nl),
        in_specs=[pl.BlockSpec((None, TILE, d), tok), pl.BlockSpec((None, TILE, d), tok),
                  pl.BlockSpec((None, None, 6, d), lambda bi, i: (bi, 1, 0, 0)),
                  pl.BlockSpec(fg.shape, lambda bi, i: (0, 0))],
        out_specs=pl.BlockSpec((None, TILE, d), lambda bi, i: (bi, i, 0)),
        out_shape=jax.ShapeDtypeStruct((b, nl * TILE, d), F32),
        compiler_params=_cparams(("parallel", "parallel")),
        name="final_norm",
    )(x, peer, mod, fg)


def _rope_tables(seq, n_ctx):
    rows = seq // GRID_W
    axis_rot = DIFF_DH // 2
    row = jnp.repeat(jnp.arange(rows), GRID_W).astype(F32)
    col = jnp.tile(jnp.arange(GRID_W), rows).astype(F32)
    inv = ROPE_BASE ** (-jnp.arange(0, axis_rot, 2, dtype=F32) / axis_rot)
    ang = jnp.concatenate([row[:, None] * inv, col[:, None] * inv], axis=-1)
    cos = jnp.repeat(jnp.cos(ang), 2, axis=-1)
    sin = jnp.repeat(jnp.sin(ang), 2, axis=-1)
    even = (jnp.arange(DIFF_DH) % 2 == 0)[None, :]
    sa = jnp.where(even, -sin, 0.0)
    sb = jnp.where(even, 0.0, sin)
    rep = LANES // DIFF_DH

    def full(tab, ctx_val):
        tab = jnp.tile(tab, (1, rep))
        return jnp.concatenate([jnp.full((n_ctx, LANES), ctx_val, F32), tab], axis=0)

    return full(cos, 1.0), full(sa, 0.0), full(sb, 0.0)


def kernel(x, c, ctx, c_ctx, ada_w, ada_b, norm1_g, norm2_g, w_in, conv_w, conv_b, conv_ln_g, conv_ln_b, mlstm_gate_b, mlstm_norm_g, diff_lambda, diff_norm_g, w_out, peer_wq, peer_keys, peer_u, peer_v, final_g):
    b, seq, d = x.shape
    n_ctx = ctx.shape[1]
    depth = ada_w.shape[0]
    cw, mw, dw = d // 4, d // 4, d // 2
    ng = 4 * MLSTM_HEADS
    assert n_ctx == TILE and seq % ATT_TQ == 0 and seq % GRID_W == 0
    assert w_in.shape[-1] == 2 * cw + 4 * mw + ng + 3 * dw
    assert peer_keys.shape[1:] == (PEER_HEADS, 2, N_KEYS, d // PEER_HEADS // 2)
    assert conv_w.shape[1] == CONV_K and (b * (seq + n_ctx)) % PEER_TM == 0

    rows = -(-(b + 1) // SUBLANES) * SUBLANES
    cvec = jnp.zeros((rows, d), F32).at[:b].set(c).at[b].set(c_ctx)
    mods = _ada_call(cvec, ada_w, ada_b)
    mod_lat = mods[:, :b].reshape(depth, b, 1, 6, d)
    mod_ctx = jnp.broadcast_to(mods[:, b].reshape(depth, 1, 1, 6, d), (depth, b, 1, 6, d))
    mod_all = jnp.concatenate([mod_ctx, mod_lat], axis=2)

    cos, sa, sb = _rope_tables(seq, n_ctx)
    g0 = 2 * cw + 4 * mw
    w_main = jnp.concatenate([w_in[:, :, :g0], w_in[:, :, g0 + ng:g0 + ng + 2 * dw], w_in[:, :, g0:g0 + ng],
                              jnp.zeros((depth, d, LANES - ng), F32)], axis=-1).astype(BF16)
    w_dvt = jnp.swapaxes(w_in[:, :, g0 + ng + 2 * dw:], 1, 2).astype(BF16)
    gbias = jnp.pad(mlstm_gate_b.reshape(depth, 1, ng), ((0, 0), (0, 0), (0, LANES - ng)))
    w_out_b = w_out.astype(BF16)
    wq_b = peer_wq.astype(BF16)
    hw = d // PEER_HEADS
    kz = jnp.zeros((depth, PEER_HEADS, N_KEYS, hw // 2), F32)
    kh = jnp.concatenate([jnp.concatenate([peer_keys[:, :, 0], kz], axis=-1),
                          jnp.concatenate([kz, peer_keys[:, :, 1]], axis=-1)], axis=2).astype(BF16)
    u_b = peer_u.astype(BF16)
    vt_b = jnp.swapaxes(peer_v, 1, 2).astype(BF16)
    mng = jnp.tile(mlstm_norm_g, (1, MLSTM_HEADS))

    xs = jnp.concatenate([ctx, x], axis=1)
    t = xs.shape[1]
    peer = None
    for l in range(depth):
        lam_init = 0.8 - 0.6 * math.exp(-0.3 * l)
        xs, u, mqkv, mo, gcol, grow, dq, dk, dvt = _inproj_call(
            xs, peer, mod_all[l - 1] if l else None, mod_all[l], norm1_g[l][None], w_main[l],
            w_dvt[l], gbias[l], ng, cos, sa, sb)
        conv = _conv_call(u, conv_w[l, :, 0, :], conv_b[l][None], conv_ln_g[l][None], conv_ln_b[l][None])
        gcol_d = gcol.reshape(b, t, 2, ng // 2).transpose(0, 2, 1, 3)
        grow_d = grow.reshape(b, 2, ng // 2, t)
        hf, hb = _mlstm_call(mqkv, gcol_d, grow_d)
        dyc = _attn_call(dq, dk, dvt, diff_lambda[l], diff_norm_g[l][:, None], lam_init, ctx=True)
        dyl = _attn_call(dq, dk, dvt, diff_lambda[l], diff_norm_g[l][:, None], lam_init, ctx=False)
        xs, h2, h2t = _outproj_call(xs, conv, hf, hb, mo, dyc, dyl, mod_all[l], mng[l][None], norm2_g[l][None],
                                    w_out_b[l])
        h2f = h2.reshape(b * t, d)
        a, ni, r1, bv = _peer_sel_call(h2f, wq_b[l], kh[l])
        peer = _peer_dense_call(h2t, a, ni, r1, bv, u_b[l], vt_b[l]).reshape(b, t, d)
    return _final_call(xs, peer, mod_all[depth - 1], final_g[None], n_ctx // TILE)
```

```python
import functools
import math

import jax
import jax.numpy as jnp
from jax import lax
from jax.experimental import pallas as pl
from jax.experimental.pallas import tpu as pltpu

F32 = jnp.float32
BF16 = jnp.bfloat16
HIGHEST = lax.Precision.HIGHEST

GRID_W = 64
EPS = 1e-6
CONV_K = 31
MLSTM_DH = 64
MLSTM_HEADS = 4
DIFF_DH = 64
DIFF_DV = 128
DIFF_HEADS = 4
ROPE_BASE = 10000.0
PEER_HEADS = 8
N_KEYS = 128
PEER_TOPK = 16

LANES = 128
SUBLANES = 8
TILE = 256
CONV_HALO = 16
ATT_KC = 1024
ATT_TQ = 1024
PEER_SEL_TILE = 256
PEER_TM = 1024
BF16_SUBLANES = 16
PEER_EC = 2048
VMEM_LIMIT = 56 * 1024 * 1024

NT_DIMS = (((1,), (1,)), ((), ()))


def _cparams(sem):
    return pltpu.CompilerParams(dimension_semantics=sem, vmem_limit_bytes=VMEM_LIMIT)


def _rms(x, eps=EPS):
    return x * lax.rsqrt(jnp.mean(x * x, axis=-1, keepdims=True) + eps)


def _log_sigmoid(x):
    return jnp.minimum(x, 0.0) - jnp.log(1.0 + jnp.exp(-jnp.abs(x)))


def _ada_kernel(c_ref, w_ref, b_ref, o_ref):
    c = c_ref[...]
    s = (c * jax.nn.sigmoid(c)).astype(BF16)
    o_ref[...] = jnp.dot(s, w_ref[...].astype(BF16), preferred_element_type=F32) + b_ref[...]


def _ada_call(cvec, ada_w, ada_b):
    depth, d, n = ada_w.shape
    tn = 1536
    rows = cvec.shape[0]
    return pl.pallas_call(
        _ada_kernel,
        grid=(depth, n // tn),
        in_specs=[pl.BlockSpec((rows, d), lambda l, j: (0, 0)),
                  pl.BlockSpec((None, d, tn), lambda l, j: (l, 0, j)),
                  pl.BlockSpec((None, 1, tn), lambda l, j: (l, 0, j))],
        out_specs=pl.BlockSpec((None, rows, tn), lambda l, j: (l, 0, j)),
        out_shape=jax.ShapeDtypeStruct((depth, rows, n), F32),
        compiler_params=_cparams(("parallel", "parallel")),
        name="ada_mod",
    )(cvec, ada_w, ada_b.reshape(depth, 1, n))


def _rope(t, c, sa, sb):
    w = t.shape[1]
    rep = w // LANES
    c, sa, sb = (jnp.concatenate([z] * rep, axis=1) for z in (c, sa, sb))
    return t * c + pltpu.roll(t, w - 1, 1) * sa + pltpu.roll(t, 1, 1) * sb


def _inproj_kernel(*refs, has_peer, cw, mw, bb):
    if has_peer:
        x_ref, peer_ref, modp_ref = refs[:3]
        refs = refs[3:]
    (mod_ref, n1g_ref, wm_ref, wdvt_ref, gb_ref, cos_ref, sa_ref, sb_ref,
     xo_ref, u_ref, mqkv_ref, mo_ref, gcol_ref, grow_ref, dq_ref, dk_ref, dvt_ref) = refs[-17:]
    if not has_peer:
        x_ref = refs[0]
    hs = []
    for k in range(bb):
        x = x_ref[k]
        if has_peer:
            x = x + modp_ref[k, 5:6, :] * peer_ref[k]
        xo_ref[k] = x
        hs.append(_rms(x) * n1g_ref[...] * (1.0 + mod_ref[k, 1:2, :]) + mod_ref[k, 0:1, :])
    hb = jnp.concatenate(hs, axis=0).astype(BF16)

    def proj(lo, hi):
        return jnp.dot(hb, wm_ref[:, lo:hi], preferred_element_type=F32)

    def rows(z, k):
        return z[k * TILE:(k + 1) * TILE]

    a = proj(0, 2 * cw)
    u = a[:, :cw] * jax.nn.sigmoid(a[:, cw:])
    o = 2 * cw
    mq = proj(o, o + mw).astype(BF16)
    mk = (proj(o + mw, o + 2 * mw) * (MLSTM_DH ** -0.5)).astype(BF16)
    mv = proj(o + 2 * mw, o + 3 * mw).astype(BF16)
    mo = proj(o + 3 * mw, o + 4 * mw)
    o = o + 4 * mw
    dw = dq_ref.shape[2]
    c, sa, sb = cos_ref[...], sa_ref[...], sb_ref[...]
    pq = proj(o, o + dw)
    pk = proj(o + dw, o + 2 * dw)
    dvt = lax.dot_general(wdvt_ref[...], hb, NT_DIMS, preferred_element_type=F32).astype(BF16)
    o = o + 2 * dw
    ng = gcol_ref.shape[2]
    g = proj(o, o + LANES) + gb_ref[...]
    cidx = lax.broadcasted_iota(jnp.int32, g.shape, 1)
    g = jnp.where((cidx // MLSTM_HEADS) % 2 == 1, _log_sigmoid(g), g)
    for k in range(bb):
        u_ref[k] = rows(u, k)
        mqkv_ref[k, :, 0:mw] = rows(mq, k)
        mqkv_ref[k, :, mw:2 * mw] = rows(mk, k)
        mqkv_ref[k, :, 2 * mw:3 * mw] = rows(mv, k)
        mo_ref[k] = rows(mo, k)
        dq_ref[k] = (_rope(rows(pq, k), c, sa, sb) * (DIFF_DH ** -0.5 * math.log2(math.e))).astype(BF16)
        dk_ref[k] = _rope(rows(pk, k), c, sa, sb).astype(BF16)
        dvt_ref[k] = dvt[:, k * TILE:(k + 1) * TILE]
        gk = rows(g, k)
        gcol_ref[k] = gk[:, :ng]
        grow_ref[k] = gk.T[:ng, :]


def _batch_block(b):
    return 4 if b % 4 == 0 else (2 if b % 2 == 0 else 1)


def _inproj_call(x, peer, modp, mod, n1g, wm, wdvt, gb, ng, cos, sa, sb):
    b, t, d = x.shape
    nt = t // TILE
    bb = _batch_block(b)
    cw = d // 4
    mw = d // 4
    dw = d // 2
    has_peer = peer is not None
    tok = lambda bi, i: (bi, i, 0)
    modspec = pl.BlockSpec((bb, None, 6, d), lambda bi, i: (bi, jnp.minimum(i, 1), 0, 0))
    full2 = lambda arr: pl.BlockSpec(arr.shape, lambda bi, i: (0, 0))
    in_specs = [pl.BlockSpec((bb, TILE, d), tok)]
    args = [x]
    if has_peer:
        in_specs += [pl.BlockSpec((bb, TILE, d), tok), modspec]
        args += [peer, modp]
    in_specs += [modspec, full2(n1g), full2(wm), full2(wdvt), full2(gb)]
    args += [mod, n1g, wm, wdvt, gb]
    in_specs += [pl.BlockSpec((TILE, LANES), lambda bi, i: (i, 0))] * 3
    args += [cos, sa, sb]
    out_shape = [jax.ShapeDtypeStruct((b, t, d), F32),
                 jax.ShapeDtypeStruct((b, t, cw), F32),
                 jax.ShapeDtypeStruct((b, t, 3 * mw), BF16),
                 jax.ShapeDtypeStruct((b, t, mw), F32),
                 jax.ShapeDtypeStruct((b, t, ng), F32),
                 jax.ShapeDtypeStruct((b, ng, t), F32),
                 jax.ShapeDtypeStruct((b, t, dw), BF16),
                 jax.ShapeDtypeStruct((b, t, dw), BF16),
                 jax.ShapeDtypeStruct((b, dw, t), BF16)]
    out_specs = [pl.BlockSpec((bb, TILE, d), tok),
                 pl.BlockSpec((bb, TILE, cw), tok),
                 pl.BlockSpec((bb, TILE, 3 * mw), tok),
                 pl.BlockSpec((bb, TILE, mw), tok),
                 pl.BlockSpec((bb, TILE, ng), tok),
                 pl.BlockSpec((bb, ng, TILE), lambda bi, i: (bi, 0, i)),
                 pl.BlockSpec((bb, TILE, dw), tok),
                 pl.BlockSpec((bb, TILE, dw), tok),
                 pl.BlockSpec((bb, dw, TILE), lambda bi, i: (bi, 0, i))]
    return pl.pallas_call(
        functools.partial(_inproj_kernel, has_peer=has_peer, cw=cw, mw=mw, bb=bb),
        grid=(b // bb, nt), in_specs=in_specs, out_specs=out_specs, out_shape=out_shape,
        compiler_params=_cparams(("parallel", "parallel")),
        name="in_proj",
    )(*args)


def _conv_kernel(up_ref, uc_ref, un_ref, w_ref, b_ref, lg_ref, lb_ref, o_ref, ext_ref, sh_ref):
    i = pl.program_id(1)
    nt = pl.num_programs(1)
    lm = jnp.where(i >= 2, 1.0, 0.0)
    rm = jnp.where(jnp.logical_and(i >= 1, i < nt - 1), 1.0, 0.0)
    hl = CONV_HALO
    ext_ref[0:hl, :] = up_ref[TILE - hl:TILE, :] * lm
    ext_ref[hl:hl + TILE, :] = uc_ref[...]
    ext_ref[hl + TILE:2 * hl + TILE, :] = un_ref[0:hl, :] * rm
    off = hl - CONV_K // 2
    span = sh_ref.shape[1]
    acc = jnp.zeros(uc_ref.shape, F32)
    for r in range(SUBLANES):
        taps = [k for k in range(CONV_K) if (off + k) % SUBLANES == r]
        if not taps:
            continue
        sh_ref[r] = ext_ref[r:r + span, :]
        for k in taps:
            q = (off + k) // SUBLANES * SUBLANES
            acc = acc + w_ref[k:k + 1, :] * sh_ref[r, q:q + TILE, :]
    y = acc + b_ref[...]
    mu = jnp.mean(y, axis=-1, keepdims=True)
    yc = y - mu
    var = jnp.mean(yc * yc, axis=-1, keepdims=True)
    z = yc * lax.rsqrt(var + EPS) * lg_ref[...] + lb_ref[...]
    o_ref[...] = (z * jax.nn.sigmoid(z)).astype(o_ref.dtype)


def _conv_call(u, w, bias, lg, lb):
    b, t, cw = u.shape
    nt = t // TILE
    full2 = lambda arr: pl.BlockSpec(arr.shape, lambda bi, i: (0, 0))
    return pl.pallas_call(
        _conv_kernel,
        grid=(b, nt),
        in_specs=[pl.BlockSpec((None, TILE, cw), lambda bi, i: (bi, jnp.maximum(i - 1, 0), 0)),
                  pl.BlockSpec((None, TILE, cw), lambda bi, i: (bi, i, 0)),
                  pl.BlockSpec((None, TILE, cw), lambda bi, i: (bi, jnp.minimum(i + 1, nt - 1), 0)),
                  full2(w), full2(bias), full2(lg), full2(lb)],
        out_specs=pl.BlockSpec((None, TILE, cw), lambda bi, i: (bi, i, 0)),
        out_shape=jax.ShapeDtypeStruct((b, t, cw), BF16),
        scratch_shapes=[pltpu.VMEM((TILE + 2 * CONV_HALO, cw), F32),
                        pltpu.VMEM((SUBLANES, TILE + 2 * CONV_HALO - SUBLANES, cw), F32)],
        compiler_params=_cparams(("parallel", "parallel")),
        name="conv_module",
    )(u, u, u, w, bias, lg, lb)


def _mlstm_chunk(qkv_ref, gc_ref, gr_ref, h_ref, c_ref, n_ref, m_ref, fwd):
    tc = qkv_ref.shape[0]
    mw = qkv_ref.shape[1] // 3
    nh = MLSTM_HEADS
    dh = MLSTM_DH
    row = lax.broadcasted_iota(jnp.int32, (tc, tc), 0)
    col = lax.broadcasted_iota(jnp.int32, (tc, tc), 1)
    tri = row >= col if fwd else row <= col
    vis = row <= col if fwd else row >= col
    trif = tri.astype(F32)
    gc = gc_ref[...]
    gr = gr_ref[...]
    bcol = jnp.dot(trif, gc, precision=HIGHEST, preferred_element_type=F32)
    brow = lax.dot_general(gr, trif, NT_DIMS, precision=HIGHEST, preferred_element_type=F32)
    bl = jnp.sum(gr, axis=1, keepdims=True)
    src = gc[:, 0:nh] - bcol[:, nh:2 * nh]

    q = qkv_ref[:, 0:mw]
    k = qkv_ref[:, mw:2 * mw]
    v = qkv_ref[:, 2 * mw:3 * mw]
    qf = q.astype(F32)
    vt = v.astype(F32).T
    vt_b = vt.astype(BF16)
    lane_head = lax.broadcasted_iota(jnp.int32, (tc, mw), 1) // dh
    rhead = lax.broadcasted_iota(jnp.int32, (mw, 1), 0) // dh
    chead = lax.broadcasted_iota(jnp.int32, (1, mw), 1) // dh
    cb = c_ref[...]
    n_old = n_ref[...]
    inter_c = lax.dot_general(cb.astype(BF16), q, NT_DIMS, preferred_element_type=F32)
    zrow = jnp.zeros((1, mw), F32)
    n4 = jnp.concatenate([jnp.where(chead == h, n_old, 0.0) for h in range(nh)] + [zrow] * (SUBLANES - nh),
                         axis=0)
    qn = lax.dot_general(n4, qf, NT_DIMS, precision=HIGHEST, preferred_element_type=F32)

    outs, vws, wks = [], [], []
    decay_col = jnp.zeros((mw, 1), F32)
    decay_row = jnp.zeros((1, mw), F32)
    for h in range(nh):
        hr = slice(h * dh, (h + 1) * dh)
        mh = m_ref[h:h + 1, 0:1]
        bt = brow[nh + h:nh + h + 1, :]
        dlog = jnp.where(vis, bt + src[:, h:h + 1], -jnp.inf)
        inter = bt + mh
        mt = jnp.maximum(inter, jnp.max(dlog, axis=0, keepdims=True))
        dwt = jnp.exp(dlog - mt)
        iw = jnp.exp(inter - mt)
        qh = jnp.where(lane_head == h, qf, 0.0).astype(BF16)
        s = lax.dot_general(k, qh, NT_DIMS, preferred_element_type=F32) * dwt
        num = jnp.dot(vt_b[hr], s.astype(BF16), preferred_element_type=F32)
        den = jnp.sum(s, axis=0, keepdims=True) + iw * qn[h:h + 1, :]
        denom = jnp.maximum(jnp.abs(den), jnp.exp(-mt))
        outs.append((num + iw * inter_c[hr]) / denom)
        blh = bl[nh + h:nh + h + 1, :]
        wlog = blh - bt + gr[h:h + 1, :]
        mn = jnp.maximum(blh + mh, jnp.max(wlog, axis=1, keepdims=True))
        decay = jnp.exp(blh + mh - mn)
        wk = jnp.exp(wlog - mn)
        wks.append(wk)
        vws.append(vt[hr] * wk)
        decay_col = jnp.where(rhead == h, decay, decay_col)
        decay_row = jnp.where(chead == h, decay, decay_row)
        m_ref[h:h + 1, :] = jnp.broadcast_to(mn, (1, m_ref.shape[1]))
    h_ref[...] = jnp.concatenate(outs, axis=0).T

    vw = jnp.concatenate(vws, axis=0).astype(BF16)
    upd = jnp.dot(vw, k, preferred_element_type=F32)
    c_ref[...] = decay_col * cb + jnp.where(rhead == chead, upd, 0.0)
    wk4 = jnp.concatenate(wks + [jnp.zeros((1, tc), F32)] * (SUBLANES - nh), axis=0)
    nk = jnp.dot(wk4, k.astype(F32), precision=HIGHEST, preferred_element_type=F32)
    n_add = zrow
    for h in range(nh):
        n_add = jnp.where(chead == h, nk[h:h + 1, :], n_add)
    n_ref[...] = decay_row * n_old + n_add


def _mlstm_kernel(qkvf_ref, gcf_ref, grf_ref, qkvb_ref, gcb_ref, grb_ref, hf_ref, hb_ref,
                  c_ref, n_ref, m_ref):
    @pl.when(pl.program_id(1) == 0)
    def _():
        c_ref[...] = jnp.zeros_like(c_ref)
        n_ref[...] = jnp.zeros_like(n_ref)
        m_ref[...] = jnp.zeros_like(m_ref)

    _mlstm_chunk(qkvf_ref, gcf_ref, grf_ref, hf_ref, c_ref.at[0], n_ref.at[0], m_ref.at[0], True)
    _mlstm_chunk(qkvb_ref, gcb_ref, grb_ref, hb_ref, c_ref.at[1], n_ref.at[1], m_ref.at[1], False)


def _mlstm_call(mqkv, gcol, grow):
    b, t, w3 = mqkv.shape
    mw = w3 // 3
    nt = t // TILE
    ng = gcol.shape[-1]

    def rev(j):
        return jnp.where(j == 0, 0, nt - j)

    hshape = jax.ShapeDtypeStruct((b, t, mw), F32)
    return pl.pallas_call(
        _mlstm_kernel,
        grid=(b, nt),
        in_specs=[pl.BlockSpec((None, TILE, w3), lambda bi, j: (bi, j, 0)),
                  pl.BlockSpec((None, None, TILE, ng), lambda bi, j: (bi, 0, j, 0)),
                  pl.BlockSpec((None, None, ng, TILE), lambda bi, j: (bi, 0, 0, j)),
                  pl.BlockSpec((None, TILE, w3), lambda bi, j: (bi, rev(j), 0)),
                  pl.BlockSpec((None, None, TILE, ng), lambda bi, j: (bi, 1, rev(j), 0)),
                  pl.BlockSpec((None, None, ng, TILE), lambda bi, j: (bi, 1, 0, rev(j)))],
        out_specs=[pl.BlockSpec((None, TILE, mw), lambda bi, j: (bi, j, 0)),
                   pl.BlockSpec((None, TILE, mw), lambda bi, j: (bi, rev(j), 0))],
        out_shape=[hshape, hshape],
        scratch_shapes=[pltpu.VMEM((2, mw, mw), F32), pltpu.VMEM((2, 1, mw), F32),
                        pltpu.VMEM((2, SUBLANES, LANES), F32)],
        compiler_params=_cparams(("arbitrary", "arbitrary")),
        name="mlstm_scan",
    )(mqkv, gcol, grow, mqkv, gcol, grow)


def _attn_kernel(*refs, lam_init, kc, nq):
    dl_ref, ng_ref = refs[:2]
    q_refs = refs[2:2 + nq]
    k_ref, vt_ref, o_ref, s_ref, m_ref, l_ref, acc_ref = refs[2 + nq:]
    n_keys = k_ref.shape[0]
    q = jnp.concatenate([r[...] for r in q_refs], axis=0) if nq > 1 else q_refs[0][...]
    lane = lax.broadcasted_iota(jnp.int32, q.shape, 1)
    zero = jnp.zeros_like(q)
    qz = (jnp.where(lane < DIFF_DH, q, zero), jnp.where(lane >= DIFF_DH, q, zero))
    m_ref[...] = jnp.full(m_ref.shape, -jnp.inf, F32)
    l_ref[...] = jnp.zeros_like(l_ref)
    acc_ref[...] = jnp.zeros_like(acc_ref)

    def scores(start, slot):
        kb = k_ref[pl.ds(start, kc), :]
        for c in range(2):
            s_ref[slot, c] = lax.dot_general(kb, qz[c], NT_DIMS, preferred_element_type=F32)

    def softmax_pv(start, slot):
        vtb = vt_ref[:, pl.ds(start, kc)]
        for c in range(2):
            s = s_ref[slot, c]
            m_old = m_ref[c]
            m_new = jnp.maximum(m_old, jnp.max(s, axis=0, keepdims=True))
            alpha = jnp.exp2(m_old - m_new)
            p = jnp.exp2(s - m_new)
            l_ref[c] = alpha * l_ref[c] + jnp.sum(p, axis=0, keepdims=True)
            acc_ref[c] = alpha * acc_ref[c] + jnp.dot(vtb, p.astype(BF16), preferred_element_type=F32)
            m_ref[c] = m_new

    n = n_keys // kc
    pairs = (n - 1) // 2
    scores(0, 0)
    if pairs:
        def body(jj, carry):
            base = pl.multiple_of(jj * (2 * kc), TILE)
            scores(base + kc, 1)
            softmax_pv(base, 0)
            scores(base + 2 * kc, 0)
            softmax_pv(base + kc, 1)
            return carry

        lax.fori_loop(0, pairs, body, 0)
    done = 2 * pairs
    if n - done == 2:
        scores((done + 1) * kc, 1)
    softmax_pv(done * kc, 0)
    if n - done == 2:
        softmax_pv((done + 1) * kc, 1)

    dl = dl_ref[...]
    lam = (jnp.exp(jnp.sum(dl[0:1] * dl[1:2], axis=1, keepdims=True))
           - jnp.exp(jnp.sum(dl[2:3] * dl[3:4], axis=1, keepdims=True)) + lam_init)
    ot = acc_ref[0] / l_ref[0] - lam * (acc_ref[1] / l_ref[1])
    ms = jnp.mean(ot * ot, axis=0, keepdims=True)
    y = ot * lax.rsqrt(ms + EPS) * (ng_ref[...] * (1.0 - lam_init))
    o_ref[...] = y.T.astype(o_ref.dtype)


def _attn_key_chunk(t):
    return max(k for k in range(TILE, ATT_KC + 1, TILE) if t % k == 0)


def _attn_call(dq, dk, dvt, dlam, dng, lam_init, *, ctx):
    b, t, w = dq.shape
    nh = w // DIFF_DV
    if ctx:
        n_keys, nq, steps, q0 = TILE, 1, 1, 0
    else:
        n_keys, nq, q0 = t, ATT_TQ // TILE, 1
        steps = (t - TILE) // ATT_TQ
    tq = nq * TILE
    kc = _attn_key_chunk(n_keys)
    kern = functools.partial(_attn_kernel, lam_init=lam_init, kc=kc, nq=nq)
    q_specs = [pl.BlockSpec((None, TILE, DIFF_DV), functools.partial(
        lambda bi, h, i, j: (bi, q0 + nq * i + j, h), j=j)) for j in range(nq)]
    return pl.pallas_call(
        kern,
        grid=(b, nh, steps),
        in_specs=[pl.BlockSpec(dlam.shape, lambda bi, h, i: (0, 0)),
                  pl.BlockSpec(dng.shape, lambda bi, h, i: (0, 0))] + q_specs + [
                  pl.BlockSpec((None, n_keys, DIFF_DV), lambda bi, h, i: (bi, 0, h)),
                  pl.BlockSpec((None, DIFF_DV, n_keys), lambda bi, h, i: (bi, h, 0))],
        out_specs=pl.BlockSpec((None, tq, DIFF_DV), lambda bi, h, i: (bi, i, h)),
        out_shape=jax.ShapeDtypeStruct((b, steps * tq, w), BF16),
        scratch_shapes=[pltpu.VMEM((2, 2, kc, tq), F32),
                        pltpu.VMEM((2, 1, tq), F32), pltpu.VMEM((2, 1, tq), F32),
                        pltpu.VMEM((2, DIFF_DV, tq), F32)],
        compiler_params=_cparams(("parallel", "parallel", "arbitrary")),
        name="diff_attn_ctx" if ctx else "diff_attn",
    )(dlam, dng, *([dq] * nq), dk, dvt)


def _outproj_kernel(x_ref, conv_ref, hf_ref, hb_ref, mo_ref, dyc_ref, dyl_ref, mod_ref, mng_ref, n2g_ref,
                    wo_ref, xo_ref, h2_ref, h2t_ref):
    cw = conv_ref.shape[1]
    mw = hf_ref.shape[1]
    hh = hf_ref[...] + hb_ref[...]
    r = lax.broadcasted_iota(jnp.int32, (mw, mw), 0) // MLSTM_DH
    c = lax.broadcasted_iota(jnp.int32, (mw, mw), 1) // MLSTM_DH
    gm = jnp.where(r == c, 1.0 / MLSTM_DH, 0.0)
    ms = jnp.dot(hh * hh, gm, precision=HIGHEST, preferred_element_type=F32)
    ym = hh * lax.rsqrt(ms + EPS) * mng_ref[...] * jax.nn.sigmoid(mo_ref[...])
    dy = jnp.where(pl.program_id(1) == 0, dyc_ref[...], dyl_ref[...])
    o = (jnp.dot(conv_ref[...], wo_ref[0:cw, :], preferred_element_type=F32)
         + jnp.dot(ym.astype(BF16), wo_ref[cw:cw + mw, :], preferred_element_type=F32)
         + jnp.dot(dy, wo_ref[cw + mw:, :], preferred_element_type=F32))
    x = x_ref[...] + mod_ref[2:3, :] * o
    xo_ref[...] = x
    h2 = _rms(x) * n2g_ref[...] * (1.0 + mod_ref[4:5, :]) + mod_ref[3:4, :]
    h2_ref[...] = h2.astype(BF16)
    h2t_ref[...] = h2.T.astype(BF16)


def _outproj_call(x, conv, hf, hb, mo, dyc, dyl, mod, mng, n2g, wo):
    b, t, d = x.shape
    nt = t // TILE
    cw, mw, dw = conv.shape[-1], mo.shape[-1], dyl.shape[-1]
    tok = lambda bi, i: (bi, i, 0)
    full2 = lambda arr: pl.BlockSpec(arr.shape, lambda bi, i: (0, 0))
    return pl.pallas_call(
        _outproj_kernel,
        grid=(b, nt),
        in_specs=[pl.BlockSpec((None, TILE, d), tok),
                  pl.BlockSpec((None, TILE, cw), tok),
                  pl.BlockSpec((None, TILE, mw), tok),
                  pl.BlockSpec((None, TILE, mw), tok),
                  pl.BlockSpec((None, TILE, mw), tok),
                  pl.BlockSpec((None, TILE, dw), lambda bi, i: (bi, 0, 0)),
                  pl.BlockSpec((None, TILE, dw), lambda bi, i: (bi, jnp.maximum(i - 1, 0), 0)),
                  pl.BlockSpec((None, None, 6, d), lambda bi, i: (bi, jnp.minimum(i, 1), 0, 0)),
                  full2(mng), full2(n2g), full2(wo)],
        out_specs=[pl.BlockSpec((None, TILE, d), tok), pl.BlockSpec((None, TILE, d), tok),
                   pl.BlockSpec((d, TILE), lambda bi, i: (0, bi * nt + i))],
        out_shape=[jax.ShapeDtypeStruct((b, t, d), F32), jax.ShapeDtypeStruct((b, t, d), BF16),
                   jax.ShapeDtypeStruct((d, b * t), BF16)],
        compiler_params=_cparams(("parallel", "parallel")),
        name="out_proj",
    )(x, conv, hf, hb, mo, dyc, dyl, mod, mng, n2g, wo)


def _top_k_rows(s, k):
    idx = lax.broadcasted_iota(jnp.int32, s.shape, 0).astype(F32)
    vals, ids = [], []
    cur = s
    for _ in range(k):
        m = jnp.max(cur, axis=0, keepdims=True)
        am = jnp.min(jnp.where(cur == m, idx, float(s.shape[0])), axis=0, keepdims=True)
        vals.append(m)
        ids.append(am)
        cur = jnp.where(idx == am, -jnp.inf, cur)
    return vals, ids


def _select_exact(s0, s1):
    kk = PEER_TOPK
    tt = s0.shape[1]
    iota_k = lax.broadcasted_iota(jnp.int32, (kk, tt), 0).astype(F32)
    iota_n = lax.broadcasted_iota(jnp.int32, (N_KEYS, tt), 0).astype(F32)
    v0, i0 = _top_k_rows(s0, kk)
    v1, i1 = _top_k_rows(s1, kk)
    st0 = jnp.concatenate(v0, axis=0)
    n = jnp.zeros((kk, tt), F32)
    hv = st0 + v1[0]
    mx = v0[0] + v1[0]
    zsum = jnp.zeros((1, tt), F32)
    for _ in range(kk):
        m = jnp.max(hv, axis=0, keepdims=True)
        a_star = jnp.min(jnp.where(hv == m, iota_k, float(kk)), axis=0, keepdims=True)
        sel = iota_k == a_star
        zsum = zsum + jnp.exp(m - mx)
        n = n + jnp.where(sel, 1.0, 0.0)
        nxt = jnp.full((kk, tt), -jnp.inf, F32)
        for b in range(1, kk):
            nxt = jnp.where(n == float(b), v1[b], nxt)
        hv = jnp.where(sel, st0 + nxt, hv)
    ni = jnp.zeros((N_KEYS, tt), F32)
    r1 = jnp.full((N_KEYS, tt), float(kk), F32)
    for a in range(kk):
        ni = jnp.where(iota_n == i0[a], n[a:a + 1, :], ni)
        r1 = jnp.where(iota_n == i1[a], float(a), r1)
    return jnp.exp(s0 - v0[0]) * (0.5 / zsum), ni, r1, jnp.exp(s1 - v1[0])


def _top_k_ranks(s, k):
    cur = s
    rank = jnp.full(s.shape, float(k), F32)
    vals = []
    for j in range(k):
        m = jnp.max(cur, axis=0, keepdims=True)
        eq = cur == m
        vals.append(m)
        rank = jnp.where(eq, float(j), rank)
        cur = jnp.where(eq, -jnp.inf, cur)
    cnt = jnp.sum(jnp.where(rank < float(k), 1.0, 0.0), axis=0, keepdims=True)
    return vals, rank, cnt


def _select_fast(s0, s1):
    kk = PEER_TOPK
    tt = s0.shape[1]
    v0, rank0, cnt0 = _top_k_ranks(s0, kk)
    v1, rank1, cnt1 = _top_k_ranks(s1, kk)
    half = SUBLANES
    st1 = jnp.concatenate(v1, axis=0)
    st1_h = st1[0:half]
    riota = lax.broadcasted_iota(jnp.int32, (half, tt), 0)
    blocks = [v0[0] + st1]
    for a in range(1, half):
        blocks.append(jnp.where(riota < kk // (a + 1), v0[a] + st1_h, -jnp.inf))
    blocks.append(jnp.concatenate(v0[half:], axis=0) + v1[0])
    p = jnp.concatenate(blocks, axis=0)
    cur = p
    for _ in range(kk):
        m = jnp.max(cur, axis=0, keepdims=True)
        cur = jnp.where(cur == m, -jnp.inf, cur)
    picked = jnp.where(cur != p, 1.0, 0.0)
    mx = v0[0] + v1[0]
    zsum = jnp.sum(picked * jnp.exp(p - mx), axis=0, keepdims=True)
    cntm = jnp.sum(picked, axis=0, keepdims=True)
    n_rows = [jnp.sum(picked[0:kk], axis=0, keepdims=True)]
    for a in range(1, half):
        lo = kk + (a - 1) * half
        n_rows.append(jnp.sum(picked[lo:lo + half], axis=0, keepdims=True))
    last = kk + (half - 1) * half
    n_rows += [picked[last + r:last + r + 1] for r in range(kk - half)]
    ni = jnp.zeros((N_KEYS, tt), F32)
    for a in range(kk):
        ni = jnp.where(rank0 == float(a), n_rows[a], ni)
    want = float(kk)
    bad = jnp.where((cnt0 != want) | (cnt1 != want) | (cntm != want), 1.0, 0.0)
    return jnp.exp(s0 - v0[0]) * (0.5 / zsum), ni, rank1, jnp.exp(s1 - v1[0]), bad


def _peer_sel_kernel(h2_ref, wq_ref, kh_ref, a_ref, ni_ref, r1_ref, bv_ref):
    q = jnp.dot(h2_ref[...], wq_ref[...], preferred_element_type=F32).astype(BF16)
    hw = q.shape[1] // PEER_HEADS

    def head_scores(h):
        st_ = lax.dot_general(kh_ref[h], q[:, h * hw:(h + 1) * hw], NT_DIMS,
                              preferred_element_type=F32)
        return st_[0:N_KEYS], st_[N_KEYS:2 * N_KEYS]

    def store(h, a, ni, r1, bv):
        a_ref[h] = a
        ni_ref[h] = ni
        r1_ref[h] = r1.astype(BF16)
        bv_ref[h] = bv.astype(BF16)

    for h in range(PEER_HEADS):
        s0, s1 = head_scores(h)
        a, ni, r1, bv, bad = _select_fast(s0, s1)
        store(h, a, ni, r1, bv)

        @pl.when(jnp.max(bad) > 0.0)
        def _(h=h, s0=s0, s1=s1):
            store(h, *_select_exact(s0, s1))


def _peer_sel_call(h2, wq, kh):
    ntok, d = h2.shape
    nt = ntok // PEER_SEL_TILE
    shp = jax.ShapeDtypeStruct((PEER_HEADS, N_KEYS, ntok), F32)
    shp_b = jax.ShapeDtypeStruct((PEER_HEADS, N_KEYS, ntok), BF16)
    ospec = pl.BlockSpec((PEER_HEADS, N_KEYS, PEER_SEL_TILE), lambda i: (0, 0, i))
    return pl.pallas_call(
        _peer_sel_kernel,
        grid=(nt,),
        in_specs=[pl.BlockSpec((PEER_SEL_TILE, d), lambda i: (i, 0)),
                  pl.BlockSpec(wq.shape, lambda i: (0, 0)),
                  pl.BlockSpec(kh.shape, lambda i: (0, 0, 0))],
        out_specs=[ospec] * 4,
        out_shape=[shp, shp, shp_b, shp_b],
        compiler_params=_cparams(("parallel",)),
        name="peer_select",
    )(h2, wq, kh)


def _gelu_x2(x):
    return x + x * lax.erf(x * (2.0 ** -0.5))


def _peer_dense_kernel(h2t_ref, a_ref, ni_ref, r1_ref, bv_ref, u_ref, vt_ref, o_ref, acc_ref):
    c = pl.program_id(1)

    @pl.when(c == 0)
    def _():
        acc_ref[...] = jnp.zeros_like(acc_ref)

    tm = h2t_ref.shape[1]
    nib = u_ref.shape[0] // N_KEYS
    rep = N_KEYS // BF16_SUBLANES

    def rows(ref, h, ib):
        r = jnp.broadcast_to(ref[h, ib:ib + 1, :], (BF16_SUBLANES, tm)).astype(BF16)
        return jnp.concatenate([r] * rep, axis=0)

    st_ = jnp.dot(u_ref[...], h2t_ref[...], preferred_element_type=F32)
    blocks = []
    for ib in range(nib):
        g = None
        for h in range(PEER_HEADS):
            bvh = bv_ref[h]
            keep = r1_ref[h] < rows(ni_ref, h, ib)
            term = rows(a_ref, h, ib) * jnp.where(keep, bvh, jnp.zeros_like(bvh))
            g = term if g is None else g + term
        blocks.append(g * _gelu_x2(st_[ib * N_KEYS:(ib + 1) * N_KEYS].astype(BF16)))
    wt = jnp.concatenate(blocks, axis=0)
    acc_ref[...] += jnp.dot(vt_ref[...], wt, preferred_element_type=F32)

    @pl.when(c == pl.num_programs(1) - 1)
    def _():
        o_ref[...] = acc_ref[...].T


def _peer_dense_call(h2t, a, ni, r1, bv, u, vt):
    d, ntok = h2t.shape
    ne = u.shape[0]
    tm, ec = PEER_TM, PEER_EC
    ib = ec // N_KEYS
    rowspec = pl.BlockSpec((PEER_HEADS, ib, tm), lambda t, c: (0, c, t))
    colspec = pl.BlockSpec((PEER_HEADS, N_KEYS, tm), lambda t, c: (0, 0, t))
    return pl.pallas_call(
        _peer_dense_kernel,
        grid=(ntok // tm, ne // ec),
        in_specs=[pl.BlockSpec((d, tm), lambda t, c: (0, t)),
                  rowspec, rowspec, colspec, colspec,
                  pl.BlockSpec((ec, d), lambda t, c: (c, 0)),
                  pl.BlockSpec((d, ec), lambda t, c: (0, c))],
        out_specs=pl.BlockSpec((tm, d), lambda t, c: (t, 0)),
        out_shape=jax.ShapeDtypeStruct((ntok, d), F32),
        scratch_shapes=[pltpu.VMEM((d, tm), F32)],
        compiler_params=_cparams(("parallel", "arbitrary")),
        name="peer_dense",
    )(h2t, a, ni, r1, bv, u, vt)


def _final_kernel(x_ref, peer_ref, mod_ref, g_ref, o_ref):
    x = x_ref[...] + mod_ref[5:6, :] * peer_ref[...]
    o_ref[...] = _rms(x) * g_ref[...]


def _final_call(x, peer, mod, fg, n_ctx_tiles):
    b, t, d = x.shape
    nl = t // TILE - n_ctx_tiles
    tok = lambda bi, i: (bi, i + n_ctx_tiles, 0)
    return pl.pallas_call(
        _final_kernel,
        grid=(b, nl),
        in_specs=[pl.BlockSpec((None, TILE, d), tok), pl.BlockSpec((None, TILE, d), tok),
                  pl.BlockSpec((None, None, 6, d), lambda bi, i: (bi, 1, 0, 0)),
                  pl.BlockSpec(fg.shape, lambda bi, i: (0, 0))],
        out_specs=pl.BlockSpec((None, TILE, d), lambda bi, i: (bi, i, 0)),
        out_shape=jax.ShapeDtypeStruct((b, nl * TILE, d), F32),
        compiler_params=_cparams(("parallel", "parallel")),
        name="final_norm",
    )(x, peer, mod, fg)


def _rope_tables(seq, n_ctx):
    rows = seq // GRID_W
    axis_rot = DIFF_DH // 2
    row = jnp.repeat(jnp.arange(rows), GRID_W).astype(F32)
    col = jnp.tile(jnp.arange(GRID_W), rows).astype(F32)
    inv = ROPE_BASE ** (-jnp.arange(0, axis_rot, 2, dtype=F32) / axis_rot)
    ang = jnp.concatenate([row[:, None] * inv, col[:, None] * inv], axis=-1)
    cos = jnp.repeat(jnp.cos(ang), 2, axis=-1)
    sin = jnp.repeat(jnp.sin(ang), 2, axis=-1)
    even = (jnp.arange(DIFF_DH) % 2 == 0)[None, :]
    sa = jnp.where(even, -sin, 0.0)
    sb = jnp.where(even, 0.0, sin)
    rep = LANES // DIFF_DH

    def full(tab, ctx_val):
        tab = jnp.tile(tab, (1, rep))
        return jnp.concatenate([jnp.full((n_ctx, LANES), ctx_val, F32), tab], axis=0)

    return full(cos, 1.0), full(sa, 0.0), full(sb, 0.0)


def kernel(x, c, ctx, c_ctx, ada_w, ada_b, norm1_g, norm2_g, w_in, conv_w, conv_b, conv_ln_g, conv_ln_b, mlstm_gate_b, mlstm_norm_g, diff_lambda, diff_norm_g, w_out, peer_wq, peer_keys, peer_u, peer_v, final_g):
    b, seq, d = x.shape
    n_ctx = ctx.shape[1]
    depth = ada_w.shape[0]
    cw, mw, dw = d // 4, d // 4, d // 2
    ng = 4 * MLSTM_HEADS
    assert n_ctx == TILE and seq % ATT_TQ == 0 and seq % GRID_W == 0
    assert w_in.shape[-1] == 2 * cw + 4 * mw + ng + 3 * dw
    assert peer_keys.shape[1:] == (PEER_HEADS, 2, N_KEYS, d // PEER_HEADS // 2)
    assert conv_w.shape[1] == CONV_K and (b * (seq + n_ctx)) % PEER_TM == 0

    rows = -(-(b + 1) // SUBLANES) * SUBLANES
    cvec = jnp.zeros((rows, d), F32).at[:b].set(c).at[b].set(c_ctx)
    mods = _ada_call(cvec, ada_w, ada_b)
    mod_lat = mods[:, :b].reshape(depth, b, 1, 6, d)
    mod_ctx = jnp.broadcast_to(mods[:, b].reshape(depth, 1, 1, 6, d), (depth, b, 1, 6, d))
    mod_all = jnp.concatenate([mod_ctx, mod_lat], axis=2)

    cos, sa, sb = _rope_tables(seq, n_ctx)
    g0 = 2 * cw + 4 * mw
    w_main = jnp.concatenate([w_in[:, :, :g0], w_in[:, :, g0 + ng:g0 + ng + 2 * dw], w_in[:, :, g0:g0 + ng],
                              jnp.zeros((depth, d, LANES - ng), F32)], axis=-1).astype(BF16)
    w_dvt = jnp.swapaxes(w_in[:, :, g0 + ng + 2 * dw:], 1, 2).astype(BF16)
    gbias = jnp.pad(mlstm_gate_b.reshape(depth, 1, ng), ((0, 0), (0, 0), (0, LANES - ng)))
    w_out_b = w_out.astype(BF16)
    wq_b = peer_wq.astype(BF16)
    hw = d // PEER_HEADS
    kz = jnp.zeros((depth, PEER_HEADS, N_KEYS, hw // 2), F32)
    kh = jnp.concatenate([jnp.concatenate([peer_keys[:, :, 0], kz], axis=-1),
                          jnp.concatenate([kz, peer_keys[:, :, 1]], axis=-1)], axis=2).astype(BF16)
    u_b = peer_u.astype(BF16)
    vt_b = jnp.swapaxes(peer_v, 1, 2).astype(BF16)
    mng = jnp.tile(mlstm_norm_g, (1, MLSTM_HEADS))

    xs = jnp.concatenate([ctx, x], axis=1)
    t = xs.shape[1]
    peer = None
    for l in range(depth):
        lam_init = 0.8 - 0.6 * math.exp(-0.3 * l)
        xs, u, mqkv, mo, gcol, grow, dq, dk, dvt = _inproj_call(
            xs, peer, mod_all[l - 1] if l else None, mod_all[l], norm1_g[l][None], w_main[l],
            w_dvt[l], gbias[l], ng, cos, sa, sb)
        conv = _conv_call(u, conv_w[l, :, 0, :], conv_b[l][None], conv_ln_g[l][None], conv_ln_b[l][None])
        gcol_d = gcol.reshape(b, t, 2, ng // 2).transpose(0, 2, 1, 3)
        grow_d = grow.reshape(b, 2, ng // 2, t)
        hf, hb = _mlstm_call(mqkv, gcol_d, grow_d)
        dyc = _attn_call(dq, dk, dvt, diff_lambda[l], diff_norm_g[l][:, None], lam_init, ctx=True)
        dyl = _attn_call(dq, dk, dvt, diff_lambda[l], diff_norm_g[l][:, None], lam_init, ctx=False)
        xs, h2, h2t = _outproj_call(xs, conv, hf, hb, mo, dyc, dyl, mod_all[l], mng[l][None], norm2_g[l][None],
                                    w_out_b[l])
        h2f = h2.reshape(b * t, d)
        a, ni, r1, bv = _peer_sel_call(h2f, wq_b[l], kh[l])
        peer = _peer_dense_call(h2t, a, ni, r1, bv, u_b[l], vt_b[l]).reshape(b, t, d)
    return _final_call(xs, peer, mod_all[depth - 1], final_g[None], n_ctx // TILE)
```
